```python
import math
import jax
import jax.numpy as jnp
from jax import lax
import numpy as np

D_MODEL = 1024
BATCH = 32
SEQ = 2048
DEPTH = 4

CTX_LEN = 256
GRID_W = 64
N_MIXERS = 4
N_A = (DEPTH + 3) // 4
N_B = (DEPTH + 2) // 4
N_C = (DEPTH + 1) // 4
N_D = DEPTH // 4

S5_GROUP = 16
S5_GROUPS = D_MODEL // S5_GROUP
S5_STATE = 64
DT_MIN = 0.001
DT_MAX = 0.1

NA_HEAD_DIM = 64
NA_HEADS = D_MODEL // NA_HEAD_DIM
WIN_R = 8
WIN_C = 16

DA_HEAD_DIM = 64
DA_HEADS = D_MODEL // (2 * DA_HEAD_DIM)
Q_BLOCK = 128
ROPE_BASE = 10000.0

HG_EXPAND = 128
HG_HEADS = D_MODEL // HG_EXPAND
HG_DV = D_MODEL // HG_HEADS
HG_CHUNK = 64

N_EXPERTS = 16
EXPERT_FF = 2 * D_MODEL
CAPACITY_FACTOR = 2

EPS = 1e-6
F32 = jnp.float32

kernel_name = 'hybrid_s5_nat_diffattn_hgrn2_ecmoe_dit'


def rmsnorm(x, g):
    xf = x.astype(F32)
    y = xf * lax.rsqrt(jnp.mean(xf * xf, axis=-1, keepdims=True) + EPS)
    return (y * g.astype(F32)).astype(x.dtype)


def s5_discretise(a_re, a_im, log_dt, b_re, b_im):
    a_re, a_im = a_re.astype(F32), a_im.astype(F32)
    dt = jnp.exp(log_dt.astype(F32))[:, None]
    lr, li = a_re * dt, a_im * dt
    cos_li, sin_li = jnp.cos(li), jnp.sin(li)
    mag = jnp.exp(lr)
    ab_re, ab_im = mag * cos_li, mag * sin_li
    nr = jnp.expm1(lr) * cos_li - 2.0 * jnp.sin(0.5 * li) ** 2
    den = a_re * a_re + a_im * a_im
    fr = (nr * a_re + ab_im * a_im) / den
    fi = (ab_im * a_re - nr * a_im) / den
    b_re, b_im = b_re.astype(F32), b_im.astype(F32)
    bb_re = fr[..., None] * b_re - fi[..., None] * b_im
    bb_im = fr[..., None] * b_im + fi[..., None] * b_re
    return ab_re, ab_im, bb_re, bb_im


def _complex_scan_combine(e1, e2):
    a1r, a1i, b1r, b1i = e1
    a2r, a2i, b2r, b2i = e2
    return (a2r * a1r - a2i * a1i, a2r * a1i + a2i * a1r,
            a2r * b1r - a2i * b1i + b2r, a2r * b1i + a2i * b1r + b2i)


def s5_scan(u, h0, ab_re, ab_im, bb_re, bb_im):
    Bn, n, _ = u.shape
    ug = u.astype(F32).reshape(Bn, n, S5_GROUPS, S5_GROUP)
    bu_re = jnp.einsum('blgh,gph->lbgp', ug, bb_re)
    bu_im = jnp.einsum('blgh,gph->lbgp', ug, bb_im)
    if h0 is not None:
        h0r, h0i = h0
        bu_re = bu_re.at[0].add(ab_re * h0r - ab_im * h0i)
        bu_im = bu_im.at[0].add(ab_re * h0i + ab_im * h0r)
    a_re = jnp.broadcast_to(ab_re, (n, 1) + ab_re.shape)
    a_im = jnp.broadcast_to(ab_im, (n, 1) + ab_im.shape)
    _, _, hr, hi = lax.associative_scan(_complex_scan_combine, (a_re, a_im, bu_re, bu_im), axis=0)
    return hr, hi


def s5_readout(hr, hi, c_re, c_im):
    y = jnp.einsum('gkp,lbgp->blgk', c_re.astype(F32), hr) - jnp.einsum('gkp,lbgp->blgk', c_im.astype(F32), hi)
    return y.reshape(y.shape[0], y.shape[1], -1)


def s5_mixer(hc, hl, a_re, a_im, log_dt, b_re, b_im, c_re, c_im, d, w_glu, with_ctx):
    def direction(k, uc, ul):
        ab_re, ab_im, bb_re, bb_im = s5_discretise(a_re[k], a_im[k], log_dt[k], b_re[k], b_im[k])
        cr, ci = s5_scan(uc, None, ab_re, ab_im, bb_re, bb_im)
        lr, li = s5_scan(ul, (cr[-1], ci[-1]), ab_re, ab_im, bb_re, bb_im)
        yl = s5_readout(lr, li, c_re[k], c_im[k])
        yc = s5_readout(cr, ci, c_re[k], c_im[k]) if with_ctx else None
        return yc, yl

    ycf, ylf = direction(0, hc, hl)
    ycb, ylb = direction(1, hc[:, ::-1], hl[:, ::-1])

    def glu(y, u):
        z = jax.nn.gelu(y + d.astype(F32) * u.astype(F32)).astype(u.dtype)
        za, zb = jnp.split(z @ w_glu, 2, axis=-1)
        return za * jax.nn.sigmoid(zb)

    out_l = glu(ylf + ylb[:, ::-1], hl)
    out_c = glu(ycf + ycb[:, ::-1], hc) if with_ctx else None
    return out_c, out_l


def na_mixer(hc, hl, w_qkv, w_o, rpb, with_ctx):
    H, dh = NA_HEADS, NA_HEAD_DIM
    Bn, L, D = hl.shape
    rows = L // GRID_W
    wr = min(WIN_R, rows)
    scale = dh ** -0.5

    def proj(h):
        n = h.shape[1]
        q, k, v = jnp.split(h @ w_qkv, 3, axis=-1)
        return q.reshape(Bn, n, H, dh), k.reshape(Bn, n, H, dh), v.reshape(Bn, n, H, dh)

    qc, kc, vc = proj(hc)
    ql, kl, vl = proj(hl)
    kg = kl.reshape(Bn, rows, GRID_W, H, dh)
    vg = vl.reshape(Bn, rows, GRID_W, H, dh)
    r_q = jnp.arange(rows)
    r0 = jnp.clip(r_q - wr // 2, 0, rows - wr)
    krows = r0[:, None] + jnp.arange(wr)[None, :]
    cq = jnp.arange(GRID_W)
    c0 = jnp.clip(cq - WIN_C // 2, 0, GRID_W - WIN_C)
    col_in = (cq[None, :] >= c0[:, None]) & (cq[None, :] < c0[:, None] + WIN_C)
    dc_idx = jnp.clip(cq[None, :] - cq[:, None] + WIN_C - 1, 0, 2 * WIN_C - 2)
    n_lat = wr * GRID_W

    def row_block(args):
        q_r, rr, krow = args
        k_r = kg[:, krow].reshape(Bn, n_lat, H, dh)
        v_r = vg[:, krow].reshape(Bn, n_lat, H, dh)
        s_lat = jnp.einsum('bqhd,bkhd->bhqk', q_r, k_r).astype(F32) * scale
        s_lat = s_lat.reshape(Bn, H, GRID_W, wr, GRID_W)
        dr_idx = krow - rr + WIN_R - 1
        bias = rpb[:, dr_idx[None, :, None], dc_idx[:, None, :]].astype(F32)
        s_lat = jnp.where(col_in[:, None, :], s_lat + bias, -jnp.inf).reshape(Bn, H, GRID_W, n_lat)
        s_ctx = jnp.einsum('bqhd,bkhd->bhqk', q_r, kc).astype(F32) * scale
        p = jax.nn.softmax(jnp.concatenate([s_lat, s_ctx], axis=-1), axis=-1).astype(vl.dtype)
        return (jnp.einsum('bhqk,bkhd->bqhd', p[..., :n_lat], v_r)
                + jnp.einsum('bhqk,bkhd->bqhd', p[..., n_lat:], vc))

    q_rows = jnp.moveaxis(ql.reshape(Bn, rows, GRID_W, H, dh), 1, 0)
    ob = lax.map(row_block, (q_rows, r_q, krows))
    out_l = jnp.moveaxis(ob, 0, 1).reshape(Bn, L, D) @ w_o
    out_c = None
    if with_ctx:
        p = jax.nn.softmax(jnp.einsum('bqhd,bkhd->bhqk', qc, kc).astype(F32) * scale, axis=-1).astype(vc.dtype)
        out_c = jnp.einsum('bhqk,bkhd->bqhd', p, vc).reshape(Bn, hc.shape[1], D) @ w_o
    return out_c, out_l


def rope_2d_angles(n, dh):
    t = jnp.arange(n)
    row = (t // GRID_W).astype(F32)
    col = (t % GRID_W).astype(F32)
    quarter = dh // 4
    inv = ROPE_BASE ** (-jnp.arange(quarter, dtype=F32) / quarter)
    return row[:, None] * inv[None, :], col[:, None] * inv[None, :]


def apply_rope_2d(x, ang_r, ang_c):
    def rot(t, a):
        cos = jnp.cos(a)[None, :, None, :].astype(t.dtype)
        sin = jnp.sin(a)[None, :, None, :].astype(t.dtype)
        t1, t2 = jnp.split(t, 2, axis=-1)
        return jnp.concatenate([t1 * cos - t2 * sin, t2 * cos + t1 * sin], axis=-1)
    xr, xcl = jnp.split(x, 2, axis=-1)
    return jnp.concatenate([rot(xr, ang_r), rot(xcl, ang_c)], axis=-1)


def diff_attend(q, k, v, lam_full, lam_init, subln):
    s = jnp.einsum('bqhmd,bkhmd->bhmqk', q, k).astype(F32) * (DA_HEAD_DIM ** -0.5)
    p = jax.nn.softmax(s, axis=-1)
    a = (p[:, :, 0] - lam_full * p[:, :, 1]).astype(v.dtype)
    o = jnp.einsum('bhqk,bkhe->bqhe', a, v)
    return rmsnorm(o, subln) * (1.0 - lam_init)


def diff_attn_mixer(hc, hl, w_qkv, w_o, lam, subln, layer_idx, with_ctx):
    H, dh = DA_HEADS, DA_HEAD_DIM
    Bn, L, D = hl.shape

    def proj(h):
        n = h.shape[1]
        q, k, v = jnp.split(h @ w_qkv, 3, axis=-1)
        return q.reshape(Bn, n, H, 2, dh), k.reshape(Bn, n, H, 2, dh), v.reshape(Bn, n, H, 2 * dh)

    qc, kc, vc = proj(hc)
    ql, kl, vl = proj(hl)
    ang_r, ang_c = rope_2d_angles(L, dh)
    rope = lambda t: apply_rope_2d(t.reshape(Bn, L, 2 * H, dh), ang_r, ang_c).reshape(Bn, L, H, 2, dh)
    ql, kl = rope(ql), rope(kl)
    lam = lam.astype(F32)
    lam_init = 0.8 - 0.6 * math.exp(-0.3 * layer_idx)
    lam_full = jnp.exp(jnp.sum(lam[0] * lam[1])) - jnp.exp(jnp.sum(lam[2] * lam[3])) + lam_init
    k_all = jnp.concatenate([kl, kc], axis=1)
    v_all = jnp.concatenate([vl, vc], axis=1)
    n_blk = L // Q_BLOCK
    q_blocks = jnp.moveaxis(ql.reshape(Bn, n_blk, Q_BLOCK, H, 2, dh), 1, 0)
    ob = lax.map(lambda qb: diff_attend(qb, k_all, v_all, lam_full, lam_init, subln), q_blocks)
    out_l = jnp.moveaxis(ob, 0, 1).reshape(Bn, L, D) @ w_o
    out_c = None
    if with_ctx:
        out_c = diff_attend(qc, kc, vc, lam_full, lam_init, subln).reshape(Bn, hc.shape[1], D) @ w_o
    return out_c, out_l


def hgrn2_chunk_scan(q, k, v, logf, s0, with_out):
    Bn, H, n, _ = k.shape
    nc = n // HG_CHUNK
    to_chunks = lambda t: jnp.moveaxis(t.reshape(Bn, H, nc, HG_CHUNK, t.shape[-1]), 2, 0)
    lower = jnp.tril(jnp.ones((HG_CHUNK, HG_CHUNK), dtype=bool))

    def step(S, xs):
        qc, kc, vc, gc = xs
        G = jnp.cumsum(gc, axis=2)
        G_end = G[:, :, -1:, :]
        S_new = (jnp.exp(G_end)[:, :, 0, :, None] * S
                 + jnp.einsum('bhsd,bhse->bhde', kc * jnp.exp(G_end - G), vc))
        if not with_out:
            return S_new, None
        o_inter = jnp.einsum('bhtd,bhde->bhte', qc * jnp.exp(G), S)
        rel = jnp.where(lower[:, :, None], G[:, :, :, None, :] - G[:, :, None, :, :], -jnp.inf)
        att = jnp.einsum('bhtd,bhsd,bhtsd->bhts', qc, kc, jnp.exp(rel))
        return S_new, o_inter + jnp.einsum('bhts,bhse->bhte', att, vc)

    S_fin, o = lax.scan(step, s0, (to_chunks(q), to_chunks(k), to_chunks(v), to_chunks(logf)))
    if not with_out:
        return S_fin, None
    return S_fin, jnp.moveaxis(o, 0, 2).reshape(Bn, H, n, v.shape[-1])


def hgrn2_mixer(hc, hl, w_in, w_o, gnorm, lb, with_ctx):
    H, dk, dv = HG_HEADS, HG_EXPAND, HG_DV
    Bn, L, D = hl.shape
    lb = lb.astype(F32).reshape(1, H, 1, dk)

    def proj(h):
        n = h.shape[1]
        z = (h @ w_in).astype(F32)
        q, ff, fb, i, g = jnp.split(z, 5, axis=-1)
        heads = lambda t, e: t.reshape(Bn, n, H, e).transpose(0, 2, 1, 3)
        logf_f = jnp.log(lb + (1.0 - lb) * jax.nn.sigmoid(heads(ff, dk)))
        logf_b = jnp.log(lb + (1.0 - lb) * jax.nn.sigmoid(heads(fb, dk)))
        return heads(q, dk), logf_f, logf_b, heads(jax.nn.silu(i), dv), g

    qc, lfc, lbc, vc, gc = proj(hc)
    ql, lfl, lbl, vl, gl = proj(hl)
    flip = lambda t: t[:, :, ::-1]
    s0 = jnp.zeros((Bn, H, dk, dv), F32)
    sc_f, oc_f = hgrn2_chunk_scan(qc, -jnp.expm1(lfc), vc, lfc, s0, with_ctx)
    _, ol_f = hgrn2_chunk_scan(ql, -jnp.expm1(lfl), vl, lfl, sc_f, True)
    sc_b, oc_b = hgrn2_chunk_scan(flip(qc), flip(-jnp.expm1(lbc)), flip(vc), flip(lbc), s0, with_ctx)
    _, ol_b = hgrn2_chunk_scan(flip(ql), flip(-jnp.expm1(lbl)), flip(vl), flip(lbl), sc_b, True)

    def readout(o_f, o_b, g, dtype):
        n = o_f.shape[2]
        o = (o_f + flip(o_b)).transpose(0, 2, 1, 3)
        o = rmsnorm(o, gnorm).reshape(Bn, n, D) * jax.nn.silu(g)
        return o.astype(dtype) @ w_o

    out_l = readout(ol_f, ol_b, gl, hl.dtype)
    out_c = readout(oc_f, oc_b, gc, hc.dtype) if with_ctx else None
    return out_c, out_l


def ec_moe(h, w_router, w_gate, w_up, w_down):
    Bn, n, _ = h.shape
    cap = CAPACITY_FACTOR * n // N_EXPERTS
    aff = jax.nn.softmax(jnp.einsum('bnd,de->ben', h, w_router).astype(F32), axis=1)
    gate, idx = lax.top_k(aff, cap)
    bidx = jnp.arange(Bn)[:, None, None]
    xs = h[bidx, idx]
    hid = jax.nn.silu(jnp.einsum('becd,edf->becf', xs, w_gate)) * jnp.einsum('becd,edf->becf', xs, w_up)
    out = jnp.einsum('becf,efd->becd', hid, w_down) * gate[..., None].astype(h.dtype)
    return jnp.zeros_like(h).at[bidx, idx].add(out)


def setup_inputs(seed: int = 0) -> dict:
    key = jax.random.key(seed)
    ks = list(jax.random.split(key, 40))
    nrm = lambda shape, s: jax.random.normal(ks.pop(), shape, F32) * s
    D, G, P, GS = D_MODEL, S5_GROUPS, S5_STATE, S5_GROUP
    x = nrm((BATCH, SEQ, D), 1.0)
    c = nrm((BATCH, D), 1.0)
    ctx = nrm((BATCH, CTX_LEN, D), 1.0)
    c_ctx = nrm((D,), 1.0)
    w_mod = nrm((DEPTH, D, 6 * D), 0.5 * D ** -0.5)
    b_mod = nrm((DEPTH, 6 * D), 0.02)
    norm1 = 1.0 + nrm((DEPTH, D), 0.01)
    norm2 = 1.0 + nrm((DEPTH, D), 0.01)
    s5_a_re = -0.5 + nrm((N_A, 2, G, P), 0.01)
    s5_a_im = math.pi * jnp.arange(P, dtype=F32) + nrm((N_A, 2, G, P), 0.01)
    s5_log_dt = jax.random.uniform(ks.pop(), (N_A, 2, G), F32, math.log(DT_MIN), math.log(DT_MAX))
    s5_b_re = nrm((N_A, 2, G, P, GS), (2 * GS) ** -0.5)
    s5_b_im = nrm((N_A, 2, G, P, GS), (2 * GS) ** -0.5)
    s5_c_re = nrm((N_A, 2, G, GS, P), (2 * P) ** -0.5)
    s5_c_im = nrm((N_A, 2, G, GS, P), (2 * P) ** -0.5)
    s5_d = nrm((N_A, D), 1.0)
    s5_w_glu = nrm((N_A, D, 2 * D), D ** -0.5)
    na_w_qkv = nrm((N_B, D, 3 * D), D ** -0.5)
    na_w_o = nrm((N_B, D, D), D ** -0.5)
    na_rpb = nrm((N_B, NA_HEADS, 2 * WIN_R - 1, 2 * WIN_C - 1), 0.02)
    da_w_qkv = nrm((N_C, D, 3 * D), D ** -0.5)
    da_w_o = nrm((N_C, D, D), D ** -0.5)
    da_lambda = nrm((N_C, 4, DA_HEAD_DIM), 0.1)
    da_subln = 1.0 + nrm((N_C, 2 * DA_HEAD_DIM), 0.01)
    hg_w_in = nrm((N_D, D, 5 * D), D ** -0.5)
    hg_w_o = nrm((N_D, D, D), D ** -0.5)
    hg_gnorm = 1.0 + nrm((N_D, HG_DV), 0.01)
    hg_lower_bounds = 1.0 + nrm((DEPTH, D), 0.1)
    moe_router = nrm((DEPTH, D, N_EXPERTS), D ** -0.5)
    moe_w_gate = nrm((DEPTH, N_EXPERTS, D, EXPERT_FF), D ** -0.5)
    moe_w_up = nrm((DEPTH, N_EXPERTS, D, EXPERT_FF), D ** -0.5)
    moe_w_down = nrm((DEPTH, N_EXPERTS, EXPERT_FF, D), EXPERT_FF ** -0.5)
    norm_f = 1.0 + nrm((D,), 0.01)
    return {'x': x, 'c': c, 'ctx': ctx, 'c_ctx': c_ctx,
            'w_mod': w_mod, 'b_mod': b_mod, 'norm1': norm1, 'norm2': norm2,
            's5_a_re': s5_a_re, 's5_a_im': s5_a_im, 's5_log_dt': s5_log_dt,
            's5_b_re': s5_b_re, 's5_b_im': s5_b_im, 's5_c_re': s5_c_re, 's5_c_im': s5_c_im,
            's5_d': s5_d, 's5_w_glu': s5_w_glu,
            'na_w_qkv': na_w_qkv, 'na_w_o': na_w_o, 'na_rpb': na_rpb,
            'da_w_qkv': da_w_qkv, 'da_w_o': da_w_o, 'da_lambda': da_lambda, 'da_subln': da_subln,
            'hg_w_in': hg_w_in, 'hg_w_o': hg_w_o, 'hg_gnorm': hg_gnorm, 'hg_lower_bounds': hg_lower_bounds,
            'moe_router': moe_router, 'moe_w_gate': moe_w_gate, 'moe_w_up': moe_w_up, 'moe_w_down': moe_w_down,
            'norm_f': norm_f}


def reference(x, c, ctx, c_ctx, w_mod, b_mod, norm1, norm2,
              s5_a_re, s5_a_im, s5_log_dt, s5_b_re, s5_b_im, s5_c_re, s5_c_im, s5_d, s5_w_glu,
              na_w_qkv, na_w_o, na_rpb,
              da_w_qkv, da_w_o, da_lambda, da_subln,
              hg_w_in, hg_w_o, hg_gnorm, hg_lower_bounds,
              moe_router, moe_w_gate, moe_w_up, moe_w_down, norm_f):
    p_lb = jax.nn.softmax(hg_lower_bounds.astype(F32), axis=0)
    lbs = jnp.cumsum(p_lb, axis=0) - p_lb[0]
    silu_c = jax.nn.silu(c)
    silu_cc = jax.nn.silu(c_ctx)
    h_ctx = ctx
    for i in range(DEPTH):
        m, j = i % N_MIXERS, i // N_MIXERS
        last = i == DEPTH - 1
        mod_l = jnp.split((silu_c @ w_mod[i] + b_mod[i])[:, None, :], 6, axis=-1)
        mod_c = jnp.split((silu_cc @ w_mod[i] + b_mod[i])[None, None, :], 6, axis=-1)
        hl = rmsnorm(x, norm1[i]) * (1.0 + mod_l[1]) + mod_l[0]
        hc = rmsnorm(h_ctx, norm1[i]) * (1.0 + mod_c[1]) + mod_c[0]
        if m == 0:
            yc, yl = s5_mixer(hc, hl, s5_a_re[j], s5_a_im[j], s5_log_dt[j], s5_b_re[j], s5_b_im[j],
                              s5_c_re[j], s5_c_im[j], s5_d[j], s5_w_glu[j], not last)
        elif m == 1:
            yc, yl = na_mixer(hc, hl, na_w_qkv[j], na_w_o[j], na_rpb[j], not last)
        elif m == 2:
            yc, yl = diff_attn_mixer(hc, hl, da_w_qkv[j], da_w_o[j], da_lambda[j], da_subln[j], i, not last)
        else:
            yc, yl = hgrn2_mixer(hc, hl, hg_w_in[j], hg_w_o[j], hg_gnorm[j], lbs[i], not last)
        x = x + mod_l[2] * yl
        x = x + mod_l[5] * ec_moe(rmsnorm(x, norm2[i]) * (1.0 + mod_l[4]) + mod_l[3],
                                  moe_router[i], moe_w_gate[i], moe_w_up[i], moe_w_down[i])
        if not last:
            h_ctx = h_ctx + mod_c[2] * yc
            h_ctx = h_ctx + mod_c[5] * ec_moe(rmsnorm(h_ctx, norm2[i]) * (1.0 + mod_c[4]) + mod_c[3],
                                              moe_router[i], moe_w_gate[i], moe_w_up[i], moe_w_down[i])
    return rmsnorm(x, norm_f)
```

```python
import functools
import math

import jax
import jax.numpy as jnp
from jax import lax
from jax.experimental import pallas as pl
from jax.experimental.pallas import tpu as pltpu

F32 = jnp.float32
BF16 = jnp.bfloat16
HIGHEST = lax.Precision.HIGHEST
EPS = 1e-6
NEG = -1e30

LANE = 128
GRID_W = 64
WIN_R = 8
WIN_C = 16
HEAD_DIM = 64
ROPE_BASE = 10000.0
S5_GROUP = 16
S5_CHUNK = 16
HG_CHUNK = 64
HG_SUB = 16
N_EXPERTS = 16
CAPACITY_FACTOR = 2
ROW_TILE = 512


def _cparams(n_axes, vmem_mb=48):
    return pltpu.CompilerParams(dimension_semantics=("arbitrary",) * n_axes,
                                vmem_limit_bytes=vmem_mb * 1024 * 1024)


def _dot(a, b):
    return jnp.dot(a, b, preferred_element_type=F32)


def _dot_nt(a, b):
    return lax.dot_general(a, b, (((1,), (1,)), ((), ())), preferred_element_type=F32)


def _normmod(x, g, shift, scale):
    y = x * lax.rsqrt(jnp.mean(x * x, axis=-1, keepdims=True) + EPS)
    return (y * g) * (1.0 + scale) + shift


def _silu(x):
    return x * jax.nn.sigmoid(x)


def _row_tile(n):
    return min(n, ROW_TILE)


def _mod_kernel(c_ref, w_ref, b_ref, o_ref):
    s = _silu(c_ref[...])
    o_ref[0] = jnp.dot(s, w_ref[0], precision=HIGHEST, preferred_element_type=F32) + b_ref[0]


def _modulation(cc, w_mod, b_mod):
    depth, d, d6 = w_mod.shape
    r = cc.shape[0]
    return pl.pallas_call(
        _mod_kernel, grid=(depth, d6 // d),
        in_specs=[pl.BlockSpec((r, d), lambda i, j: (0, 0)),
                  pl.BlockSpec((1, d, d), lambda i, j: (i, 0, j)),
                  pl.BlockSpec((1, 1, d), lambda i, j: (i, 0, j))],
        out_specs=pl.BlockSpec((1, r, d), lambda i, j: (i, 0, j)),
        out_shape=jax.ShapeDtypeStruct((depth, r, d6), F32),
        compiler_params=_cparams(2), name="modulation",
    )(cc, w_mod, b_mod.reshape(depth, 1, d6))


def _proj_in_kernel(x_ref, mod_ref, g_ref, w_ref, *rest, rope_cols):
    o_ref = rest[-1]
    h = _normmod(x_ref[0], g_ref[...], mod_ref[0, 0:1, :], mod_ref[0, 1:2, :])
    y = _dot(h.astype(BF16), w_ref[...])
    if rope_cols:
        cos, s_up, s_dn = rest[0][...], rest[1][...], rest[2][...]
        for j in range(rope_cols // LANE):
            t = y[:, j * LANE:(j + 1) * LANE]
            t = t * cos + pltpu.roll(t, LANE - 16, 1) * s_up + pltpu.roll(t, 16, 1) * s_dn
            o_ref[0, :, j * LANE:(j + 1) * LANE] = t.astype(o_ref.dtype)
        o_ref[0, :, rope_cols:] = y[:, rope_cols:].astype(o_ref.dtype)
    else:
        o_ref[0] = y.astype(o_ref.dtype)


def _proj_in(x, mod, g, w, out_dtype, rope=None, rope_cols=0):
    b, n, d = x.shape
    nout = w.shape[1]
    tm = _row_tile(n)
    in_specs = [pl.BlockSpec((1, tm, d), lambda i, j: (i, j, 0)),
                pl.BlockSpec((1, 6, d), lambda i, j: (i, 0, 0)),
                pl.BlockSpec((1, d), lambda i, j: (0, 0)),
                pl.BlockSpec((d, nout), lambda i, j: (0, 0))]
    args = [x, mod, g.reshape(1, d), w]
    if rope_cols:
        in_specs += [pl.BlockSpec((tm, LANE), lambda i, j: (j, 0))] * 3
        args += list(rope)
    return pl.pallas_call(
        functools.partial(_proj_in_kernel, rope_cols=rope_cols), grid=(b, n // tm),
        in_specs=in_specs,
        out_specs=pl.BlockSpec((1, tm, nout), lambda i, j: (i, j, 0)),
        out_shape=jax.ShapeDtypeStruct((b, n, nout), out_dtype),
        compiler_params=_cparams(2, 56), name="proj_in",
    )(*args)


def _rope_tables(n):
    t = jnp.arange(n)
    row = (t // GRID_W).astype(F32)
    col = (t % GRID_W).astype(F32)
    quarter = HEAD_DIM // 4
    inv = ROPE_BASE ** (-jnp.arange(quarter, dtype=F32) / quarter)
    lane = jnp.arange(LANE)
    pos = jnp.where(((lane % HEAD_DIM) // (HEAD_DIM // 2))[None, :] == 0, row[:, None], col[:, None])
    ang = pos * inv[lane % quarter][None, :]
    first = ((lane % (HEAD_DIM // 2)) < quarter)[None, :]
    cos, sin = jnp.cos(ang), jnp.sin(ang)
    return cos, jnp.where(first, -sin, 0.0), jnp.where(first, 0.0, sin)


def _proj_out_kernel(o_ref, x_ref, mod_ref, w_ref, out_ref):
    out_ref[0] = x_ref[0] + mod_ref[0, 2:3, :] * _dot(o_ref[0], w_ref[...])


def _proj_out(o, x, mod, w):
    b, n, d = x.shape
    tm = _row_tile(n)
    return pl.pallas_call(
        _proj_out_kernel, grid=(b, n // tm),
        in_specs=[pl.BlockSpec((1, tm, d), lambda i, j: (i, j, 0)),
                  pl.BlockSpec((1, tm, d), lambda i, j: (i, j, 0)),
                  pl.BlockSpec((1, 6, d), lambda i, j: (i, 0, 0)),
                  pl.BlockSpec((d, d), lambda i, j: (0, 0))],
        out_specs=pl.BlockSpec((1, tm, d), lambda i, j: (i, j, 0)),
        out_shape=jax.ShapeDtypeStruct((b, n, d), F32),
        compiler_params=_cparams(2), name="proj_out",
    )(o, x, mod, w)


def _softmax_parts(scores):
    m = scores[0].max(axis=-1, keepdims=True)
    for s in scores[1:]:
        m = jnp.maximum(m, s.max(axis=-1, keepdims=True))
    es = [jnp.exp(s - m) for s in scores]
    z = es[0].sum(axis=-1, keepdims=True)
    for e in es[1:]:
        z = z + e.sum(axis=-1, keepdims=True)
    return es, z


def _attn_kernel(*refs, diff, has_lat, scale, post_scale):
    refs = list(refs)
    q_ref = refs.pop(0)
    kv = [(refs.pop(0), refs.pop(0))] if has_lat else []
    kv.append((refs.pop(0), refs.pop(0)))
    if diff:
        lam_ref, sub_ref = refs.pop(0), refs.pop(0)
    o_ref = refs.pop(0)
    tq = q_ref.shape[1]
    lo = lax.broadcasted_iota(jnp.int32, (1, LANE), 1) < HEAD_DIM
    zero = jnp.zeros((), BF16)
    for t in range(q_ref.shape[2] // LANE):
        sl = slice(t * LANE, (t + 1) * LANE)
        qt = q_ref[0, :, sl] * jnp.asarray(scale, BF16)
        q0, q1 = jnp.where(lo, qt, zero), jnp.where(lo, zero, qt)
        ks = [k_ref[0, :, sl] for k_ref, _ in kv]
        vs = [v_ref[0, :, sl] for _, v_ref in kv]
        e0, z0 = _softmax_parts([_dot_nt(q0, k) for k in ks])
        e1, z1 = _softmax_parts([_dot_nt(q1, k) for k in ks])
        r0 = 1.0 / z0
        if diff:
            r1 = lam_ref[0, 0] / z1
            o = jnp.zeros((tq, LANE), F32)
            for a0, a1, v in zip(e0, e1, vs):
                o = o + _dot((a0 * r0 - a1 * r1).astype(BF16), v)
            o = o * lax.rsqrt(jnp.mean(o * o, axis=-1, keepdims=True) + EPS) * sub_ref[...] * post_scale
        else:
            r1 = 1.0 / z1
            o = jnp.zeros((tq, LANE), F32)
            for a0, a1, v in zip(e0, e1, vs):
                o = o + _dot((a0 * r0).astype(BF16), jnp.where(lo, v, zero))
                o = o + _dot((a1 * r1).astype(BF16), jnp.where(lo, zero, v))
        o_ref[0, :, sl] = o.astype(o_ref.dtype)


def _attention(qkv_q, qkv_lat, qkv_ctx, d, diff, lam=None, subln=None, post_scale=1.0):
    b, nq, _ = qkv_q.shape
    tq = min(nq, 256)
    has_lat = qkv_lat is not None
    in_specs = [pl.BlockSpec((1, tq, d), lambda i, j: (i, j, 0))]
    args = [qkv_q]
    for src in ([qkv_lat] if has_lat else []) + [qkv_ctx]:
        nk = src.shape[1]
        in_specs += [pl.BlockSpec((1, nk, d), lambda i, j: (i, 0, 1)),
                     pl.BlockSpec((1, nk, d), lambda i, j: (i, 0, 2))]
        args += [src, src]
    if diff:
        in_specs += [pl.BlockSpec(memory_space=pltpu.SMEM), pl.BlockSpec((1, LANE), lambda i, j: (0, 0))]
        args += [lam.reshape(1, 1), subln.reshape(1, LANE)]
    return pl.pallas_call(
        functools.partial(_attn_kernel, diff=diff, has_lat=has_lat, scale=HEAD_DIM ** -0.5,
                          post_scale=post_scale),
        grid=(b, nq // tq), in_specs=in_specs,
        out_specs=pl.BlockSpec((1, tq, d), lambda i, j: (i, j, 0)),
        out_shape=jax.ShapeDtypeStruct((b, nq, d), BF16),
        compiler_params=_cparams(2, 56), name="attention",
    )(*args)


def _na_kernel(q_ref, k_ref, v_ref, kc_ref, vc_ref, bias_ref, o_ref, *, rows, wr, scale):
    r = pl.program_id(1)
    r0 = jnp.clip(r - wr // 2, 0, rows - wr)
    start = pl.multiple_of(r0 * GRID_W, GRID_W)
    nk = wr * GRID_W
    lo = lax.broadcasted_iota(jnp.int32, (1, LANE), 1) < HEAD_DIM
    zero = jnp.zeros((), BF16)
    for t in range(q_ref.shape[2] // LANE):
        sl = slice(t * LANE, (t + 1) * LANE)
        qt = q_ref[0, :, sl] * jnp.asarray(scale, BF16)
        kw, vw = k_ref[0, pl.ds(start, nk), sl], v_ref[0, pl.ds(start, nk), sl]
        kc, vc = kc_ref[0, :, sl], vc_ref[0, :, sl]
        o = jnp.zeros((GRID_W, LANE), F32)
        for half in range(2):
            keep = lo if half == 0 else jnp.logical_not(lo)
            qh = jnp.where(keep, qt, zero)
            (el, ec), z = _softmax_parts([_dot_nt(qh, kw) + bias_ref[0, 2 * t + half], _dot_nt(qh, kc)])
            rz = 1.0 / z
            o = o + _dot((el * rz).astype(BF16), jnp.where(keep, vw, zero))
            o = o + _dot((ec * rz).astype(BF16), jnp.where(keep, vc, zero))
        o_ref[0, :, sl] = o.astype(o_ref.dtype)


def _na_bias_table(rpb, wr):
    heads = rpb.shape[0]
    cq = jnp.arange(GRID_W)
    c0 = jnp.clip(cq - WIN_C // 2, 0, GRID_W - WIN_C)
    col_in = (cq[None, :] >= c0[:, None]) & (cq[None, :] < c0[:, None] + WIN_C)
    dc = jnp.clip(cq[None, :] - cq[:, None] + WIN_C - 1, 0, 2 * WIN_C - 2)
    dr = jnp.arange(wr)[None, :] + (WIN_R - 1) - jnp.arange(wr)[:, None]
    tab = rpb.astype(F32)[:, dr[:, :, None, None], dc[None, None, :, :]]
    tab = jnp.where(col_in[None, None, None], tab, NEG)
    return tab.transpose(1, 0, 3, 2, 4).reshape(wr, heads, GRID_W, wr * GRID_W)


def _na_attention(qkv_l, qkv_c, bias, d):
    b, n, _ = qkv_l.shape
    nc = qkv_c.shape[1]
    rows = n // GRID_W
    wr = min(WIN_R, rows)
    heads = d // HEAD_DIM

    def bias_map(i, r):
        return (r - jnp.clip(r - wr // 2, 0, rows - wr), 0, 0, 0)

    return pl.pallas_call(
        functools.partial(_na_kernel, rows=rows, wr=wr, scale=HEAD_DIM ** -0.5),
        grid=(b, rows),
        in_specs=[pl.BlockSpec((1, GRID_W, d), lambda i, r: (i, r, 0)),
                  pl.BlockSpec((1, n, d), lambda i, r: (i, 0, 1)),
                  pl.BlockSpec((1, n, d), lambda i, r: (i, 0, 2)),
                  pl.BlockSpec((1, nc, d), lambda i, r: (i, 0, 1)),
                  pl.BlockSpec((1, nc, d), lambda i, r: (i, 0, 2)),
                  pl.BlockSpec((1, heads, GRID_W, wr * GRID_W), bias_map)],
        out_specs=pl.BlockSpec((1, GRID_W, d), lambda i, r: (i, r, 0)),
        out_shape=jax.ShapeDtypeStruct((b, n, d), BF16),
        compiler_params=_cparams(2), name="na_attention",
    )(qkv_l, qkv_l, qkv_l, qkv_c, qkv_c, bias)


def _normmod_kernel(x_ref, mod_ref, g_ref, o_ref):
    o_ref[0] = _normmod(x_ref[0], g_ref[...], mod_ref[0, 0:1, :], mod_ref[0, 1:2, :]).astype(o_ref.dtype)


def _normmod_call(x, mod, g):
    b, n, d = x.shape
    tm = _row_tile(n)
    return pl.pallas_call(
        _normmod_kernel, grid=(b, n // tm),
        in_specs=[pl.BlockSpec((1, tm, d), lambda i, j: (i, j, 0)),
                  pl.BlockSpec((1, 6, d), lambda i, j: (i, 0, 0)),
                  pl.BlockSpec((1, d), lambda i, j: (0, 0))],
        out_specs=pl.BlockSpec((1, tm, d), lambda i, j: (i, j, 0)),
        out_shape=jax.ShapeDtypeStruct((b, n, d), BF16),
        compiler_params=_cparams(2), name="normmod",
    )(x, mod, g.reshape(1, d))


def _s5_tables(a_re, a_im, log_dt, b_re, b_im, c_re, c_im):
    t_len = S5_CHUNK
    a_re, a_im = a_re.astype(F32), a_im.astype(F32)
    dt = jnp.exp(log_dt.astype(F32))[..., None]
    lr, li = a_re * dt, a_im * dt
    cos_li, sin_li = jnp.cos(li), jnp.sin(li)
    ab_im = jnp.exp(lr) * sin_li
    nr = jnp.expm1(lr) * cos_li - 2.0 * jnp.sin(0.5 * li) ** 2
    den = a_re * a_re + a_im * a_im
    fr = (nr * a_re + ab_im * a_im) / den
    fi = (ab_im * a_re - nr * a_im) / den
    b_re, b_im = b_re.astype(F32), b_im.astype(F32)
    bb_re = fr[..., None] * b_re - fi[..., None] * b_im
    bb_im = fr[..., None] * b_im + fi[..., None] * b_re
    tau = jnp.arange(t_len + 1, dtype=F32)[:, None, None, None]
    mag = jnp.exp(tau * lr)
    pr, pi = mag * jnp.cos(tau * li), mag * jnp.sin(tau * li)
    c_re, c_im = c_re.astype(F32), c_im.astype(F32)
    cw_re = c_re[None] * pr[:, :, :, None, :] - c_im[None] * pi[:, :, :, None, :]
    cw_im = c_re[None] * pi[:, :, :, None, :] + c_im[None] * pr[:, :, :, None, :]
    kern = (jnp.einsum('tdgkp,dgph->tdgkh', cw_re, bb_re, precision=HIGHEST)
            - jnp.einsum('tdgkp,dgph->tdgkh', cw_im, bb_im, precision=HIGHEST))
    g = a_re.shape[1]
    h = S5_GROUP
    s_idx = jnp.arange(t_len)[:, None]
    t_idx = jnp.arange(t_len)[None, :]
    k_f = jnp.where((s_idx <= t_idx)[:, :, None, None, None], kern[jnp.clip(t_idx - s_idx, 0, t_len), 0], 0.0)
    k_b = jnp.where((s_idx >= t_idx)[:, :, None, None, None], kern[jnp.clip(s_idx - t_idx, 0, t_len), 1], 0.0)
    m_both = (k_f + k_b).transpose(2, 0, 4, 1, 3).reshape(g, t_len * h, t_len * h)

    def in_map(d, powers):
        wr_ = pr[powers, d][..., None] * bb_re[d][None] - pi[powers, d][..., None] * bb_im[d][None]
        wi_ = pr[powers, d][..., None] * bb_im[d][None] + pi[powers, d][..., None] * bb_re[d][None]
        to_rows = lambda w: w.transpose(1, 0, 3, 2).reshape(g, t_len * h, -1)
        return to_rows(wr_), to_rows(wi_)

    f_re, f_im = in_map(0, t_len - 1 - jnp.arange(t_len))
    r_re, r_im = in_map(1, jnp.arange(t_len))
    w_in = jnp.concatenate([f_re, f_im, f_im, f_re, r_re, r_im, r_im, r_re], axis=-1)

    def out_map(d, powers):
        to_cols = lambda w: w.transpose(1, 3, 0, 2).reshape(g, -1, t_len * h)
        return jnp.concatenate([to_cols(cw_re[powers, d]), -to_cols(cw_im[powers, d])], axis=1)

    w_out = jnp.concatenate([out_map(0, jnp.arange(t_len) + 1), out_map(1, t_len - jnp.arange(t_len))], axis=1)
    a_pow = jnp.stack([jnp.concatenate([pr[t_len, 0], pr[t_len, 0]], -1),
                       jnp.concatenate([-pi[t_len, 0], pi[t_len, 0]], -1),
                       jnp.concatenate([pr[t_len, 1], pr[t_len, 1]], -1),
                       jnp.concatenate([-pi[t_len, 1], pi[t_len, 1]], -1)], axis=1)
    return m_both.astype(BF16), w_in.astype(BF16), w_out.astype(BF16), a_pow


def _s5_core_kernel(u_ref, m_ref, win_ref, wout_ref, a_ref, y_ref, sall_ref, sin_ref, *, bn, nc_c, nc_l):
    u = u_ref[0]
    sin_ref[...] = _dot(u, win_ref[0])
    a = a_ref[0]
    a1f, a2f, a1b, a2b = a[0:1], a[1:2], a[2:3], a[3:4]
    w = LANE
    zero = jnp.zeros((bn, w), F32)

    def rows_of(c):
        return pl.ds(pl.multiple_of(c * bn, bn), bn)

    def fwd(c, carry):
        s, sw = carry
        r = rows_of(c)
        sall_ref[r, 0:w] = s
        return (a1f * s + a2f * sw + sin_ref[r, 0:w], a1f * sw - a2f * s + sin_ref[r, w:2 * w])

    def bwd(c, carry):
        s, sw = carry
        r = rows_of(c)
        sall_ref[r, w:2 * w] = s
        return (a1b * s + a2b * sw + sin_ref[r, 2 * w:3 * w], a1b * sw - a2b * s + sin_ref[r, 3 * w:4 * w])

    lax.fori_loop(0, nc_c + nc_l, fwd, (zero, zero))
    carry = lax.fori_loop(0, nc_c, lambda i, cr: bwd(nc_c - 1 - i, cr), (zero, zero))
    lax.fori_loop(0, nc_l, lambda i, cr: bwd(nc_c + nc_l - 1 - i, cr), carry)
    y_ref[0] = _dot(u, m_ref[0]) + _dot(sall_ref[...].astype(BF16), wout_ref[0])


def _s5_core(u_rows, tables, bn, nc_c, nc_l):
    g, rows, width = u_rows.shape
    m_both, w_in, w_out, a_pow = tables
    return pl.pallas_call(
        functools.partial(_s5_core_kernel, bn=bn, nc_c=nc_c, nc_l=nc_l), grid=(g,),
        in_specs=[pl.BlockSpec((1, rows, width), lambda i: (i, 0, 0)),
                  pl.BlockSpec((1, width, width), lambda i: (i, 0, 0)),
                  pl.BlockSpec((1, width, 4 * LANE), lambda i: (i, 0, 0)),
                  pl.BlockSpec((1, 2 * LANE, width), lambda i: (i, 0, 0)),
                  pl.BlockSpec((1, 4, LANE), lambda i: (i, 0, 0))],
        out_specs=pl.BlockSpec((1, rows, width), lambda i: (i, 0, 0)),
        out_shape=jax.ShapeDtypeStruct((g, rows, width), F32),
        scratch_shapes=[pltpu.VMEM((rows, 2 * LANE), F32), pltpu.VMEM((rows, 4 * LANE), F32)],
        compiler_params=_cparams(1, 56), name="s5_core",
    )(u_rows, m_both, w_in, w_out, a_pow)


def _s5_glu_kernel(x_ref, y_ref, mod_ref, g_ref, d_ref, w_ref, out_ref):
    x = x_ref[0]
    d = x.shape[1]
    u = _normmod(x, g_ref[...], mod_ref[0, 0:1, :], mod_ref[0, 1:2, :])
    z = jax.nn.gelu(y_ref[0] + d_ref[...] * u)
    zz = _dot(z.astype(BF16), w_ref[...])
    out_ref[0] = x + mod_ref[0, 2:3, :] * (zz[:, :d] * jax.nn.sigmoid(zz[:, d:]))


def _s5_glu(x, y, mod, g, dskip, w_glu):
    b, n, d = x.shape
    tm = _row_tile(n)
    return pl.pallas_call(
        _s5_glu_kernel, grid=(b, n // tm),
        in_specs=[pl.BlockSpec((1, tm, d), lambda i, j: (i, j, 0)),
                  pl.BlockSpec((1, tm, d), lambda i, j: (i, j, 0)),
                  pl.BlockSpec((1, 6, d), lambda i, j: (i, 0, 0)),
                  pl.BlockSpec((1, d), lambda i, j: (0, 0)),
                  pl.BlockSpec((1, d), lambda i, j: (0, 0)),
                  pl.BlockSpec((d, 2 * d), lambda i, j: (0, 0))],
        out_specs=pl.BlockSpec((1, tm, d), lambda i, j: (i, j, 0)),
        out_shape=jax.ShapeDtypeStruct((b, n, d), F32),
        compiler_params=_cparams(2), name="s5_glu",
    )(x, y, mod, g.reshape(1, d), dskip.reshape(1, d), w_glu)


def _s5_layer(x, h_ctx, mod_l, mod_c, g, tables, dskip, w_glu):
    b, n, d = x.shape
    n_c = h_ctx.shape[1]
    t_len, h = S5_CHUNK, S5_GROUP
    groups = d // h
    nc_l, nc_c = n // t_len, n_c // t_len

    def to_rows(u, nc):
        return u.reshape(b, nc, t_len, groups, h).transpose(3, 1, 0, 2, 4).reshape(groups, nc * b, t_len * h)

    def from_rows(y, nc):
        return y.reshape(groups, nc, b, t_len, h).transpose(2, 1, 3, 0, 4).reshape(b, nc * t_len, d)

    u_rows = jnp.concatenate([to_rows(_normmod_call(h_ctx, mod_c, g), nc_c),
                              to_rows(_normmod_call(x, mod_l, g), nc_l)], axis=1)
    y_rows = _s5_core(u_rows, tables, b, nc_c, nc_l)
    y_c = from_rows(y_rows[:, :nc_c * b], nc_c)
    y_l = from_rows(y_rows[:, nc_c * b:], nc_l)
    return (_s5_glu(h_ctx, y_c, mod_c, g, dskip, w_glu), _s5_glu(x, y_l, mod_l, g, dskip, w_glu))


def _hg_kernel(ql_ref, ffl_ref, fbl_ref, il_ref, ffc_ref, fbc_ref, ic_ref, lb_ref, tril_ref, triu_ref,
               o_ref, ob_ref, *, nc_c, nc_l):
    cn, sub = HG_CHUNK, HG_SUB
    lb = lb_ref[...]
    tril, triu = tril_ref[...], triu_ref[...]
    tio = lax.broadcasted_iota(jnp.int32, (sub, 1), 0)

    def gates(ff):
        f = lb + (1.0 - lb) * jax.nn.sigmoid(ff)
        return jnp.log(f), 1.0 - f

    def cumulative(lf, fwd):
        return jnp.dot(tril if fwd else triu, lf, precision=HIGHEST, preferred_element_type=F32)

    def total(cum, fwd):
        return cum[cn - 1:cn] if fwd else cum[0:1]

    def update(st, k, v, cum, tot):
        kd = k * jnp.exp(tot - cum)
        return st * jnp.exp(tot) + _dot(v.T.astype(BF16), kd.astype(BF16))

    def diag(qs, ks, vs, cs, fwd):
        acc = jnp.zeros((sub, LANE), F32)
        for s in range(sub):
            keep = (tio >= s) if fwd else (tio <= s)
            e = jnp.exp(jnp.where(keep, cs - cs[s:s + 1], NEG))
            col = jnp.sum(e * qs * ks[s:s + 1], axis=-1, keepdims=True)
            acc = acc + col * vs[s:s + 1]
        return acc

    def chunk_out(q, k, v, cum, st, fwd):
        o = _dot_nt((q * jnp.exp(cum)).astype(BF16), st.astype(BF16))
        vb = v.astype(BF16)
        pieces = []
        for i in range(cn // sub):
            lo_, hi_ = i * sub, (i + 1) * sub
            oi = diag(q[lo_:hi_], k[lo_:hi_], v[lo_:hi_], cum[lo_:hi_], fwd)
            if fwd and i > 0:
                anchor, other = cum[lo_ - 1:lo_], slice(0, lo_)
            elif (not fwd) and hi_ < cn:
                anchor, other = cum[hi_:hi_ + 1], slice(hi_, cn)
            else:
                other = None
            if other is not None:
                qi = (q[lo_:hi_] * jnp.exp(cum[lo_:hi_] - anchor)).astype(BF16)
                kk = (k[other] * jnp.exp(anchor - cum[other])).astype(BF16)
                oi = oi + _dot(_dot_nt(qi, kk).astype(BF16), vb[other])
            pieces.append(oi)
        return o + jnp.concatenate(pieces, axis=0)

    def rows_of(c):
        return pl.ds(pl.multiple_of(c * cn, cn), cn)

    def ctx_body(j, carry):
        sf, sb = carry
        r, rb = rows_of(j), rows_of(nc_c - 1 - j)
        lf, k = gates(ffc_ref[0, r, :])
        cum = cumulative(lf, True)
        sf = update(sf, k, _silu(ic_ref[0, r, :]), cum, total(cum, True))
        lfb, kb = gates(fbc_ref[0, rb, :])
        cumb = cumulative(lfb, False)
        sb = update(sb, kb, _silu(ic_ref[0, rb, :]), cumb, total(cumb, False))
        return sf, sb

    def lat_body(j, carry):
        sf, sb = carry
        r, rb = rows_of(j), rows_of(nc_l - 1 - j)
        lf, k = gates(ffl_ref[0, r, :])
        v, q = _silu(il_ref[0, r, :]), ql_ref[0, r, :]
        cum = cumulative(lf, True)
        o_ref[0, r, :] = chunk_out(q, k, v, cum, sf, True)
        sf = update(sf, k, v, cum, total(cum, True))
        lfb, kb = gates(fbl_ref[0, rb, :])
        vb, qb = _silu(il_ref[0, rb, :]), ql_ref[0, rb, :]
        cumb = cumulative(lfb, False)
        ob_ref[rb, :] = chunk_out(qb, kb, vb, cumb, sb, False)
        sb = update(sb, kb, vb, cumb, total(cumb, False))
        return sf, sb

    zero = jnp.zeros((LANE, LANE), F32)
    carry = lax.fori_loop(0, nc_c, ctx_body, (zero, zero))
    lax.fori_loop(0, nc_l, lat_body, carry)
    o_ref[0] = o_ref[0] + ob_ref[...]


def _hg_scan(z_l, z_c, lb, d):
    b, n, _ = z_l.shape
    n_c = z_c.shape[1]
    heads = d // LANE
    cn = HG_CHUNK
    idx = jnp.arange(cn)
    tril = (idx[None, :] <= idx[:, None]).astype(F32)
    triu = (idx[None, :] >= idx[:, None]).astype(F32)
    col = lambda k: (lambda i, h: (i, 0, k * heads + h))
    lat = lambda k: pl.BlockSpec((1, n, LANE), col(k))
    ctx = lambda k: pl.BlockSpec((1, n_c, LANE), col(k))
    return pl.pallas_call(
        functools.partial(_hg_kernel, nc_c=n_c // cn, nc_l=n // cn), grid=(b, heads),
        in_specs=[lat(0), lat(1), lat(2), lat(3), ctx(1), ctx(2), ctx(3),
                  pl.BlockSpec((1, LANE), lambda i, h: (0, h)),
                  pl.BlockSpec((cn, cn), lambda i, h: (0, 0)),
                  pl.BlockSpec((cn, cn), lambda i, h: (0, 0))],
        out_specs=pl.BlockSpec((1, n, LANE), lambda i, h: (i, 0, h)),
        out_shape=jax.ShapeDtypeStruct((b, n, d), F32),
        scratch_shapes=[pltpu.VMEM((n, LANE), F32)],
        compiler_params=_cparams(2), name="hgrn2_scan",
    )(z_l, z_l, z_l, z_l, z_c, z_c, z_c, lb.reshape(1, d), tril, triu)


def _hg_out_kernel(o_ref, z_ref, gn_ref, x_ref, mod_ref, w_ref, out_ref):
    o, gate = o_ref[0], z_ref[0]
    parts = []
    for h in range(o.shape[1] // LANE):
        sl = slice(h * LANE, (h + 1) * LANE)
        t = o[:, sl]
        t = t * lax.rsqrt(jnp.mean(t * t, axis=-1, keepdims=True) + EPS) * gn_ref[...]
        parts.append((t * _silu(gate[:, sl])).astype(BF16))
    y = _dot(jnp.concatenate(parts, axis=-1), w_ref[...])
    out_ref[0] = x_ref[0] + mod_ref[0, 2:3, :] * y


def _hg_out(o, z, gnorm, x, mod, w):
    b, n, d = x.shape
    tm = _row_tile(n)
    return pl.pallas_call(
        _hg_out_kernel, grid=(b, n // tm),
        in_specs=[pl.BlockSpec((1, tm, d), lambda i, j: (i, j, 0)),
                  pl.BlockSpec((1, tm, d), lambda i, j: (i, j, 4)),
                  pl.BlockSpec((1, LANE), lambda i, j: (0, 0)),
                  pl.BlockSpec((1, tm, d), lambda i, j: (i, j, 0)),
                  pl.BlockSpec((1, 6, d), lambda i, j: (i, 0, 0)),
                  pl.BlockSpec((d, d), lambda i, j: (0, 0))],
        out_specs=pl.BlockSpec((1, tm, d), lambda i, j: (i, j, 0)),
        out_shape=jax.ShapeDtypeStruct((b, n, d), F32),
        compiler_params=_cparams(2), name="hgrn2_out",
    )(o, z, gnorm.reshape(1, LANE), x, mod, w)


def _moe_pre_kernel(x_ref, mod_ref, g_ref, wr_ref, h_ref, aff_ref):
    h = _normmod(x_ref[0], g_ref[...], mod_ref[0, 3:4, :], mod_ref[0, 4:5, :])
    h_ref[0] = h.astype(BF16)
    logits = jnp.dot(h, wr_ref[...], precision=HIGHEST, preferred_element_type=F32)
    lane = lax.broadcasted_iota(jnp.int32, logits.shape, 1)
    logits = jnp.where(lane < N_EXPERTS, logits, NEG)
    e = jnp.exp(logits - logits.max(axis=-1, keepdims=True))
    aff_ref[0] = e / e.sum(axis=-1, keepdims=True)


def _moe_pre(x, mod, g, wr_pad):
    b, n, d = x.shape
    tm = _row_tile(n)
    return pl.pallas_call(
        _moe_pre_kernel, grid=(b, n // tm),
        in_specs=[pl.BlockSpec((1, tm, d), lambda i, j: (i, j, 0)),
                  pl.BlockSpec((1, 6, d), lambda i, j: (i, 0, 0)),
                  pl.BlockSpec((1, d), lambda i, j: (0, 0)),
                  pl.BlockSpec((d, LANE), lambda i, j: (0, 0))],
        out_specs=[pl.BlockSpec((1, tm, d), lambda i, j: (i, j, 0)),
                   pl.BlockSpec((1, tm, LANE), lambda i, j: (i, j, 0))],
        out_shape=[jax.ShapeDtypeStruct((b, n, d), BF16), jax.ShapeDtypeStruct((b, n, LANE), F32)],
        compiler_params=_cparams(2), name="moe_pre",
    )(x, mod, g.reshape(1, d), wr_pad)


def _route_kernel(aff_ref, tri_ref, pos_ref, selw_ref, *, cap):
    aff = aff_ref[0]
    bits = lax.bitcast_convert_type(aff, jnp.int32)
    ones = jnp.ones((), F32)
    zeros = jnp.zeros((), F32)

    def count(mask):
        return jnp.sum(jnp.where(mask, ones, zeros), axis=0, keepdims=True)

    def search(i, thr):
        cand = thr | jnp.left_shift(jnp.int32(1), 30 - i)
        return jnp.where(count(bits >= cand) >= cap, cand, thr)

    thr = lax.fori_loop(0, 31, search, jnp.zeros((1, LANE), jnp.int32))
    above, tie = bits > thr, bits == thr
    need = cap - count(above)
    tie_rank = _dot(tri_ref[...], jnp.where(tie, ones, zeros).astype(BF16))
    lane = lax.broadcasted_iota(jnp.int32, aff.shape, 1)
    sel = (above | (tie & (tie_rank <= need))) & (lane < N_EXPERTS)
    rank = _dot(tri_ref[...], jnp.where(sel, ones, zeros).astype(BF16))
    pos_ref[0] = jnp.where(sel, rank - 1.0, -1.0)
    selw_ref[0] = jnp.where(sel, aff, 0.0)


def _route(aff, tri, cap):
    b, n, _ = aff.shape
    spec = pl.BlockSpec((1, n, LANE), lambda i: (i, 0, 0))
    return pl.pallas_call(
        functools.partial(_route_kernel, cap=cap), grid=(b,),
        in_specs=[spec, pl.BlockSpec((n, n), lambda i: (0, 0))],
        out_specs=[spec, spec],
        out_shape=[jax.ShapeDtypeStruct((b, n, LANE), F32)] * 2,
        compiler_params=_cparams(1), name="moe_route",
    )(aff, tri)


def _gather_kernel(pos_ref, h_ref, xs_ref, *, cap):
    h = h_ref[0]
    slot = lax.broadcasted_iota(jnp.int32, (cap, h.shape[0]), 0).astype(F32)
    for e in range(N_EXPERTS):
        onehot = jnp.where(pos_ref[0, e:e + 1, :] == slot, 1.0, 0.0).astype(BF16)
        xs_ref[e, 0] = _dot(onehot, h).astype(BF16)


def _gather(pos_rows, h, cap):
    b, n, d = h.shape
    return pl.pallas_call(
        functools.partial(_gather_kernel, cap=cap), grid=(b,),
        in_specs=[pl.BlockSpec((1, N_EXPERTS, n), lambda i: (i, 0, 0)),
                  pl.BlockSpec((1, n, d), lambda i: (i, 0, 0))],
        out_specs=pl.BlockSpec((N_EXPERTS, 1, cap, d), lambda i: (0, i, 0, 0)),
        out_shape=jax.ShapeDtypeStruct((N_EXPERTS, b, cap, d), BF16),
        compiler_params=_cparams(1), name="moe_gather",
    )(pos_rows, h)


def _ffn_kernel(xs_ref, wg_ref, wu_ref, wd_ref, o_ref):
    xs = xs_ref[0]
    hid = _silu(_dot(xs, wg_ref[0])) * _dot(xs, wu_ref[0])
    o_ref[0] = _dot(hid.astype(BF16), wd_ref[0]).astype(BF16)


def _ffn(xs, wg, wu, wd):
    e, m, d = xs.shape
    ff = wg.shape[2]
    tm = _row_tile(m)
    return pl.pallas_call(
        _ffn_kernel, grid=(e, m // tm),
        in_specs=[pl.BlockSpec((1, tm, d), lambda i, j: (i, j, 0)),
                  pl.BlockSpec((1, d, ff), lambda i, j: (i, 0, 0)),
                  pl.BlockSpec((1, d, ff), lambda i, j: (i, 0, 0)),
                  pl.BlockSpec((1, ff, d), lambda i, j: (i, 0, 0))],
        out_specs=pl.BlockSpec((1, tm, d), lambda i, j: (i, j, 0)),
        out_shape=jax.ShapeDtypeStruct((e, m, d), BF16),
        compiler_params=_cparams(2, 56), name="moe_ffn",
    )(xs, wg, wu, wd)


def _combine_kernel(out_ref, pos_ref, selw_ref, x_ref, mod_ref, *rest, cap, final):
    o_ref = rest[-1]
    x = x_ref[0]
    pos, selw = pos_ref[0], selw_ref[0]
    slot = lax.broadcasted_iota(jnp.int32, (x.shape[0], cap), 1).astype(F32)
    y = jnp.zeros(x.shape, F32)
    for e in range(N_EXPERTS):
        weighted = jnp.where(pos[:, e:e + 1] == slot, selw[:, e:e + 1], 0.0).astype(BF16)
        y = y + _dot(weighted, out_ref[e, 0])
    x = x + mod_ref[0, 5:6, :] * y
    if final:
        x = x * lax.rsqrt(jnp.mean(x * x, axis=-1, keepdims=True) + EPS) * rest[0][...]
    o_ref[0] = x


def _combine(out, pos, selw, x, mod, cap, norm_f):
    b, n, d = x.shape
    tm = _row_tile(n)
    final = norm_f is not None
    in_specs = [pl.BlockSpec((N_EXPERTS, 1, cap, d), lambda i, j: (0, i, 0, 0)),
                pl.BlockSpec((1, tm, LANE), lambda i, j: (i, j, 0)),
                pl.BlockSpec((1, tm, LANE), lambda i, j: (i, j, 0)),
                pl.BlockSpec((1, tm, d), lambda i, j: (i, j, 0)),
                pl.BlockSpec((1, 6, d), lambda i, j: (i, 0, 0))]
    args = [out, pos, selw, x, mod]
    if final:
        in_specs.append(pl.BlockSpec((1, d), lambda i, j: (0, 0)))
        args.append(norm_f.reshape(1, d))
    return pl.pallas_call(
        functools.partial(_combine_kernel, cap=cap, final=final), grid=(b, n // tm),
        in_specs=in_specs,
        out_specs=pl.BlockSpec((1, tm, d), lambda i, j: (i, j, 0)),
        out_shape=jax.ShapeDtypeStruct((b, n, d), F32),
        compiler_params=_cparams(2), name="moe_combine",
    )(*args)


def _moe(x, mod, g, wr_pad, wg, wu, wd, norm_f=None):
    b, n, d = x.shape
    cap = CAPACITY_FACTOR * n // N_EXPERTS
    idx = jnp.arange(n)
    tri = (idx[None, :] <= idx[:, None]).astype(BF16)
    h, aff = _moe_pre(x, mod, g, wr_pad)
    pos, selw = _route(aff, tri, cap)
    xs = _gather(pos[:, :, :N_EXPERTS].transpose(0, 2, 1), h, cap)
    out = _ffn(xs.reshape(N_EXPERTS, b * cap, d), wg, wu, wd).reshape(N_EXPERTS, b, cap, d)
    return _combine(out, pos, selw, x, mod, cap, norm_f)


def kernel(x, c, ctx, c_ctx, w_mod, b_mod, norm1, norm2, s5_a_re, s5_a_im, s5_log_dt, s5_b_re, s5_b_im, s5_c_re, s5_c_im, s5_d, s5_w_glu, na_w_qkv, na_w_o, na_rpb, da_w_qkv, da_w_o, da_lambda, da_subln, hg_w_in, hg_w_o, hg_gnorm, hg_lower_bounds, moe_router, moe_w_gate, moe_w_up, moe_w_down, norm_f):
    b, n, d = x.shape
    depth = w_mod.shape[0]
    assert depth == 4, "layer i uses mixer i; only the last layer drops the context stream"

    rows_pad = -(b + 1) % 8
    cc = jnp.concatenate([c, c_ctx[None, :], jnp.zeros((rows_pad, d), F32)], axis=0)
    mod_all = _modulation(cc, w_mod, b_mod)
    p_lb = jax.nn.softmax(hg_lower_bounds.astype(F32), axis=0)
    lbs = jnp.cumsum(p_lb, axis=0) - p_lb[0]
    h_ctx = ctx

    for i in range(depth):
        last = i == depth - 1
        mod_l = mod_all[i, :b].reshape(b, 6, d)
        mod_c = jnp.broadcast_to(mod_all[i, b].reshape(1, 6, d), (b, 6, d))
        g1 = norm1[i]
        if i == 0:
            tables = _s5_tables(s5_a_re[0], s5_a_im[0], s5_log_dt[0], s5_b_re[0], s5_b_im[0], s5_c_re[0], s5_c_im[0])
            h_ctx, x = _s5_layer(x, h_ctx, mod_l, mod_c, g1, tables, s5_d[0], s5_w_glu[0].astype(BF16))
        elif i == 1:
            w_qkv = na_w_qkv[0].astype(BF16)
            qkv_l = _proj_in(x, mod_l, g1, w_qkv, BF16)
            qkv_c = _proj_in(h_ctx, mod_c, g1, w_qkv, BF16)
            bias = _na_bias_table(na_rpb[0], min(WIN_R, n // GRID_W))
            w_o = na_w_o[0].astype(BF16)
            x = _proj_out(_na_attention(qkv_l, qkv_c, bias, d), x, mod_l, w_o)
            h_ctx = _proj_out(_attention(qkv_c, None, qkv_c, d, diff=False), h_ctx, mod_c, w_o)
        elif i == 2:
            w_qkv = da_w_qkv[0].astype(BF16)
            qkv_l = _proj_in(x, mod_l, g1, w_qkv, BF16, rope=_rope_tables(n), rope_cols=2 * d)
            qkv_c = _proj_in(h_ctx, mod_c, g1, w_qkv, BF16)
            lam = da_lambda[0].astype(F32)
            lam_init = 0.8 - 0.6 * math.exp(-0.3 * i)
            lam_full = jnp.exp(jnp.sum(lam[0] * lam[1])) - jnp.exp(jnp.sum(lam[2] * lam[3])) + lam_init
            w_o = da_w_o[0].astype(BF16)
            attend = functools.partial(_attention, d=d, diff=True, lam=lam_full, subln=da_subln[0],
                                       post_scale=1.0 - lam_init)
            x = _proj_out(attend(qkv_l, qkv_l, qkv_c), x, mod_l, w_o)
            h_ctx = _proj_out(attend(qkv_c, None, qkv_c), h_ctx, mod_c, w_o)
        else:
            w_in = hg_w_in[0].astype(BF16)
            z_l = _proj_in(x, mod_l, g1, w_in, F32)
            z_c = _proj_in(h_ctx, mod_c, g1, w_in, F32)
            x = _hg_out(_hg_scan(z_l, z_c, lbs[i], d), z_l, hg_gnorm[0], x, mod_l, hg_w_o[0].astype(BF16))

        wr_pad = jnp.pad(moe_router[i].astype(F32), ((0, 0), (0, LANE - N_EXPERTS)))
        wg, wu, wd = moe_w_gate[i].astype(BF16), moe_w_up[i].astype(BF16), moe_w_down[i].astype(BF16)
        x = _moe(x, mod_l, norm2[i], wr_pad, wg, wu, wd, norm_f if last else None)
        if not last:
            h_ctx = _moe(h_ctx, mod_c, norm2[i], wr_pad, wg, wu, wd)
    return x
```

```python
import functools
import math

import jax
import jax.numpy as jnp
import numpy as np
from jax import lax
from jax.experimental import pallas as pl
from jax.experimental.pallas import tpu as pltpu

F32 = jnp.float32
BF16 = jnp.bfloat16
HIGHEST = lax.Precision.HIGHEST
EPS = 1e-6
NEG = -1e30

LANE = 128
GRID_W = 64
WIN_R = 8
WIN_C = 16
NA_ROWS = 4
HEAD_DIM = 64
ROPE_BASE = 10000.0
S5_GROUP = 16
S5_CHUNK = 16
HG_CHUNK = 256
HG_HEADS_PER_STEP = 2
N_EXPERTS = 16
CAPACITY_FACTOR = 2
ROW_TILE = 512


def _cparams(n_axes, vmem_mb=48):
    return pltpu.CompilerParams(dimension_semantics=("arbitrary",) * n_axes,
                                vmem_limit_bytes=vmem_mb * 1024 * 1024)


def _dot(a, b):
    return jnp.dot(a, b, preferred_element_type=F32)


def _dot_nt(a, b):
    return lax.dot_general(a, b, (((1,), (1,)), ((), ())), preferred_element_type=F32)


def _normmod(x, g, shift, scale):
    y = x * lax.rsqrt(jnp.mean(x * x, axis=-1, keepdims=True) + EPS)
    return (y * g) * (1.0 + scale) + shift


def _silu(x):
    return x * jax.nn.sigmoid(x)


def _row_tile(n):
    return min(n, ROW_TILE)


def _mod_kernel(c_ref, w_ref, b_ref, o_ref):
    s = _silu(c_ref[...])
    o_ref[0] = jnp.dot(s, w_ref[0], precision=HIGHEST, preferred_element_type=F32) + b_ref[0]


def _modulation(cc, w_mod, b_mod):
    depth, d, d6 = w_mod.shape
    r = cc.shape[0]
    return pl.pallas_call(
        _mod_kernel, grid=(depth, d6 // d),
        in_specs=[pl.BlockSpec((r, d), lambda i, j: (0, 0)),
                  pl.BlockSpec((1, d, d), lambda i, j: (i, 0, j)),
                  pl.BlockSpec((1, 1, d), lambda i, j: (i, 0, j))],
        out_specs=pl.BlockSpec((1, r, d), lambda i, j: (i, 0, j)),
        out_shape=jax.ShapeDtypeStruct((depth, r, d6), F32),
        compiler_params=_cparams(2), name="modulation",
    )(cc, w_mod, b_mod.reshape(depth, 1, d6))


def _proj_in_kernel(x_ref, mod_ref, g_ref, w_ref, *rest, rope_cols):
    o_ref = rest[-1]
    h = _normmod(x_ref[0], g_ref[...], mod_ref[0, 0:1, :], mod_ref[0, 1:2, :])
    y = _dot(h.astype(BF16), w_ref[...])
    if rope_cols:
        cos, s_up, s_dn = rest[0][...], rest[1][...], rest[2][...]
        for j in range(rope_cols // LANE):
            t = y[:, j * LANE:(j + 1) * LANE]
            t = t * cos + pltpu.roll(t, LANE - 16, 1) * s_up + pltpu.roll(t, 16, 1) * s_dn
            o_ref[0, :, j * LANE:(j + 1) * LANE] = t.astype(o_ref.dtype)
        o_ref[0, :, rope_cols:] = y[:, rope_cols:].astype(o_ref.dtype)
    else:
        o_ref[0] = y.astype(o_ref.dtype)


def _proj_in(x, mod, g, w, out_dtype, rope=None, rope_cols=0):
    b, n, d = x.shape
    nout = w.shape[1]
    tm = _row_tile(n)
    in_specs = [pl.BlockSpec((1, tm, d), lambda i, j: (i, j, 0)),
                pl.BlockSpec((1, 6, d), lambda i, j: (i, 0, 0)),
                pl.BlockSpec((1, d), lambda i, j: (0, 0)),
                pl.BlockSpec((d, nout), lambda i, j: (0, 0))]
    args = [x, mod, g.reshape(1, d), w]
    if rope_cols:
        in_specs += [pl.BlockSpec((tm, LANE), lambda i, j: (j, 0))] * 3
        args += list(rope)
    return pl.pallas_call(
        functools.partial(_proj_in_kernel, rope_cols=rope_cols), grid=(b, n // tm),
        in_specs=in_specs,
        out_specs=pl.BlockSpec((1, tm, nout), lambda i, j: (i, j, 0)),
        out_shape=jax.ShapeDtypeStruct((b, n, nout), out_dtype),
        compiler_params=_cparams(2, 56), name="proj_in",
    )(*args)


def _rope_tables(n):
    t = jnp.arange(n)
    row = (t // GRID_W).astype(F32)
    col = (t % GRID_W).astype(F32)
    quarter = HEAD_DIM // 4
    inv = ROPE_BASE ** (-jnp.arange(quarter, dtype=F32) / quarter)
    lane = jnp.arange(LANE)
    pos = jnp.where(((lane % HEAD_DIM) // (HEAD_DIM // 2))[None, :] == 0, row[:, None], col[:, None])
    ang = pos * inv[lane % quarter][None, :]
    first = ((lane % (HEAD_DIM // 2)) < quarter)[None, :]
    cos, sin = jnp.cos(ang), jnp.sin(ang)
    return cos, jnp.where(first, -sin, 0.0), jnp.where(first, 0.0, sin)


def _proj_out_kernel(o_ref, x_ref, mod_ref, w_ref, out_ref):
    out_ref[0] = x_ref[0] + mod_ref[0, 2:3, :] * _dot(o_ref[0], w_ref[...])


def _proj_out(o, x, mod, w):
    b, n, d = x.shape
    tm = _row_tile(n)
    return pl.pallas_call(
        _proj_out_kernel, grid=(b, n // tm),
        in_specs=[pl.BlockSpec((1, tm, d), lambda i, j: (i, j, 0)),
                  pl.BlockSpec((1, tm, d), lambda i, j: (i, j, 0)),
                  pl.BlockSpec((1, 6, d), lambda i, j: (i, 0, 0)),
                  pl.BlockSpec((d, d), lambda i, j: (0, 0))],
        out_specs=pl.BlockSpec((1, tm, d), lambda i, j: (i, j, 0)),
        out_shape=jax.ShapeDtypeStruct((b, n, d), F32),
        compiler_params=_cparams(2), name="proj_out",
    )(o, x, mod, w)


def _softmax_parts(scores):
    m = scores[0].max(axis=-1, keepdims=True)
    for s in scores[1:]:
        m = jnp.maximum(m, s.max(axis=-1, keepdims=True))
    es = [jnp.exp(s - m) for s in scores]
    z = es[0].sum(axis=-1, keepdims=True)
    for e in es[1:]:
        z = z + e.sum(axis=-1, keepdims=True)
    return es, z


def _attn_kernel(*refs, diff, has_lat, scale, post_scale):
    refs = list(refs)
    q_ref = refs.pop(0)
    kv = [(refs.pop(0), refs.pop(0))] if has_lat else []
    kv.append((refs.pop(0), refs.pop(0)))
    if diff:
        lam_ref, sub_ref = refs.pop(0), refs.pop(0)
    o_ref = refs.pop(0)
    tq = q_ref.shape[1]
    lo = lax.broadcasted_iota(jnp.int32, (1, LANE), 1) < HEAD_DIM
    zero = jnp.zeros((), BF16)
    for t in range(q_ref.shape[2] // LANE):
        sl = slice(t * LANE, (t + 1) * LANE)
        qt = q_ref[0, :, sl] * jnp.asarray(scale, BF16)
        q0, q1 = jnp.where(lo, qt, zero), jnp.where(lo, zero, qt)
        ks = [k_ref[0, :, sl] for k_ref, _ in kv]
        vs = [v_ref[0, :, sl] for _, v_ref in kv]
        e0, z0 = _softmax_parts([_dot_nt(q0, k) for k in ks])
        e1, z1 = _softmax_parts([_dot_nt(q1, k) for k in ks])
        r0 = 1.0 / z0
        if diff:
            r1 = lam_ref[0, 0] / z1
            o = jnp.zeros((tq, LANE), F32)
            for a0, a1, v in zip(e0, e1, vs):
                o = o + _dot((a0 * r0 - a1 * r1).astype(BF16), v)
            o = o * lax.rsqrt(jnp.mean(o * o, axis=-1, keepdims=True) + EPS) * sub_ref[...] * post_scale
        else:
            r1 = 1.0 / z1
            o = jnp.zeros((tq, LANE), F32)
            for a0, a1, v in zip(e0, e1, vs):
                o = o + _dot((a0 * r0).astype(BF16), jnp.where(lo, v, zero))
                o = o + _dot((a1 * r1).astype(BF16), jnp.where(lo, zero, v))
        o_ref[0, :, sl] = o.astype(o_ref.dtype)


def _attention(qkv_q, qkv_lat, qkv_ctx, d, diff, lam=None, subln=None, post_scale=1.0):
    b, nq, _ = qkv_q.shape
    tq = min(nq, 256)
    has_lat = qkv_lat is not None
    in_specs = [pl.BlockSpec((1, tq, d), lambda i, j: (i, j, 0))]
    args = [qkv_q]
    for src in ([qkv_lat] if has_lat else []) + [qkv_ctx]:
        nk = src.shape[1]
        in_specs += [pl.BlockSpec((1, nk, d), lambda i, j: (i, 0, 1)),
                     pl.BlockSpec((1, nk, d), lambda i, j: (i, 0, 2))]
        args += [src, src]
    if diff:
        in_specs += [pl.BlockSpec(memory_space=pltpu.SMEM), pl.BlockSpec((1, LANE), lambda i, j: (0, 0))]
        args += [lam.reshape(1, 1), subln.reshape(1, LANE)]
    return pl.pallas_call(
        functools.partial(_attn_kernel, diff=diff, has_lat=has_lat, scale=HEAD_DIM ** -0.5,
                          post_scale=post_scale),
        grid=(b, nq // tq), in_specs=in_specs,
        out_specs=pl.BlockSpec((1, tq, d), lambda i, j: (i, j, 0)),
        out_shape=jax.ShapeDtypeStruct((b, nq, d), BF16),
        compiler_params=_cparams(2, 56), name="attention",
    )(*args)


def _na_kernel(cls_ref, q_ref, k_ref, v_ref, kc_ref, vc_ref, bias_ref, o_ref, *, rows, wr, kr, scale):
    del cls_ref
    kstart = jnp.clip(pl.program_id(1) * NA_ROWS - wr // 2, 0, rows - kr)
    start = pl.multiple_of(kstart * GRID_W, GRID_W)
    nk = kr * GRID_W
    lo = lax.broadcasted_iota(jnp.int32, (1, LANE), 1) < HEAD_DIM
    zero = jnp.zeros((), BF16)
    for t in range(q_ref.shape[2] // LANE):
        sl = slice(t * LANE, (t + 1) * LANE)
        qt = q_ref[0, :, sl] * jnp.asarray(scale, BF16)
        kw, vw = k_ref[0, pl.ds(start, nk), sl], v_ref[0, pl.ds(start, nk), sl]
        kc, vc = kc_ref[0, :, sl], vc_ref[0, :, sl]
        o = jnp.zeros((q_ref.shape[1], LANE), F32)
        for half in range(2):
            keep = lo if half == 0 else jnp.logical_not(lo)
            qh = jnp.where(keep, qt, zero)
            (el, ec), z = _softmax_parts([_dot_nt(qh, kw) + bias_ref[0, 2 * t + half], _dot_nt(qh, kc)])
            rz = 1.0 / z
            o = o + _dot((el * rz).astype(BF16), jnp.where(keep, vw, zero))
            o = o + _dot((ec * rz).astype(BF16), jnp.where(keep, vc, zero))
        o_ref[0, :, sl] = o.astype(o_ref.dtype)


def _na_layout(rows):
    wr = min(WIN_R, rows)
    kr = min(rows, wr + NA_ROWS - 1)
    classes, cls_of_block = [], []
    for blk in range(rows // NA_ROWS):
        kstart = min(max(blk * NA_ROWS - wr // 2, 0), rows - kr)
        key = []
        for qr in range(NA_ROWS):
            r = blk * NA_ROWS + qr
            r0 = min(max(r - wr // 2, 0), rows - wr)
            key.append(tuple((kstart + j - r + WIN_R - 1) if r0 <= kstart + j < r0 + wr else None for j in range(kr)))
        key = tuple(key)
        if key not in classes:
            classes.append(key)
        cls_of_block.append(classes.index(key))
    return wr, kr, classes, cls_of_block


def _na_bias_table(rpb, rows):
    heads = rpb.shape[0]
    _, _, classes, _ = _na_layout(rows)
    cq = jnp.arange(GRID_W)
    c0 = jnp.clip(cq - WIN_C // 2, 0, GRID_W - WIN_C)
    col_in = (cq[None, :] >= c0[:, None]) & (cq[None, :] < c0[:, None] + WIN_C)
    dc = jnp.clip(cq[None, :] - cq[:, None] + WIN_C - 1, 0, 2 * WIN_C - 2)
    dc_onehot = (dc[:, :, None] == jnp.arange(2 * WIN_C - 1)[None, None, :]).astype(F32)
    per_dr = jnp.einsum('hrd,qkd->hrqk', rpb.astype(F32), dc_onehot, precision=HIGHEST)
    per_dr = jnp.where(col_in[None, None], per_dr, NEG)
    masked = jnp.full((heads, GRID_W, GRID_W), NEG, F32)
    tabs = []
    for key in classes:
        slabs = [jnp.concatenate([masked if dr is None else per_dr[:, dr] for dr in row], axis=-1) for row in key]
        tabs.append(jnp.concatenate(slabs, axis=1))
    return jnp.stack(tabs, axis=0)


def _na_attention(qkv_l, qkv_c, bias, d):
    b, n, _ = qkv_l.shape
    nc = qkv_c.shape[1]
    rows = n // GRID_W
    wr, kr, _, cls_of_block = _na_layout(rows)
    heads = d // HEAD_DIM
    tq = NA_ROWS * GRID_W
    grid_spec = pltpu.PrefetchScalarGridSpec(
        num_scalar_prefetch=1, grid=(b, rows // NA_ROWS),
        in_specs=[pl.BlockSpec((1, tq, d), lambda i, r, cls: (i, r, 0)),
                  pl.BlockSpec((1, n, d), lambda i, r, cls: (i, 0, 1)),
                  pl.BlockSpec((1, n, d), lambda i, r, cls: (i, 0, 2)),
                  pl.BlockSpec((1, nc, d), lambda i, r, cls: (i, 0, 1)),
                  pl.BlockSpec((1, nc, d), lambda i, r, cls: (i, 0, 2)),
                  pl.BlockSpec((1, heads, tq, kr * GRID_W), lambda i, r, cls: (cls[r], 0, 0, 0),
                               pipeline_mode=pl.Buffered(1))],
        out_specs=pl.BlockSpec((1, tq, d), lambda i, r, cls: (i, r, 0)))
    return pl.pallas_call(
        functools.partial(_na_kernel, rows=rows, wr=wr, kr=kr, scale=HEAD_DIM ** -0.5),
        grid_spec=grid_spec,
        out_shape=jax.ShapeDtypeStruct((b, n, d), BF16),
        compiler_params=_cparams(2, 56), name="na_attention",
    )(jnp.asarray(cls_of_block, jnp.int32), qkv_l, qkv_l, qkv_l, qkv_c, qkv_c, bias)


def _normmod_kernel(x_ref, mod_ref, g_ref, o_ref):
    o_ref[0] = _normmod(x_ref[0], g_ref[...], mod_ref[0, 0:1, :], mod_ref[0, 1:2, :]).astype(o_ref.dtype)


def _normmod_call(x, mod, g):
    b, n, d = x.shape
    tm = _row_tile(n)
    return pl.pallas_call(
        _normmod_kernel, grid=(b, n // tm),
        in_specs=[pl.BlockSpec((1, tm, d), lambda i, j: (i, j, 0)),
                  pl.BlockSpec((1, 6, d), lambda i, j: (i, 0, 0)),
                  pl.BlockSpec((1, d), lambda i, j: (0, 0))],
        out_specs=pl.BlockSpec((1, tm, d), lambda i, j: (i, j, 0)),
        out_shape=jax.ShapeDtypeStruct((b, n, d), BF16),
        compiler_params=_cparams(2), name="normmod",
    )(x, mod, g.reshape(1, d))


def _s5_tables(a_re, a_im, log_dt, b_re, b_im, c_re, c_im):
    t_len = S5_CHUNK
    a_re, a_im = a_re.astype(F32), a_im.astype(F32)
    dt = jnp.exp(log_dt.astype(F32))[..., None]
    lr, li = a_re * dt, a_im * dt
    cos_li, sin_li = jnp.cos(li), jnp.sin(li)
    ab_im = jnp.exp(lr) * sin_li
    nr = jnp.expm1(lr) * cos_li - 2.0 * jnp.sin(0.5 * li) ** 2
    den = a_re * a_re + a_im * a_im
    fr = (nr * a_re + ab_im * a_im) / den
    fi = (ab_im * a_re - nr * a_im) / den
    b_re, b_im = b_re.astype(F32), b_im.astype(F32)
    bb_re = fr[..., None] * b_re - fi[..., None] * b_im
    bb_im = fr[..., None] * b_im + fi[..., None] * b_re
    tau = jnp.arange(t_len + 1, dtype=F32)[:, None, None, None]
    mag = jnp.exp(tau * lr)
    pr, pi = mag * jnp.cos(tau * li), mag * jnp.sin(tau * li)
    c_re, c_im = c_re.astype(F32), c_im.astype(F32)
    cw_re = c_re[None] * pr[:, :, :, None, :] - c_im[None] * pi[:, :, :, None, :]
    cw_im = c_re[None] * pi[:, :, :, None, :] + c_im[None] * pr[:, :, :, None, :]
    kern = (jnp.einsum('tdgkp,dgph->tdgkh', cw_re, bb_re, precision=HIGHEST)
            - jnp.einsum('tdgkp,dgph->tdgkh', cw_im, bb_im, precision=HIGHEST))
    g = a_re.shape[1]
    h = S5_GROUP
    diff = jnp.arange(t_len)[None, :] - jnp.arange(t_len)[:, None]
    lags = jnp.arange(t_len)[None, None, :]
    fwd_lag = (diff[:, :, None] == lags).astype(F32)
    bwd_lag = (-diff[:, :, None] == lags).astype(F32)
    k_f = jnp.einsum('sta,agkh->stgkh', fwd_lag, kern[:t_len, 0], precision=HIGHEST)
    k_b = jnp.einsum('sta,agkh->stgkh', bwd_lag, kern[:t_len, 1], precision=HIGHEST)
    m_both = (k_f + k_b).transpose(2, 0, 4, 1, 3).reshape(g, t_len * h, t_len * h)

    def in_map(d, p_re, p_im):
        wr_ = p_re[..., None] * bb_re[d][None] - p_im[..., None] * bb_im[d][None]
        wi_ = p_re[..., None] * bb_im[d][None] + p_im[..., None] * bb_re[d][None]
        to_rows = lambda w: w.transpose(1, 0, 3, 2).reshape(g, t_len * h, -1)
        return to_rows(wr_), to_rows(wi_)

    f_re, f_im = in_map(0, pr[:t_len, 0][::-1], pi[:t_len, 0][::-1])
    r_re, r_im = in_map(1, pr[:t_len, 1], pi[:t_len, 1])
    w_in = jnp.concatenate([f_re, f_im, f_im, f_re, r_re, r_im, r_im, r_re], axis=-1)

    def out_map(w_re, w_im):
        to_cols = lambda w: w.transpose(1, 3, 0, 2).reshape(g, -1, t_len * h)
        return jnp.concatenate([to_cols(w_re), -to_cols(w_im)], axis=1)

    w_out = jnp.concatenate([out_map(cw_re[1:, 0], cw_im[1:, 0]),
                             out_map(cw_re[1:, 1][::-1], cw_im[1:, 1][::-1])], axis=1)
    a_pow = jnp.stack([jnp.concatenate([pr[t_len, 0], pr[t_len, 0]], -1),
                       jnp.concatenate([-pi[t_len, 0], pi[t_len, 0]], -1),
                       jnp.concatenate([pr[t_len, 1], pr[t_len, 1]], -1),
                       jnp.concatenate([-pi[t_len, 1], pi[t_len, 1]], -1)], axis=1)
    return m_both.astype(BF16), w_in.astype(BF16), w_out.astype(BF16), a_pow


def _s5_core_kernel(u_ref, m_ref, win_ref, wout_ref, a_ref, y_ref, sall_ref, sin_ref, *, bn, nc_c, nc_l):
    u = u_ref[0]
    sin_ref[...] = _dot(u, win_ref[0])
    a = a_ref[0]
    a1f, a2f, a1b, a2b = a[0:1], a[1:2], a[2:3], a[3:4]
    w = LANE
    zero = jnp.zeros((bn, w), F32)

    def rows_of(c):
        return pl.ds(pl.multiple_of(c * bn, bn), bn)

    def fwd(c, carry):
        s, sw = carry
        r = rows_of(c)
        sall_ref[r, 0:w] = s
        return (a1f * s + a2f * sw + sin_ref[r, 0:w], a1f * sw - a2f * s + sin_ref[r, w:2 * w])

    def bwd(c, carry):
        s, sw = carry
        r = rows_of(c)
        sall_ref[r, w:2 * w] = s
        return (a1b * s + a2b * sw + sin_ref[r, 2 * w:3 * w], a1b * sw - a2b * s + sin_ref[r, 3 * w:4 * w])

    lax.fori_loop(0, nc_c + nc_l, fwd, (zero, zero))
    carry = lax.fori_loop(0, nc_c, lambda i, cr: bwd(nc_c - 1 - i, cr), (zero, zero))
    lax.fori_loop(0, nc_l, lambda i, cr: bwd(nc_c + nc_l - 1 - i, cr), carry)
    y_ref[0] = _dot(u, m_ref[0]) + _dot(sall_ref[...].astype(BF16), wout_ref[0])


def _s5_core(u_rows, tables, bn, nc_c, nc_l):
    g, rows, width = u_rows.shape
    m_both, w_in, w_out, a_pow = tables
    return pl.pallas_call(
        functools.partial(_s5_core_kernel, bn=bn, nc_c=nc_c, nc_l=nc_l), grid=(g,),
        in_specs=[pl.BlockSpec((1, rows, width), lambda i: (i, 0, 0)),
                  pl.BlockSpec((1, width, width), lambda i: (i, 0, 0)),
                  pl.BlockSpec((1, width, 4 * LANE), lambda i: (i, 0, 0)),
                  pl.BlockSpec((1, 2 * LANE, width), lambda i: (i, 0, 0)),
                  pl.BlockSpec((1, 4, LANE), lambda i: (i, 0, 0))],
        out_specs=pl.BlockSpec((1, rows, width), lambda i: (i, 0, 0)),
        out_shape=jax.ShapeDtypeStruct((g, rows, width), F32),
        scratch_shapes=[pltpu.VMEM((rows, 2 * LANE), F32), pltpu.VMEM((rows, 4 * LANE), F32)],
        compiler_params=_cparams(1, 56), name="s5_core",
    )(u_rows, m_both, w_in, w_out, a_pow)


def _s5_glu_kernel(x_ref, y_ref, mod_ref, g_ref, d_ref, w_ref, out_ref):
    x = x_ref[0]
    d = x.shape[1]
    u = _normmod(x, g_ref[...], mod_ref[0, 0:1, :], mod_ref[0, 1:2, :])
    z = jax.nn.gelu(y_ref[0] + d_ref[...] * u)
    zz = _dot(z.astype(BF16), w_ref[...])
    out_ref[0] = x + mod_ref[0, 2:3, :] * (zz[:, :d] * jax.nn.sigmoid(zz[:, d:]))


def _s5_glu(x, y, mod, g, dskip, w_glu):
    b, n, d = x.shape
    tm = _row_tile(n)
    return pl.pallas_call(
        _s5_glu_kernel, grid=(b, n // tm),
        in_specs=[pl.BlockSpec((1, tm, d), lambda i, j: (i, j, 0)),
                  pl.BlockSpec((1, tm, d), lambda i, j: (i, j, 0)),
                  pl.BlockSpec((1, 6, d), lambda i, j: (i, 0, 0)),
                  pl.BlockSpec((1, d), lambda i, j: (0, 0)),
                  pl.BlockSpec((1, d), lambda i, j: (0, 0)),
                  pl.BlockSpec((d, 2 * d), lambda i, j: (0, 0))],
        out_specs=pl.BlockSpec((1, tm, d), lambda i, j: (i, j, 0)),
        out_shape=jax.ShapeDtypeStruct((b, n, d), F32),
        compiler_params=_cparams(2), name="s5_glu",
    )(x, y, mod, g.reshape(1, d), dskip.reshape(1, d), w_glu)


def _s5_layer(x, h_ctx, mod_l, mod_c, g, tables, dskip, w_glu):
    b, n, d = x.shape
    n_c = h_ctx.shape[1]
    t_len, h = S5_CHUNK, S5_GROUP
    groups = d // h
    nc_l, nc_c = n // t_len, n_c // t_len

    def to_rows(u, nc):
        return u.reshape(b, nc, t_len, groups, h).transpose(3, 1, 0, 2, 4).reshape(groups, nc * b, t_len * h)

    def from_rows(y, nc):
        return y.reshape(groups, nc, b, t_len, h).transpose(2, 1, 3, 0, 4).reshape(b, nc * t_len, d)

    u_rows = jnp.concatenate([to_rows(_normmod_call(h_ctx, mod_c, g), nc_c),
                              to_rows(_normmod_call(x, mod_l, g), nc_l)], axis=1)
    y_rows = _s5_core(u_rows, tables, b, nc_c, nc_l)
    y_c = from_rows(y_rows[:, :nc_c * b], nc_c)
    y_l = from_rows(y_rows[:, nc_c * b:], nc_l)
    return (_s5_glu(h_ctx, y_c, mod_c, g, dskip, w_glu), _s5_glu(x, y_l, mod_l, g, dskip, w_glu))


def _hg_kernel(ql_ref, ffl_ref, fbl_ref, il_ref, ffc_ref, fbc_ref, ic_ref, lb_ref, wf_ref, wb_ref, mask_ref,
               o_ref, ob_ref, sf_ref, sb_ref, *, nc_c, nc_l):
    cn = HG_CHUNK
    levels = cn.bit_length() - 1
    nh = HG_HEADS_PER_STEP
    lb = lb_ref[...]
    row = lax.broadcasted_iota(jnp.int32, (cn, 1), 0)

    def gates(ff):
        f = lb + (1.0 - lb) * jax.nn.sigmoid(ff)
        return jnp.log(f), 1.0 - f

    def dot3(w01, x):
        hi = x.astype(BF16)
        rest = x - hi.astype(F32)
        mid = rest.astype(BF16)
        low = (rest - mid.astype(F32)).astype(BF16)
        return _dot(w01, hi) + _dot(w01, mid) + _dot(w01, low)

    def visit(rows, q_ref, ff_ref, i_ref, w_ref, s_ref, out_ref, d_idx):
        fwd = d_idx == 0
        lf, k = gates(ff_ref[0, rows, :])
        v = _silu(i_ref[0, rows, :])
        n_blocks = 2 if out_ref is None else 2 + levels
        e = jnp.exp(dot3(w_ref[0:n_blocks * cn, :], lf))
        blk = lambda i: e[i * cn:(i + 1) * cn]
        kd = (k * blk(1)).astype(BF16)
        decay = blk(0)[cn - 1:cn] if fwd else blk(0)[0:1]
        vb = v.astype(BF16)
        if out_ref is not None:
            q = q_ref[0, rows, :]
            qd = (q * blk(0)).astype(BF16)
            qk = q * k
            outs = []
            for hh in range(nh):
                sl = slice(hh * LANE, (hh + 1) * LANE)
                att = jnp.zeros((cn, cn), F32)
                for lv in range(levels):
                    is_query = ((row // (cn >> (lv + 1))) % 2 == 1) if fwd else ((row // (cn >> (lv + 1))) % 2 == 0)
                    x = (jnp.where(is_query, q[:, sl], k[:, sl]) * blk(2 + lv)[:, sl]).astype(BF16)
                    att = att + jnp.where(mask_ref[d_idx * levels + lv] > 0.0, _dot_nt(x, x), 0.0)
                same_token = jnp.sum(qk[:, sl], axis=-1, keepdims=True) * v[:, sl]
                outs.append(_dot_nt(qd[:, sl], s_ref[hh].astype(BF16)) + _dot(att.astype(BF16), vb[:, sl]) + same_token)
            dst = out_ref.at[0] if len(out_ref.shape) == 3 else out_ref
            dst[rows, :] = jnp.concatenate(outs, axis=-1)
        for hh in range(nh):
            sl = slice(hh * LANE, (hh + 1) * LANE)
            s_ref[hh] = s_ref[hh] * decay[:, sl] + _dot(v[:, sl].T.astype(BF16), kd[:, sl])

    def rows_of(c):
        return pl.ds(pl.multiple_of(c * cn, cn), cn)

    sf_ref[...] = jnp.zeros(sf_ref.shape, F32)
    sb_ref[...] = jnp.zeros(sb_ref.shape, F32)

    def ctx_body(j, carry):
        visit(rows_of(j), None, ffc_ref, ic_ref, wf_ref, sf_ref, None, 0)
        visit(rows_of(nc_c - 1 - j), None, fbc_ref, ic_ref, wb_ref, sb_ref, None, 1)
        return carry

    def lat_body(j, carry):
        visit(rows_of(j), ql_ref, ffl_ref, il_ref, wf_ref, sf_ref, o_ref, 0)
        visit(rows_of(nc_l - 1 - j), ql_ref, fbl_ref, il_ref, wb_ref, sb_ref, ob_ref, 1)
        return carry

    lax.fori_loop(0, nc_c, ctx_body, 0)
    lax.fori_loop(0, nc_l, lat_body, 0)
    o_ref[0] = o_ref[0] + ob_ref[...]


def _hg_operators():
    cn = HG_CHUNK
    levels = cn.bit_length() - 1
    t = np.arange(cn)[:, None]
    r = np.arange(cn)[None, :]
    w = np.zeros((2, 2 + levels, cn, cn), np.float32)
    mask = np.zeros((2, levels, cn, cn), np.float32)
    w[0, 0], w[0, 1] = r <= t, r > t
    w[1, 0], w[1, 1] = r >= t, r < t
    for lv in range(levels):
        bs = cn >> (lv + 1)
        parent = t // (2 * bs) * (2 * bs)
        second = (t // bs) % 2 == 1
        anchor_f = parent + bs - 1
        w[0, 2 + lv] = np.where(second, (r > anchor_f) & (r <= t), (r > t) & (r <= anchor_f))
        anchor_b = parent + bs
        w[1, 2 + lv] = np.where(second, (r >= anchor_b) & (r < t), (r >= t) & (r < anchor_b))
        same_parent = (t // (2 * bs)) == (r // (2 * bs))
        key_second = (r // bs) % 2 == 1
        mask[0, lv] = same_parent & second & ~key_second
        mask[1, lv] = same_parent & ~second & key_second
    return (jnp.asarray(w.reshape(2, (2 + levels) * cn, cn), BF16),
            jnp.asarray(mask.reshape(2 * levels, cn, cn), F32))


def _hg_scan(z_l, z_c, lb, d):
    b, n, _ = z_l.shape
    n_c = z_c.shape[1]
    width = HG_HEADS_PER_STEP * LANE
    steps = d // width
    cn = HG_CHUNK
    w, mask = _hg_operators()
    col = lambda k: (lambda i, h: (i, 0, k * steps + h))
    lat = lambda k: pl.BlockSpec((1, n, width), col(k))
    ctx = lambda k: pl.BlockSpec((1, n_c, width), col(k))
    const = lambda a: pl.BlockSpec(a.shape, lambda i, h: (0,) * a.ndim)
    return pl.pallas_call(
        functools.partial(_hg_kernel, nc_c=n_c // cn, nc_l=n // cn), grid=(b, steps),
        in_specs=[lat(0), lat(1), lat(2), lat(3), ctx(1), ctx(2), ctx(3),
                  pl.BlockSpec((1, width), lambda i, h: (0, h)),
                  const(w[0]), const(w[1]), const(mask)],
        out_specs=pl.BlockSpec((1, n, width), lambda i, h: (i, 0, h)),
        out_shape=jax.ShapeDtypeStruct((b, n, d), F32),
        scratch_shapes=[pltpu.VMEM((n, width), F32),
                        pltpu.VMEM((HG_HEADS_PER_STEP, LANE, LANE), F32),
                        pltpu.VMEM((HG_HEADS_PER_STEP, LANE, LANE), F32)],
        compiler_params=_cparams(2), name="hgrn2_scan",
    )(z_l, z_l, z_l, z_l, z_c, z_c, z_c, lb.reshape(1, d), w[0], w[1], mask)


def _hg_out_kernel(o_ref, z_ref, gn_ref, x_ref, mod_ref, w_ref, out_ref):
    o, gate = o_ref[0], z_ref[0]
    parts = []
    for h in range(o.shape[1] // LANE):
        sl = slice(h * LANE, (h + 1) * LANE)
        t = o[:, sl]
        t = t * lax.rsqrt(jnp.mean(t * t, axis=-1, keepdims=True) + EPS) * gn_ref[...]
        parts.append((t * _silu(gate[:, sl])).astype(BF16))
    y = _dot(jnp.concatenate(parts, axis=-1), w_ref[...])
    out_ref[0] = x_ref[0] + mod_ref[0, 2:3, :] * y


def _hg_out(o, z, gnorm, x, mod, w):
    b, n, d = x.shape
    tm = _row_tile(n)
    return pl.pallas_call(
        _hg_out_kernel, grid=(b, n // tm),
        in_specs=[pl.BlockSpec((1, tm, d), lambda i, j: (i, j, 0)),
                  pl.BlockSpec((1, tm, d), lambda i, j: (i, j, 4)),
                  pl.BlockSpec((1, LANE), lambda i, j: (0, 0)),
                  pl.BlockSpec((1, tm, d), lambda i, j: (i, j, 0)),
                  pl.BlockSpec((1, 6, d), lambda i, j: (i, 0, 0)),
                  pl.BlockSpec((d, d), lambda i, j: (0, 0))],
        out_specs=pl.BlockSpec((1, tm, d), lambda i, j: (i, j, 0)),
        out_shape=jax.ShapeDtypeStruct((b, n, d), F32),
        compiler_params=_cparams(2), name="hgrn2_out",
    )(o, z, gnorm.reshape(1, LANE), x, mod, w)


def _moe_pre_kernel(x_ref, mod_ref, g_ref, wr_ref, h_ref, aff_ref):
    h = _normmod(x_ref[0], g_ref[...], mod_ref[0, 3:4, :], mod_ref[0, 4:5, :])
    h_ref[0] = h.astype(BF16)
    logits = jnp.dot(h, wr_ref[...], precision=HIGHEST, preferred_element_type=F32)
    lane = lax.broadcasted_iota(jnp.int32, logits.shape, 1)
    logits = jnp.where(lane < N_EXPERTS, logits, NEG)
    e = jnp.exp(logits - logits.max(axis=-1, keepdims=True))
    aff_ref[0] = e / e.sum(axis=-1, keepdims=True)


def _moe_pre(x, mod, g, wr_pad):
    b, n, d = x.shape
    tm = _row_tile(n)
    return pl.pallas_call(
        _moe_pre_kernel, grid=(b, n // tm),
        in_specs=[pl.BlockSpec((1, tm, d), lambda i, j: (i, j, 0)),
                  pl.BlockSpec((1, 6, d), lambda i, j: (i, 0, 0)),
                  pl.BlockSpec((1, d), lambda i, j: (0, 0)),
                  pl.BlockSpec((d, LANE), lambda i, j: (0, 0))],
        out_specs=[pl.BlockSpec((1, tm, d), lambda i, j: (i, j, 0)),
                   pl.BlockSpec((1, tm, LANE), lambda i, j: (i, j, 0))],
        out_shape=[jax.ShapeDtypeStruct((b, n, d), BF16), jax.ShapeDtypeStruct((b, n, LANE), F32)],
        compiler_params=_cparams(2), name="moe_pre",
    )(x, mod, g.reshape(1, d), wr_pad)


def _route_kernel(aff_ref, tri_ref, pos_ref, selw_ref, *, cap):
    groups = LANE // N_EXPERTS
    rg = aff_ref.shape[1] // groups
    dense = aff_ref[0, 0:rg, :]
    for g in range(1, groups):
        dense = dense + pltpu.roll(aff_ref[0, g * rg:(g + 1) * rg, :], g * N_EXPERTS, 1)
    bits = lax.bitcast_convert_type(dense, jnp.int32)
    lane = lax.broadcasted_iota(jnp.int32, (1, LANE), 1)
    tri = tri_ref[...]

    def indicator(mask):
        return jnp.where(mask, jnp.ones((), F32), jnp.zeros((), F32))

    def over_groups(row):
        for shift in (LANE // 2, LANE // 4, LANE // 8):
            row = row + pltpu.roll(row, shift, 1)
        return row

    def before_groups(row):
        out = jnp.zeros_like(row)
        for j in range(1, groups):
            out = out + jnp.where(lane >= j * N_EXPERTS, pltpu.roll(row, j * N_EXPERTS, 1), 0.0)
        return out

    def count(mask):
        return over_groups(jnp.sum(indicator(mask), axis=0, keepdims=True))

    def prefix(mask):
        x = indicator(mask)
        return _dot(tri, x.astype(BF16)) + before_groups(jnp.sum(x, axis=0, keepdims=True))

    def search(i, thr):
        cand = thr | jnp.left_shift(jnp.int32(1), 30 - i)
        return jnp.where(count(bits >= cand) >= cap, cand, thr)

    thr = lax.fori_loop(0, 31, search, jnp.zeros((1, LANE), jnp.int32))
    above, tie = bits > thr, bits == thr
    need = cap - count(above)
    tie_rank = prefix(tie)
    sel = above | (tie & (tie_rank <= need))
    pos_ref[0] = jnp.where(sel, prefix(above) + jnp.minimum(tie_rank, need) - 1.0, -1.0)
    selw_ref[0] = jnp.where(sel, dense, 0.0)


def _route(aff, cap):
    b, n, _ = aff.shape
    rg = n // (LANE // N_EXPERTS)
    idx = jnp.arange(rg)
    tri = (idx[None, :] <= idx[:, None]).astype(BF16)
    out_spec = pl.BlockSpec((1, rg, LANE), lambda i: (i, 0, 0))
    return pl.pallas_call(
        functools.partial(_route_kernel, cap=cap), grid=(b,),
        in_specs=[pl.BlockSpec((1, n, LANE), lambda i: (i, 0, 0)), pl.BlockSpec((rg, rg), lambda i: (0, 0))],
        out_specs=[out_spec, out_spec],
        out_shape=[jax.ShapeDtypeStruct((b, rg, LANE), F32)] * 2,
        compiler_params=_cparams(1), name="moe_route",
    )(aff, tri)


def _gather_kernel(pos_ref, h_ref, xs_ref, *, cap):
    h = h_ref[0]
    slot = lax.broadcasted_iota(jnp.int32, (cap, h.shape[0]), 0).astype(F32)
    for e in range(N_EXPERTS):
        onehot = jnp.where(pos_ref[0, e:e + 1, :] == slot, 1.0, 0.0).astype(BF16)
        xs_ref[e, 0] = _dot(onehot, h).astype(BF16)


def _gather(pos_rows, h, cap):
    b, n, d = h.shape
    return pl.pallas_call(
        functools.partial(_gather_kernel, cap=cap), grid=(b,),
        in_specs=[pl.BlockSpec((1, N_EXPERTS, n), lambda i: (i, 0, 0)),
                  pl.BlockSpec((1, n, d), lambda i: (i, 0, 0))],
        out_specs=pl.BlockSpec((N_EXPERTS, 1, cap, d), lambda i: (0, i, 0, 0)),
        out_shape=jax.ShapeDtypeStruct((N_EXPERTS, b, cap, d), BF16),
        compiler_params=_cparams(1), name="moe_gather",
    )(pos_rows, h)


def _ffn_kernel(xs_ref, wg_ref, wu_ref, wd_ref, o_ref):
    xs = xs_ref[0]
    hid = _silu(_dot(xs, wg_ref[0])) * _dot(xs, wu_ref[0])
    o_ref[0] = _dot(hid.astype(BF16), wd_ref[0]).astype(BF16)


def _ffn(xs, wg, wu, wd):
    e, m, d = xs.shape
    ff = wg.shape[2]
    tm = _row_tile(m)
    return pl.pallas_call(
        _ffn_kernel, grid=(e, m // tm),
        in_specs=[pl.BlockSpec((1, tm, d), lambda i, j: (i, j, 0)),
                  pl.BlockSpec((1, d, ff), lambda i, j: (i, 0, 0)),
                  pl.BlockSpec((1, d, ff), lambda i, j: (i, 0, 0)),
                  pl.BlockSpec((1, ff, d), lambda i, j: (i, 0, 0))],
        out_specs=pl.BlockSpec((1, tm, d), lambda i, j: (i, j, 0)),
        out_shape=jax.ShapeDtypeStruct((e, m, d), BF16),
        compiler_params=_cparams(2, 56), name="moe_ffn",
    )(xs, wg, wu, wd)


def _combine_kernel(out_ref, pos_ref, selw_ref, x_ref, mod_ref, *rest, cap, final):
    o_ref = rest[-1]
    x = x_ref[0]
    pos, selw = pos_ref[0], selw_ref[0]
    slot = lax.broadcasted_iota(jnp.int32, (x.shape[0], cap), 1).astype(F32)
    y = jnp.zeros(x.shape, F32)
    for e in range(N_EXPERTS):
        weighted = jnp.where(pos[:, e:e + 1] == slot, selw[:, e:e + 1], 0.0).astype(BF16)
        y = y + _dot(weighted, out_ref[e, 0])
    x = x + mod_ref[0, 5:6, :] * y
    if final:
        x = x * lax.rsqrt(jnp.mean(x * x, axis=-1, keepdims=True) + EPS) * rest[0][...]
    o_ref[0] = x


def _combine(out, pos, selw, x, mod, cap, norm_f):
    b, n, d = x.shape
    tm = _row_tile(n)
    final = norm_f is not None
    in_specs = [pl.BlockSpec((N_EXPERTS, 1, cap, d), lambda i, j: (0, i, 0, 0)),
                pl.BlockSpec((1, tm, N_EXPERTS), lambda i, j: (i, j, 0)),
                pl.BlockSpec((1, tm, N_EXPERTS), lambda i, j: (i, j, 0)),
                pl.BlockSpec((1, tm, d), lambda i, j: (i, j, 0)),
                pl.BlockSpec((1, 6, d), lambda i, j: (i, 0, 0))]
    args = [out, pos, selw, x, mod]
    if final:
        in_specs.append(pl.BlockSpec((1, d), lambda i, j: (0, 0)))
        args.append(norm_f.reshape(1, d))
    return pl.pallas_call(
        functools.partial(_combine_kernel, cap=cap, final=final), grid=(b, n // tm),
        in_specs=in_specs,
        out_specs=pl.BlockSpec((1, tm, d), lambda i, j: (i, j, 0)),
        out_shape=jax.ShapeDtypeStruct((b, n, d), F32),
        compiler_params=_cparams(2), name="moe_combine",
    )(*args)


def _moe(x, mod, g, wr_pad, wg, wu, wd, norm_f=None):
    b, n, d = x.shape
    cap = CAPACITY_FACTOR * n // N_EXPERTS
    groups = LANE // N_EXPERTS
    h, aff = _moe_pre(x, mod, g, wr_pad)
    pos, selw = _route(aff, cap)
    split = lambda a: a.reshape(b, n // groups, groups, N_EXPERTS)
    by_token = lambda a: split(a).transpose(0, 2, 1, 3).reshape(b, n, N_EXPERTS)
    by_expert = lambda a: split(a).transpose(0, 3, 2, 1).reshape(b, N_EXPERTS, n)
    xs = _gather(by_expert(pos), h, cap)
    out = _ffn(xs.reshape(N_EXPERTS, b * cap, d), wg, wu, wd).reshape(N_EXPERTS, b, cap, d)
    return _combine(out, by_token(pos), by_token(selw), x, mod, cap, norm_f)


def kernel(x, c, ctx, c_ctx, w_mod, b_mod, norm1, norm2, s5_a_re, s5_a_im, s5_log_dt, s5_b_re, s5_b_im, s5_c_re, s5_c_im, s5_d, s5_w_glu, na_w_qkv, na_w_o, na_rpb, da_w_qkv, da_w_o, da_lambda, da_subln, hg_w_in, hg_w_o, hg_gnorm, hg_lower_bounds, moe_router, moe_w_gate, moe_w_up, moe_w_down, norm_f):
    b, n, d = x.shape
    depth = w_mod.shape[0]
    assert depth == 4, "layer i uses mixer i; only the last layer drops the context stream"

    rows_pad = -(b + 1) % 8
    cc = jnp.concatenate([c, c_ctx[None, :], jnp.zeros((rows_pad, d), F32)], axis=0)
    mod_all = _modulation(cc, w_mod, b_mod)
    p_lb = jax.nn.softmax(hg_lower_bounds.astype(F32), axis=0)
    lbs = jnp.cumsum(p_lb, axis=0) - p_lb[0]
    h_ctx = ctx

    for i in range(depth):
        last = i == depth - 1
        mod_l = mod_all[i, :b].reshape(b, 6, d)
        mod_c = jnp.broadcast_to(mod_all[i, b].reshape(1, 6, d), (b, 6, d))
        g1 = norm1[i]
        if i == 0:
            tables = _s5_tables(s5_a_re[0], s5_a_im[0], s5_log_dt[0], s5_b_re[0], s5_b_im[0], s5_c_re[0], s5_c_im[0])
            h_ctx, x = _s5_layer(x, h_ctx, mod_l, mod_c, g1, tables, s5_d[0], s5_w_glu[0].astype(BF16))
        elif i == 1:
            w_qkv = na_w_qkv[0].astype(BF16)
            qkv_l = _proj_in(x, mod_l, g1, w_qkv, BF16)
            qkv_c = _proj_in(h_ctx, mod_c, g1, w_qkv, BF16)
            bias = _na_bias_table(na_rpb[0], n // GRID_W)
            w_o = na_w_o[0].astype(BF16)
            x = _proj_out(_na_attention(qkv_l, qkv_c, bias, d), x, mod_l, w_o)
            h_ctx = _proj_out(_attention(qkv_c, None, qkv_c, d, diff=False), h_ctx, mod_c, w_o)
        elif i == 2:
            w_qkv = da_w_qkv[0].astype(BF16)
            qkv_l = _proj_in(x, mod_l, g1, w_qkv, BF16, rope=_rope_tables(n), rope_cols=2 * d)
            qkv_c = _proj_in(h_ctx, mod_c, g1, w_qkv, BF16)
            lam = da_lambda[0].astype(F32)
            lam_init = 0.8 - 0.6 * math.exp(-0.3 * i)
            lam_full = jnp.exp(jnp.sum(lam[0] * lam[1])) - jnp.exp(jnp.sum(lam[2] * lam[3])) + lam_init
            w_o = da_w_o[0].astype(BF16)
            attend = functools.partial(_attention, d=d, diff=True, lam=lam_full, subln=da_subln[0],
                                       post_scale=1.0 - lam_init)
            x = _proj_out(attend(qkv_l, qkv_l, qkv_c), x, mod_l, w_o)
            h_ctx = _proj_out(attend(qkv_c, None, qkv_c), h_ctx, mod_c, w_o)
        else:
            w_in = hg_w_in[0].astype(BF16)
            z_l = _proj_in(x, mod_l, g1, w_in, F32)
            z_c = _proj_in(h_ctx, mod_c, g1, w_in, F32)
            x = _hg_out(_hg_scan(z_l, z_c, lbs[i], d), z_l, hg_gnorm[0], x, mod_l, hg_w_o[0].astype(BF16))

        wr_pad = jnp.pad(moe_router[i].astype(F32), ((0, 0), (0, LANE - N_EXPERTS)))
        wg, wu, wd = moe_w_gate[i].astype(BF16), moe_w_up[i].astype(BF16), moe_w_down[i].astype(BF16)
        x = _moe(x, mod_l, norm2[i], wr_pad, wg, wu, wd, norm_f if last else None)
        if not last:
            h_ctx = _moe(h_ctx, mod_c, norm2[i], wr_pad, wg, wu, wd)
    return x
```

```python
import functools
import math

import jax
import jax.numpy as jnp
import numpy as np
from jax import lax
from jax.experimental import pallas as pl
from jax.experimental.pallas import tpu as pltpu

F32 = jnp.float32
BF16 = jnp.bfloat16
HIGHEST = lax.Precision.HIGHEST
EPS = 1e-6
NEG = -1e30

LANE = 128
GRID_W = 64
WIN_R = 8
WIN_C = 16
NA_ROWS = 4
HEAD_DIM = 64
QUERY_SCALE = HEAD_DIM ** -0.5 * math.log2(math.e)
ROPE_BASE = 10000.0
S5_GROUP = 16
S5_CHUNK = 16
HG_CHUNK = 256
HG_HEADS_PER_STEP = 2
HG_SMALL_LEVELS = 3
N_EXPERTS = 16
CAPACITY_FACTOR = 2
ROW_TILE = 512


def _cparams(n_axes, vmem_mb=48):
    return pltpu.CompilerParams(dimension_semantics=("arbitrary",) * n_axes,
                                vmem_limit_bytes=vmem_mb * 1024 * 1024)


def _dot(a, b):
    return jnp.dot(a, b, preferred_element_type=F32)


def _dot_nt(a, b):
    return lax.dot_general(a, b, (((1,), (1,)), ((), ())), preferred_element_type=F32)


def _normmod(x, g, shift, scale):
    y = x * lax.rsqrt(jnp.mean(x * x, axis=-1, keepdims=True) + EPS)
    return (y * g) * (1.0 + scale) + shift


def _silu(x):
    return x * jax.nn.sigmoid(x)


def _row_tile(n):
    return min(n, ROW_TILE)


def _mod_kernel(c_ref, w_ref, b_ref, o_ref):
    s = _silu(c_ref[...])
    o_ref[0] = jnp.dot(s, w_ref[0], precision=HIGHEST, preferred_element_type=F32) + b_ref[0]


def _modulation(cc, w_mod, b_mod):
    depth, d, d6 = w_mod.shape
    r = cc.shape[0]
    return pl.pallas_call(
        _mod_kernel, grid=(depth, d6 // d),
        in_specs=[pl.BlockSpec((r, d), lambda i, j: (0, 0)),
                  pl.BlockSpec((1, d, d), lambda i, j: (i, 0, j)),
                  pl.BlockSpec((1, 1, d), lambda i, j: (i, 0, j))],
        out_specs=pl.BlockSpec((1, r, d), lambda i, j: (i, 0, j)),
        out_shape=jax.ShapeDtypeStruct((depth, r, d6), F32),
        compiler_params=_cparams(2), name="modulation",
    )(cc, w_mod, b_mod.reshape(depth, 1, d6))


def _proj_in_kernel(x_ref, mod_ref, g_ref, w_ref, *rest, rope_cols, q_cols):
    o_ref = rest[-1]
    h = _normmod(x_ref[0], g_ref[...], mod_ref[0, 0:1, :], mod_ref[0, 1:2, :])
    y = _dot(h.astype(BF16), w_ref[...])
    if rope_cols:
        cos, s_up, s_dn = rest[0][...], rest[1][...], rest[2][...]
        for j in range(rope_cols // LANE):
            t = y[:, j * LANE:(j + 1) * LANE]
            t = t * cos + pltpu.roll(t, LANE - 16, 1) * s_up + pltpu.roll(t, 16, 1) * s_dn
            if j * LANE < q_cols:
                t = t * QUERY_SCALE
            o_ref[0, :, j * LANE:(j + 1) * LANE] = t.astype(o_ref.dtype)
        o_ref[0, :, rope_cols:] = y[:, rope_cols:].astype(o_ref.dtype)
    elif q_cols:
        o_ref[0, :, :q_cols] = (y[:, :q_cols] * QUERY_SCALE).astype(o_ref.dtype)
        o_ref[0, :, q_cols:] = y[:, q_cols:].astype(o_ref.dtype)
    else:
        o_ref[0] = y.astype(o_ref.dtype)


def _proj_in(x, mod, g, w, out_dtype, rope=None, rope_cols=0, q_cols=0):
    b, n, d = x.shape
    nout = w.shape[1]
    tm = _row_tile(n)
    in_specs = [pl.BlockSpec((1, tm, d), lambda i, j: (i, j, 0)),
                pl.BlockSpec((1, 6, d), lambda i, j: (i, 0, 0)),
                pl.BlockSpec((1, d), lambda i, j: (0, 0)),
                pl.BlockSpec((d, nout), lambda i, j: (0, 0))]
    args = [x, mod, g.reshape(1, d), w]
    if rope_cols:
        in_specs += [pl.BlockSpec((tm, LANE), lambda i, j: (j, 0))] * 3
        args += list(rope)
    return pl.pallas_call(
        functools.partial(_proj_in_kernel, rope_cols=rope_cols, q_cols=q_cols), grid=(b, n // tm),
        in_specs=in_specs,
        out_specs=pl.BlockSpec((1, tm, nout), lambda i, j: (i, j, 0)),
        out_shape=jax.ShapeDtypeStruct((b, n, nout), out_dtype),
        compiler_params=_cparams(2, 56), name="proj_in",
    )(*args)


def _rope_tables(n):
    t = jnp.arange(n)
    row = (t // GRID_W).astype(F32)
    col = (t % GRID_W).astype(F32)
    quarter = HEAD_DIM // 4
    inv = ROPE_BASE ** (-jnp.arange(quarter, dtype=F32) / quarter)
    lane = jnp.arange(LANE)
    pos = jnp.where(((lane % HEAD_DIM) // (HEAD_DIM // 2))[None, :] == 0, row[:, None], col[:, None])
    ang = pos * inv[lane % quarter][None, :]
    first = ((lane % (HEAD_DIM // 2)) < quarter)[None, :]
    cos, sin = jnp.cos(ang), jnp.sin(ang)
    return cos, jnp.where(first, -sin, 0.0), jnp.where(first, 0.0, sin)


def _proj_out_kernel(o_ref, x_ref, mod_ref, w_ref, out_ref):
    out_ref[0] = x_ref[0] + mod_ref[0, 2:3, :] * _dot(o_ref[0], w_ref[...])


def _proj_out(o, x, mod, w):
    b, n, d = x.shape
    tm = _row_tile(n)
    return pl.pallas_call(
        _proj_out_kernel, grid=(b, n // tm),
        in_specs=[pl.BlockSpec((1, tm, d), lambda i, j: (i, j, 0)),
                  pl.BlockSpec((1, tm, d), lambda i, j: (i, j, 0)),
                  pl.BlockSpec((1, 6, d), lambda i, j: (i, 0, 0)),
                  pl.BlockSpec((d, d), lambda i, j: (0, 0))],
        out_specs=pl.BlockSpec((1, tm, d), lambda i, j: (i, j, 0)),
        out_shape=jax.ShapeDtypeStruct((b, n, d), F32),
        compiler_params=_cparams(2), name="proj_out",
    )(o, x, mod, w)


def _softmax_parts(scores):
    m = scores[0].max(axis=-1, keepdims=True)
    for s in scores[1:]:
        m = jnp.maximum(m, s.max(axis=-1, keepdims=True))
    es = [jnp.exp2(s - m) for s in scores]
    z = es[0].sum(axis=-1, keepdims=True)
    for e in es[1:]:
        z = z + e.sum(axis=-1, keepdims=True)
    return es, z


def _attn_kernel(*refs, diff, has_lat, post_scale):
    refs = list(refs)
    q_ref = refs.pop(0)
    kv = [(refs.pop(0), refs.pop(0))] if has_lat else []
    kv.append((refs.pop(0), refs.pop(0)))
    if diff:
        lam_ref, sub_ref = refs.pop(0), refs.pop(0)
    o_ref = refs.pop(0)
    tq = q_ref.shape[1]
    lo = lax.broadcasted_iota(jnp.int32, (1, LANE), 1) < HEAD_DIM
    zero = jnp.zeros((), BF16)
    for t in range(q_ref.shape[2] // LANE):
        sl = slice(t * LANE, (t + 1) * LANE)
        qt = q_ref[0, :, sl]
        q0, q1 = jnp.where(lo, qt, zero), jnp.where(lo, zero, qt)
        ks = [k_ref[0, :, sl] for k_ref, _ in kv]
        vs = [v_ref[0, :, sl] for _, v_ref in kv]
        e0, z0 = _softmax_parts([_dot_nt(q0, k) for k in ks])
        e1, z1 = _softmax_parts([_dot_nt(q1, k) for k in ks])
        r0 = 1.0 / z0
        o = jnp.zeros((tq, LANE), F32)
        if diff:
            ratio = lam_ref[0, 0] * z0 / z1
            for a0, a1, v in zip(e0, e1, vs):
                o = o + _dot((a0 - a1 * ratio).astype(BF16), v)
            o = o * r0
            o = o * lax.rsqrt(jnp.mean(o * o, axis=-1, keepdims=True) + EPS) * sub_ref[...] * post_scale
        else:
            for a0, a1, v in zip(e0, e1, vs):
                o = o + _dot(a0.astype(BF16), jnp.where(lo, v, zero))
                o = o + _dot(a1.astype(BF16), jnp.where(lo, zero, v))
            o = o * jnp.where(lo, r0, 1.0 / z1)
        o_ref[0, :, sl] = o.astype(o_ref.dtype)


def _attention(qkv_q, qkv_lat, qkv_ctx, d, diff, lam=None, subln=None, post_scale=1.0):
    b, nq, _ = qkv_q.shape
    tq = min(nq, 256)
    has_lat = qkv_lat is not None
    in_specs = [pl.BlockSpec((1, tq, d), lambda i, j: (i, j, 0))]
    args = [qkv_q]
    for src in ([qkv_lat] if has_lat else []) + [qkv_ctx]:
        nk = src.shape[1]
        in_specs += [pl.BlockSpec((1, nk, d), lambda i, j: (i, 0, 1)),
                     pl.BlockSpec((1, nk, d), lambda i, j: (i, 0, 2))]
        args += [src, src]
    if diff:
        in_specs += [pl.BlockSpec(memory_space=pltpu.SMEM), pl.BlockSpec((1, LANE), lambda i, j: (0, 0))]
        args += [lam.reshape(1, 1), subln.reshape(1, LANE)]
    return pl.pallas_call(
        functools.partial(_attn_kernel, diff=diff, has_lat=has_lat, post_scale=post_scale),
        grid=(b, nq // tq), in_specs=in_specs,
        out_specs=pl.BlockSpec((1, tq, d), lambda i, j: (i, j, 0)),
        out_shape=jax.ShapeDtypeStruct((b, nq, d), BF16),
        compiler_params=_cparams(2, 56), name="attention",
    )(*args)


def _na_kernel(cls_ref, q_ref, k_ref, v_ref, kc_ref, vc_ref, bias_ref, o_ref, *, rows, wr, kr):
    del cls_ref
    kstart = jnp.clip(pl.program_id(1) * NA_ROWS - wr // 2, 0, rows - kr)
    start = pl.multiple_of(kstart * GRID_W, GRID_W)
    nk = kr * GRID_W
    lo = lax.broadcasted_iota(jnp.int32, (1, LANE), 1) < HEAD_DIM
    zero = jnp.zeros((), BF16)
    for t in range(q_ref.shape[2] // LANE):
        sl = slice(t * LANE, (t + 1) * LANE)
        qt = q_ref[0, :, sl]
        kw, vw = k_ref[0, pl.ds(start, nk), sl], v_ref[0, pl.ds(start, nk), sl]
        kc, vc = kc_ref[0, :, sl], vc_ref[0, :, sl]
        o = jnp.zeros((q_ref.shape[1], LANE), F32)
        zs = []
        for half in range(2):
            keep = lo if half == 0 else jnp.logical_not(lo)
            qh = jnp.where(keep, qt, zero)
            (el, ec), z = _softmax_parts([_dot_nt(qh, kw) + bias_ref[0, 2 * t + half], _dot_nt(qh, kc)])
            zs.append(z)
            o = o + _dot(el.astype(BF16), jnp.where(keep, vw, zero))
            o = o + _dot(ec.astype(BF16), jnp.where(keep, vc, zero))
        o_ref[0, :, sl] = (o * jnp.where(lo, 1.0 / zs[0], 1.0 / zs[1])).astype(o_ref.dtype)


def _na_layout(rows):
    wr = min(WIN_R, rows)
    kr = min(rows, wr + NA_ROWS - 1)
    classes, cls_of_block = [], []
    for blk in range(rows // NA_ROWS):
        kstart = min(max(blk * NA_ROWS - wr // 2, 0), rows - kr)
        key = []
        for qr in range(NA_ROWS):
            r = blk * NA_ROWS + qr
            r0 = min(max(r - wr // 2, 0), rows - wr)
            key.append(tuple((kstart + j - r + WIN_R - 1) if r0 <= kstart + j < r0 + wr else None for j in range(kr)))
        key = tuple(key)
        if key not in classes:
            classes.append(key)
        cls_of_block.append(classes.index(key))
    return wr, kr, classes, cls_of_block


def _na_bias_table(rpb, rows):
    heads = rpb.shape[0]
    _, _, classes, _ = _na_layout(rows)
    cq = jnp.arange(GRID_W)
    c0 = jnp.clip(cq - WIN_C // 2, 0, GRID_W - WIN_C)
    col_in = (cq[None, :] >= c0[:, None]) & (cq[None, :] < c0[:, None] + WIN_C)
    dc = jnp.clip(cq[None, :] - cq[:, None] + WIN_C - 1, 0, 2 * WIN_C - 2)
    dc_onehot = (dc[:, :, None] == jnp.arange(2 * WIN_C - 1)[None, None, :]).astype(F32)
    per_dr = jnp.einsum('hrd,qkd->hrqk', rpb.astype(F32) * math.log2(math.e), dc_onehot,
                        precision=HIGHEST)
    per_dr = jnp.where(col_in[None, None], per_dr, NEG)
    masked = jnp.full((heads, GRID_W, GRID_W), NEG, F32)
    tabs = []
    for key in classes:
        slabs = [jnp.concatenate([masked if dr is None else per_dr[:, dr] for dr in row], axis=-1) for row in key]
        tabs.append(jnp.concatenate(slabs, axis=1))
    return jnp.stack(tabs, axis=0)


def _na_attention(qkv_l, qkv_c, bias, d):
    b, n, _ = qkv_l.shape
    nc = qkv_c.shape[1]
    rows = n // GRID_W
    wr, kr, _, cls_of_block = _na_layout(rows)
    heads = d // HEAD_DIM
    tq = NA_ROWS * GRID_W
    grid_spec = pltpu.PrefetchScalarGridSpec(
        num_scalar_prefetch=1, grid=(b, rows // NA_ROWS),
        in_specs=[pl.BlockSpec((1, tq, d), lambda i, r, cls: (i, r, 0)),
                  pl.BlockSpec((1, n, d), lambda i, r, cls: (i, 0, 1)),
                  pl.BlockSpec((1, n, d), lambda i, r, cls: (i, 0, 2)),
                  pl.BlockSpec((1, nc, d), lambda i, r, cls: (i, 0, 1)),
                  pl.BlockSpec((1, nc, d), lambda i, r, cls: (i, 0, 2)),
                  pl.BlockSpec((1, heads, tq, kr * GRID_W), lambda i, r, cls: (cls[r], 0, 0, 0),
                               pipeline_mode=pl.Buffered(1))],
        out_specs=pl.BlockSpec((1, tq, d), lambda i, r, cls: (i, r, 0)))
    return pl.pallas_call(
        functools.partial(_na_kernel, rows=rows, wr=wr, kr=kr),
        grid_spec=grid_spec,
        out_shape=jax.ShapeDtypeStruct((b, n, d), BF16),
        compiler_params=_cparams(2, 56), name="na_attention",
    )(jnp.asarray(cls_of_block, jnp.int32), qkv_l, qkv_l, qkv_l, qkv_c, qkv_c, bias)


def _normmod_kernel(x_ref, mod_ref, g_ref, o_ref):
    o_ref[0] = _normmod(x_ref[0], g_ref[...], mod_ref[0, 0:1, :], mod_ref[0, 1:2, :]).astype(o_ref.dtype)


def _normmod_call(x, mod, g):
    b, n, d = x.shape
    tm = _row_tile(n)
    return pl.pallas_call(
        _normmod_kernel, grid=(b, n // tm),
        in_specs=[pl.BlockSpec((1, tm, d), lambda i, j: (i, j, 0)),
                  pl.BlockSpec((1, 6, d), lambda i, j: (i, 0, 0)),
                  pl.BlockSpec((1, d), lambda i, j: (0, 0))],
        out_specs=pl.BlockSpec((1, tm, d), lambda i, j: (i, j, 0)),
        out_shape=jax.ShapeDtypeStruct((b, n, d), BF16),
        compiler_params=_cparams(2), name="normmod",
    )(x, mod, g.reshape(1, d))


def _s5_tables(a_re, a_im, log_dt, b_re, b_im, c_re, c_im):
    t_len = S5_CHUNK
    a_re, a_im = a_re.astype(F32), a_im.astype(F32)
    dt = jnp.exp(log_dt.astype(F32))[..., None]
    lr, li = a_re * dt, a_im * dt
    cos_li, sin_li = jnp.cos(li), jnp.sin(li)
    ab_im = jnp.exp(lr) * sin_li
    nr = jnp.expm1(lr) * cos_li - 2.0 * jnp.sin(0.5 * li) ** 2
    den = a_re * a_re + a_im * a_im
    fr = (nr * a_re + ab_im * a_im) / den
    fi = (ab_im * a_re - nr * a_im) / den
    b_re, b_im = b_re.astype(F32), b_im.astype(F32)
    bb_re = fr[..., None] * b_re - fi[..., None] * b_im
    bb_im = fr[..., None] * b_im + fi[..., None] * b_re
    tau = jnp.arange(t_len + 1, dtype=F32)[:, None, None, None]
    mag = jnp.exp(tau * lr)
    pr, pi = mag * jnp.cos(tau * li), mag * jnp.sin(tau * li)
    c_re, c_im = c_re.astype(F32), c_im.astype(F32)
    cw_re = c_re[None] * pr[:, :, :, None, :] - c_im[None] * pi[:, :, :, None, :]
    cw_im = c_re[None] * pi[:, :, :, None, :] + c_im[None] * pr[:, :, :, None, :]
    kern = (jnp.einsum('tdgkp,dgph->tdgkh', cw_re, bb_re, precision=HIGHEST)
            - jnp.einsum('tdgkp,dgph->tdgkh', cw_im, bb_im, precision=HIGHEST))
    g = a_re.shape[1]
    h = S5_GROUP
    diff = jnp.arange(t_len)[None, :] - jnp.arange(t_len)[:, None]
    lags = jnp.arange(t_len)[None, None, :]
    fwd_lag = (diff[:, :, None] == lags).astype(F32)
    bwd_lag = (-diff[:, :, None] == lags).astype(F32)
    k_f = jnp.einsum('sta,agkh->stgkh', fwd_lag, kern[:t_len, 0], precision=HIGHEST)
    k_b = jnp.einsum('sta,agkh->stgkh', bwd_lag, kern[:t_len, 1], precision=HIGHEST)
    m_both = (k_f + k_b).transpose(2, 0, 4, 1, 3).reshape(g, t_len * h, t_len * h)

    def in_map(d, p_re, p_im):
        wr_ = p_re[..., None] * bb_re[d][None] - p_im[..., None] * bb_im[d][None]
        wi_ = p_re[..., None] * bb_im[d][None] + p_im[..., None] * bb_re[d][None]
        to_rows = lambda w: w.transpose(1, 0, 3, 2).reshape(g, t_len * h, -1)
        return to_rows(wr_), to_rows(wi_)

    f_re, f_im = in_map(0, pr[:t_len, 0][::-1], pi[:t_len, 0][::-1])
    r_re, r_im = in_map(1, pr[:t_len, 1], pi[:t_len, 1])
    w_in = jnp.concatenate([f_re, f_im, f_im, f_re, r_re, r_im, r_im, r_re], axis=-1)

    def out_map(w_re, w_im):
        to_cols = lambda w: w.transpose(1, 3, 0, 2).reshape(g, -1, t_len * h)
        return jnp.concatenate([to_cols(w_re), -to_cols(w_im)], axis=1)

    w_out = jnp.concatenate([out_map(cw_re[1:, 0], cw_im[1:, 0]),
                             out_map(cw_re[1:, 1][::-1], cw_im[1:, 1][::-1])], axis=1)
    a_pow = jnp.stack([jnp.concatenate([pr[t_len, 0], pr[t_len, 0]], -1),
                       jnp.concatenate([-pi[t_len, 0], pi[t_len, 0]], -1),
                       jnp.concatenate([pr[t_len, 1], pr[t_len, 1]], -1),
                       jnp.concatenate([-pi[t_len, 1], pi[t_len, 1]], -1)], axis=1)
    return m_both.astype(BF16), w_in.astype(BF16), w_out.astype(BF16), a_pow


def _s5_core_kernel(u_ref, m_ref, win_ref, wout_ref, a_ref, y_ref, sall_ref, sin_ref, *, bn, nc_c, nc_l):
    u = u_ref[0]
    sin_ref[...] = _dot(u, win_ref[0])
    a = a_ref[0]
    a1f, a2f, a1b, a2b = a[0:1], a[1:2], a[2:3], a[3:4]
    w = LANE
    zero = jnp.zeros((bn, w), F32)

    def rows_of(c):
        return pl.ds(pl.multiple_of(c * bn, bn), bn)

    def fwd(c, carry):
        s, sw = carry
        r = rows_of(c)
        sall_ref[r, 0:w] = s
        return (a1f * s + a2f * sw + sin_ref[r, 0:w], a1f * sw - a2f * s + sin_ref[r, w:2 * w])

    def bwd(c, carry):
        s, sw = carry
        r = rows_of(c)
        sall_ref[r, w:2 * w] = s
        return (a1b * s + a2b * sw + sin_ref[r, 2 * w:3 * w], a1b * sw - a2b * s + sin_ref[r, 3 * w:4 * w])

    lax.fori_loop(0, nc_c + nc_l, fwd, (zero, zero))
    carry = lax.fori_loop(0, nc_c, lambda i, cr: bwd(nc_c - 1 - i, cr), (zero, zero))
    lax.fori_loop(0, nc_l, lambda i, cr: bwd(nc_c + nc_l - 1 - i, cr), carry)
    y_ref[0] = _dot(u, m_ref[0]) + _dot(sall_ref[...].astype(BF16), wout_ref[0])


def _s5_core(u_rows, tables, bn, nc_c, nc_l):
    g, rows, width = u_rows.shape
    m_both, w_in, w_out, a_pow = tables
    return pl.pallas_call(
        functools.partial(_s5_core_kernel, bn=bn, nc_c=nc_c, nc_l=nc_l), grid=(g,),
        in_specs=[pl.BlockSpec((1, rows, width), lambda i: (i, 0, 0)),
                  pl.BlockSpec((1, width, width), lambda i: (i, 0, 0)),
                  pl.BlockSpec((1, width, 4 * LANE), lambda i: (i, 0, 0)),
                  pl.BlockSpec((1, 2 * LANE, width), lambda i: (i, 0, 0)),
                  pl.BlockSpec((1, 4, LANE), lambda i: (i, 0, 0))],
        out_specs=pl.BlockSpec((1, rows, width), lambda i: (i, 0, 0)),
        out_shape=jax.ShapeDtypeStruct((g, rows, width), F32),
        scratch_shapes=[pltpu.VMEM((rows, 2 * LANE), F32), pltpu.VMEM((rows, 4 * LANE), F32)],
        compiler_params=_cparams(1, 56), name="s5_core",
    )(u_rows, m_both, w_in, w_out, a_pow)


def _s5_glu_kernel(x_ref, y_ref, mod_ref, g_ref, d_ref, w_ref, out_ref):
    x = x_ref[0]
    d = x.shape[1]
    u = _normmod(x, g_ref[...], mod_ref[0, 0:1, :], mod_ref[0, 1:2, :])
    z = jax.nn.gelu(y_ref[0] + d_ref[...] * u)
    zz = _dot(z.astype(BF16), w_ref[...])
    out_ref[0] = x + mod_ref[0, 2:3, :] * (zz[:, :d] * jax.nn.sigmoid(zz[:, d:]))


def _s5_glu(x, y, mod, g, dskip, w_glu):
    b, n, d = x.shape
    tm = _row_tile(n)
    return pl.pallas_call(
        _s5_glu_kernel, grid=(b, n // tm),
        in_specs=[pl.BlockSpec((1, tm, d), lambda i, j: (i, j, 0)),
                  pl.BlockSpec((1, tm, d), lambda i, j: (i, j, 0)),
                  pl.BlockSpec((1, 6, d), lambda i, j: (i, 0, 0)),
                  pl.BlockSpec((1, d), lambda i, j: (0, 0)),
                  pl.BlockSpec((1, d), lambda i, j: (0, 0)),
                  pl.BlockSpec((d, 2 * d), lambda i, j: (0, 0))],
        out_specs=pl.BlockSpec((1, tm, d), lambda i, j: (i, j, 0)),
        out_shape=jax.ShapeDtypeStruct((b, n, d), F32),
        compiler_params=_cparams(2), name="s5_glu",
    )(x, y, mod, g.reshape(1, d), dskip.reshape(1, d), w_glu)


def _s5_layer(x, h_ctx, mod_l, mod_c, g, tables, dskip, w_glu):
    b, n, d = x.shape
    n_c = h_ctx.shape[1]
    t_len, h = S5_CHUNK, S5_GROUP
    groups = d // h
    nc_l, nc_c = n // t_len, n_c // t_len

    def to_rows(u, nc):
        return u.reshape(b, nc, t_len, groups, h).transpose(3, 1, 0, 2, 4).reshape(groups, nc * b, t_len * h)

    def from_rows(y, nc):
        return y.reshape(groups, nc, b, t_len, h).transpose(2, 1, 3, 0, 4).reshape(b, nc * t_len, d)

    u_rows = jnp.concatenate([to_rows(_normmod_call(h_ctx, mod_c, g), nc_c),
                              to_rows(_normmod_call(x, mod_l, g), nc_l)], axis=1)
    y_rows = _s5_core(u_rows, tables, b, nc_c, nc_l)
    y_c = from_rows(y_rows[:, :nc_c * b], nc_c)
    y_l = from_rows(y_rows[:, nc_c * b:], nc_l)
    return (_s5_glu(h_ctx, y_c, mod_c, g, dskip, w_glu), _s5_glu(x, y_l, mod_l, g, dskip, w_glu))


def _hg_kernel(ql_ref, ffl_ref, fbl_ref, il_ref, ffc_ref, fbc_ref, ic_ref, lb_ref, wf_ref, wb_ref, mask_ref,
               o_ref, ob_ref, sf_ref, sb_ref, *, nc_c, nc_l):
    cn = HG_CHUNK
    levels = cn.bit_length() - 1
    small = min(levels, HG_SMALL_LEVELS)
    nh = HG_HEADS_PER_STEP
    lb = lb_ref[...]
    row = lax.broadcasted_iota(jnp.int32, (cn, 1), 0)

    def gates(ff):
        f = lb + (1.0 - lb) * jax.nn.sigmoid(ff)
        return jnp.log(f), 1.0 - f

    def dot3(w01, x):
        hi = x.astype(BF16)
        rest = x - hi.astype(F32)
        mid = rest.astype(BF16)
        low = (rest - mid.astype(F32)).astype(BF16)
        return _dot(w01, hi) + _dot(w01, mid) + _dot(w01, low)

    def visit(rows, q_ref, ff_ref, i_ref, w_ref, s_ref, out_ref, d_idx):
        fwd = d_idx == 0
        lf, k = gates(ff_ref[0, rows, :])
        v = _silu(i_ref[0, rows, :])
        n_blocks = 1 if out_ref is None else 1 + small
        sums = dot3(w_ref[0:n_blocks * cn, :], lf)
        cum = sums[0:cn]
        total = cum[cn - 1:cn] if fwd else cum[0:1]
        kd = (k * jnp.exp(total - cum)).astype(BF16)
        decay = jnp.exp(total)
        vb = v.astype(BF16)
        if out_ref is not None:
            q = q_ref[0, rows, :]
            qd = (q * jnp.exp(cum)).astype(BF16)
            qk = q * k
            scaled = []
            for lv in range(levels):
                bs = cn >> (lv + 1)
                is_query = ((row // bs) % 2 == 1) if fwd else ((row // bs) % 2 == 0)
                if lv < levels - small:
                    pairs = cum.reshape(cn // (2 * bs), 2 * bs, cum.shape[1])
                    a_row = bs - 1 if fwd else bs
                    anchor = jnp.broadcast_to(pairs[:, a_row:a_row + 1, :], pairs.shape).reshape(cum.shape)
                    to_anchor = jnp.where(is_query, cum - anchor, anchor - cum)
                else:
                    blk = 1 + lv - (levels - small)
                    to_anchor = sums[blk * cn:(blk + 1) * cn]
                scaled.append((jnp.where(is_query, q, k) * jnp.exp(to_anchor)).astype(BF16))
            outs = []
            for hh in range(nh):
                sl = slice(hh * LANE, (hh + 1) * LANE)
                att = jnp.zeros((cn, cn), F32)
                for lv in range(levels):
                    x = scaled[lv][:, sl]
                    att = att + jnp.where(mask_ref[d_idx * levels + lv] > 0.0, _dot_nt(x, x), 0.0)
                same_token = jnp.sum(qk[:, sl], axis=-1, keepdims=True) * v[:, sl]
                outs.append(_dot_nt(qd[:, sl], s_ref[hh].astype(BF16)) + _dot(att.astype(BF16), vb[:, sl]) + same_token)
            dst = out_ref.at[0] if len(out_ref.shape) == 3 else out_ref
            dst[rows, :] = jnp.concatenate(outs, axis=-1)
        for hh in range(nh):
            sl = slice(hh * LANE, (hh + 1) * LANE)
            s_ref[hh] = s_ref[hh] * decay[:, sl] + _dot(v[:, sl].T.astype(BF16), kd[:, sl])

    def rows_of(c):
        return pl.ds(pl.multiple_of(c * cn, cn), cn)

    sf_ref[...] = jnp.zeros(sf_ref.shape, F32)
    sb_ref[...] = jnp.zeros(sb_ref.shape, F32)

    def ctx_body(j, carry):
        visit(rows_of(j), None, ffc_ref, ic_ref, wf_ref, sf_ref, None, 0)
        visit(rows_of(nc_c - 1 - j), None, fbc_ref, ic_ref, wb_ref, sb_ref, None, 1)
        return carry

    def lat_body(j, carry):
        visit(rows_of(j), ql_ref, ffl_ref, il_ref, wf_ref, sf_ref, o_ref, 0)
        visit(rows_of(nc_l - 1 - j), ql_ref, fbl_ref, il_ref, wb_ref, sb_ref, ob_ref, 1)
        return carry

    lax.fori_loop(0, nc_c, ctx_body, 0)
    lax.fori_loop(0, nc_l, lat_body, 0)
    o_ref[0] = o_ref[0] + ob_ref[...]


def _hg_operators():
    cn = HG_CHUNK
    levels = cn.bit_length() - 1
    t = np.arange(cn)[:, None]
    r = np.arange(cn)[None, :]
    small = min(levels, HG_SMALL_LEVELS)
    w = np.zeros((2, 1 + small, cn, cn), np.float32)
    mask = np.zeros((2, levels, cn, cn), np.float32)
    w[0, 0] = r <= t
    w[1, 0] = r >= t
    for lv in range(levels):
        bs = cn >> (lv + 1)
        parent = t // (2 * bs) * (2 * bs)
        second = (t // bs) % 2 == 1
        blk = 1 + lv - (levels - small)
        if blk >= 1:
            anchor_f = parent + bs - 1
            w[0, blk] = np.where(second, (r > anchor_f) & (r <= t), (r > t) & (r <= anchor_f))
            anchor_b = parent + bs
            w[1, blk] = np.where(second, (r >= anchor_b) & (r < t), (r >= t) & (r < anchor_b))
        same_parent = (t // (2 * bs)) == (r // (2 * bs))
        key_second = (r // bs) % 2 == 1
        mask[0, lv] = same_parent & second & ~key_second
        mask[1, lv] = same_parent & ~second & key_second
    return (jnp.asarray(w.reshape(2, (1 + small) * cn, cn), BF16),
            jnp.asarray(mask.reshape(2 * levels, cn, cn), F32))


def _hg_scan(z_l, z_c, lb, d):
    b, n, _ = z_l.shape
    n_c = z_c.shape[1]
    width = HG_HEADS_PER_STEP * LANE
    steps = d // width
    cn = HG_CHUNK
    w, mask = _hg_operators()
    col = lambda k: (lambda i, h: (i, 0, k * steps + h))
    lat = lambda k: pl.BlockSpec((1, n, width), col(k))
    ctx = lambda k: pl.BlockSpec((1, n_c, width), col(k))
    const = lambda a: pl.BlockSpec(a.shape, lambda i, h: (0,) * a.ndim)
    return pl.pallas_call(
        functools.partial(_hg_kernel, nc_c=n_c // cn, nc_l=n // cn), grid=(b, steps),
        in_specs=[lat(0), lat(1), lat(2), lat(3), ctx(1), ctx(2), ctx(3),
                  pl.BlockSpec((1, width), lambda i, h: (0, h)),
                  const(w[0]), const(w[1]), const(mask)],
        out_specs=pl.BlockSpec((1, n, width), lambda i, h: (i, 0, h)),
        out_shape=jax.ShapeDtypeStruct((b, n, d), F32),
        scratch_shapes=[pltpu.VMEM((n, width), F32),
                        pltpu.VMEM((HG_HEADS_PER_STEP, LANE, LANE), F32),
                        pltpu.VMEM((HG_HEADS_PER_STEP, LANE, LANE), F32)],
        compiler_params=_cparams(2), name="hgrn2_scan",
    )(z_l, z_l, z_l, z_l, z_c, z_c, z_c, lb.reshape(1, d), w[0], w[1], mask)


def _hg_out_kernel(o_ref, z_ref, gn_ref, x_ref, mod_ref, w_ref, out_ref):
    o, gate = o_ref[0], z_ref[0]
    parts = []
    for h in range(o.shape[1] // LANE):
        sl = slice(h * LANE, (h + 1) * LANE)
        t = o[:, sl]
        t = t * lax.rsqrt(jnp.mean(t * t, axis=-1, keepdims=True) + EPS) * gn_ref[...]
        parts.append((t * _silu(gate[:, sl])).astype(BF16))
    y = _dot(jnp.concatenate(parts, axis=-1), w_ref[...])
    out_ref[0] = x_ref[0] + mod_ref[0, 2:3, :] * y


def _hg_out(o, z, gnorm, x, mod, w):
    b, n, d = x.shape
    tm = _row_tile(n)
    return pl.pallas_call(
        _hg_out_kernel, grid=(b, n // tm),
        in_specs=[pl.BlockSpec((1, tm, d), lambda i, j: (i, j, 0)),
                  pl.BlockSpec((1, tm, d), lambda i, j: (i, j, 4)),
                  pl.BlockSpec((1, LANE), lambda i, j: (0, 0)),
                  pl.BlockSpec((1, tm, d), lambda i, j: (i, j, 0)),
                  pl.BlockSpec((1, 6, d), lambda i, j: (i, 0, 0)),
                  pl.BlockSpec((d, d), lambda i, j: (0, 0))],
        out_specs=pl.BlockSpec((1, tm, d), lambda i, j: (i, j, 0)),
        out_shape=jax.ShapeDtypeStruct((b, n, d), F32),
        compiler_params=_cparams(2), name="hgrn2_out",
    )(o, z, gnorm.reshape(1, LANE), x, mod, w)


def _moe_pre_kernel(x_ref, mod_ref, g_ref, wr_ref, h_ref, aff_ref):
    h = _normmod(x_ref[0], g_ref[...], mod_ref[0, 3:4, :], mod_ref[0, 4:5, :])
    h_ref[0] = h.astype(BF16)
    logits = jnp.dot(h, wr_ref[...], precision=HIGHEST, preferred_element_type=F32)
    lane = lax.broadcasted_iota(jnp.int32, logits.shape, 1)
    logits = jnp.where(lane < N_EXPERTS, logits, NEG)
    e = jnp.exp(logits - logits.max(axis=-1, keepdims=True))
    aff_ref[0] = e / e.sum(axis=-1, keepdims=True)


def _moe_pre(x, mod, g, wr_pad):
    b, n, d = x.shape
    tm = _row_tile(n)
    return pl.pallas_call(
        _moe_pre_kernel, grid=(b, n // tm),
        in_specs=[pl.BlockSpec((1, tm, d), lambda i, j: (i, j, 0)),
                  pl.BlockSpec((1, 6, d), lambda i, j: (i, 0, 0)),
                  pl.BlockSpec((1, d), lambda i, j: (0, 0)),
                  pl.BlockSpec((d, LANE), lambda i, j: (0, 0))],
        out_specs=[pl.BlockSpec((1, tm, d), lambda i, j: (i, j, 0)),
                   pl.BlockSpec((1, tm, LANE), lambda i, j: (i, j, 0))],
        out_shape=[jax.ShapeDtypeStruct((b, n, d), BF16), jax.ShapeDtypeStruct((b, n, LANE), F32)],
        compiler_params=_cparams(2), name="moe_pre",
    )(x, mod, g.reshape(1, d), wr_pad)


def _route_kernel(aff_ref, tri_ref, pos_ref, selw_ref, *, cap):
    groups = LANE // N_EXPERTS
    rg = aff_ref.shape[1] // groups
    dense = aff_ref[0, 0:rg, :]
    for g in range(1, groups):
        dense = dense + pltpu.roll(aff_ref[0, g * rg:(g + 1) * rg, :], g * N_EXPERTS, 1)
    bits = lax.bitcast_convert_type(dense, jnp.int32)
    lane = lax.broadcasted_iota(jnp.int32, (1, LANE), 1)
    tri = tri_ref[...]

    def indicator(mask):
        return jnp.where(mask, jnp.ones((), F32), jnp.zeros((), F32))

    def over_groups(row):
        for shift in (LANE // 2, LANE // 4, LANE // 8):
            row = row + pltpu.roll(row, shift, 1)
        return row

    def before_groups(row):
        out = jnp.zeros_like(row)
        for j in range(1, groups):
            out = out + jnp.where(lane >= j * N_EXPERTS, pltpu.roll(row, j * N_EXPERTS, 1), 0.0)
        return out

    def count(mask):
        return over_groups(jnp.sum(indicator(mask), axis=0, keepdims=True))

    def prefix(mask):
        x = indicator(mask)
        return _dot(tri, x.astype(BF16)) + before_groups(jnp.sum(x, axis=0, keepdims=True))

    def search(i, thr):
        cand = thr | jnp.left_shift(jnp.int32(1), 30 - i)
        return jnp.where(count(bits >= cand) >= cap, cand, thr)

    thr = lax.fori_loop(0, 31, search, jnp.zeros((1, LANE), jnp.int32))
    above, tie = bits > thr, bits == thr
    need = cap - count(above)
    tie_rank = prefix(tie)
    sel = above | (tie & (tie_rank <= need))
    pos_ref[0] = jnp.where(sel, prefix(above) + jnp.minimum(tie_rank, need) - 1.0, -1.0)
    selw_ref[0] = jnp.where(sel, dense, 0.0)


def _route(aff, cap):
    b, n, _ = aff.shape
    rg = n // (LANE // N_EXPERTS)
    idx = jnp.arange(rg)
    tri = (idx[None, :] <= idx[:, None]).astype(BF16)
    out_spec = pl.BlockSpec((1, rg, LANE), lambda i: (i, 0, 0))
    return pl.pallas_call(
        functools.partial(_route_kernel, cap=cap), grid=(b,),
        in_specs=[pl.BlockSpec((1, n, LANE), lambda i: (i, 0, 0)), pl.BlockSpec((rg, rg), lambda i: (0, 0))],
        out_specs=[out_spec, out_spec],
        out_shape=[jax.ShapeDtypeStruct((b, rg, LANE), F32)] * 2,
        compiler_params=_cparams(1), name="moe_route",
    )(aff, tri)


def _gather_kernel(pos_ref, h_ref, xs_ref, *, cap):
    h = h_ref[0]
    slot = lax.broadcasted_iota(jnp.int32, (cap, h.shape[0]), 0).astype(F32)
    for e in range(N_EXPERTS):
        onehot = jnp.where(pos_ref[0, e:e + 1, :] == slot, 1.0, 0.0).astype(BF16)
        xs_ref[e, 0] = _dot(onehot, h).astype(BF16)


def _gather(pos_rows, h, cap):
    b, n, d = h.shape
    return pl.pallas_call(
        functools.partial(_gather_kernel, cap=cap), grid=(b,),
        in_specs=[pl.BlockSpec((1, N_EXPERTS, n), lambda i: (i, 0, 0)),
                  pl.BlockSpec((1, n, d), lambda i: (i, 0, 0))],
        out_specs=pl.BlockSpec((N_EXPERTS, 1, cap, d), lambda i: (0, i, 0, 0)),
        out_shape=jax.ShapeDtypeStruct((N_EXPERTS, b, cap, d), BF16),
        compiler_params=_cparams(1), name="moe_gather",
    )(pos_rows, h)


def _ffn_kernel(xs_ref, wg_ref, wu_ref, wd_ref, o_ref):
    xs = xs_ref[0]
    hid = _silu(_dot(xs, wg_ref[0])) * _dot(xs, wu_ref[0])
    o_ref[0] = _dot(hid.astype(BF16), wd_ref[0]).astype(BF16)


def _ffn(xs, wg, wu, wd):
    e, m, d = xs.shape
    ff = wg.shape[2]
    tm = _row_tile(m)
    return pl.pallas_call(
        _ffn_kernel, grid=(e, m // tm),
        in_specs=[pl.BlockSpec((1, tm, d), lambda i, j: (i, j, 0)),
                  pl.BlockSpec((1, d, ff), lambda i, j: (i, 0, 0)),
                  pl.BlockSpec((1, d, ff), lambda i, j: (i, 0, 0)),
                  pl.BlockSpec((1, ff, d), lambda i, j: (i, 0, 0))],
        out_specs=pl.BlockSpec((1, tm, d), lambda i, j: (i, j, 0)),
        out_shape=jax.ShapeDtypeStruct((e, m, d), BF16),
        compiler_params=_cparams(2, 56), name="moe_ffn",
    )(xs, wg, wu, wd)


def _combine_kernel(out_ref, pos_ref, selw_ref, x_ref, mod_ref, *rest, cap, final):
    o_ref = rest[-1]
    x = x_ref[0]
    pos, selw = pos_ref[0], selw_ref[0]
    slot = lax.broadcasted_iota(jnp.int32, (x.shape[0], cap), 1).astype(F32)
    y = jnp.zeros(x.shape, F32)
    for e in range(N_EXPERTS):
        weighted = jnp.where(pos[:, e:e + 1] == slot, selw[:, e:e + 1], 0.0).astype(BF16)
        y = y + _dot(weighted, out_ref[e, 0])
    x = x + mod_ref[0, 5:6, :] * y
    if final:
        x = x * lax.rsqrt(jnp.mean(x * x, axis=-1, keepdims=True) + EPS) * rest[0][...]
    o_ref[0] = x


def _combine(out, pos, selw, x, mod, cap, norm_f):
    b, n, d = x.shape
    tm = _row_tile(n)
    final = norm_f is not None
    in_specs = [pl.BlockSpec((N_EXPERTS, 1, cap, d), lambda i, j: (0, i, 0, 0)),
                pl.BlockSpec((1, tm, N_EXPERTS), lambda i, j: (i, j, 0)),
                pl.BlockSpec((1, tm, N_EXPERTS), lambda i, j: (i, j, 0)),
                pl.BlockSpec((1, tm, d), lambda i, j: (i, j, 0)),
                pl.BlockSpec((1, 6, d), lambda i, j: (i, 0, 0))]
    args = [out, pos, selw, x, mod]
    if final:
        in_specs.append(pl.BlockSpec((1, d), lambda i, j: (0, 0)))
        args.append(norm_f.reshape(1, d))
    return pl.pallas_call(
        functools.partial(_combine_kernel, cap=cap, final=final), grid=(b, n // tm),
        in_specs=in_specs,
        out_specs=pl.BlockSpec((1, tm, d), lambda i, j: (i, j, 0)),
        out_shape=jax.ShapeDtypeStruct((b, n, d), F32),
        compiler_params=_cparams(2), name="moe_combine",
    )(*args)


def _moe(x, mod, g, wr_pad, wg, wu, wd, norm_f=None):
    b, n, d = x.shape
    cap = CAPACITY_FACTOR * n // N_EXPERTS
    groups = LANE // N_EXPERTS
    h, aff = _moe_pre(x, mod, g, wr_pad)
    pos, selw = _route(aff, cap)
    split = lambda a: a.reshape(b, n // groups, groups, N_EXPERTS)
    by_token = lambda a: split(a).transpose(0, 2, 1, 3).reshape(b, n, N_EXPERTS)
    by_expert = lambda a: split(a).transpose(0, 3, 2, 1).reshape(b, N_EXPERTS, n)
    xs = _gather(by_expert(pos), h, cap)
    out = _ffn(xs.reshape(N_EXPERTS, b * cap, d), wg, wu, wd).reshape(N_EXPERTS, b, cap, d)
    return _combine(out, by_token(pos), by_token(selw), x, mod, cap, norm_f)


def kernel(x, c, ctx, c_ctx, w_mod, b_mod, norm1, norm2, s5_a_re, s5_a_im, s5_log_dt, s5_b_re, s5_b_im, s5_c_re, s5_c_im, s5_d, s5_w_glu, na_w_qkv, na_w_o, na_rpb, da_w_qkv, da_w_o, da_lambda, da_subln, hg_w_in, hg_w_o, hg_gnorm, hg_lower_bounds, moe_router, moe_w_gate, moe_w_up, moe_w_down, norm_f):
    b, n, d = x.shape
    depth = w_mod.shape[0]
    assert depth == 4, "layer i uses mixer i; only the last layer drops the context stream"

    rows_pad = -(b + 1) % 8
    cc = jnp.concatenate([c, c_ctx[None, :], jnp.zeros((rows_pad, d), F32)], axis=0)
    mod_all = _modulation(cc, w_mod, b_mod)
    p_lb = jax.nn.softmax(hg_lower_bounds.astype(F32), axis=0)
    lbs = jnp.cumsum(p_lb, axis=0) - p_lb[0]
    h_ctx = ctx

    for i in range(depth):
        last = i == depth - 1
        mod_l = mod_all[i, :b].reshape(b, 6, d)
        mod_c = jnp.broadcast_to(mod_all[i, b].reshape(1, 6, d), (b, 6, d))
        g1 = norm1[i]
        if i == 0:
            tables = _s5_tables(s5_a_re[0], s5_a_im[0], s5_log_dt[0], s5_b_re[0], s5_b_im[0], s5_c_re[0], s5_c_im[0])
            h_ctx, x = _s5_layer(x, h_ctx, mod_l, mod_c, g1, tables, s5_d[0], s5_w_glu[0].astype(BF16))
        elif i == 1:
            w_qkv = na_w_qkv[0].astype(BF16)
            qkv_l = _proj_in(x, mod_l, g1, w_qkv, BF16, q_cols=d)
            qkv_c = _proj_in(h_ctx, mod_c, g1, w_qkv, BF16, q_cols=d)
            bias = _na_bias_table(na_rpb[0], n // GRID_W)
            w_o = na_w_o[0].astype(BF16)
            x = _proj_out(_na_attention(qkv_l, qkv_c, bias, d), x, mod_l, w_o)
            h_ctx = _proj_out(_attention(qkv_c, None, qkv_c, d, diff=False), h_ctx, mod_c, w_o)
        elif i == 2:
            w_qkv = da_w_qkv[0].astype(BF16)
            qkv_l = _proj_in(x, mod_l, g1, w_qkv, BF16, rope=_rope_tables(n), rope_cols=2 * d, q_cols=d)
            qkv_c = _proj_in(h_ctx, mod_c, g1, w_qkv, BF16, q_cols=d)
            lam = da_lambda[0].astype(F32)
            lam_init = 0.8 - 0.6 * math.exp(-0.3 * i)
            lam_full = jnp.exp(jnp.sum(lam[0] * lam[1])) - jnp.exp(jnp.sum(lam[2] * lam[3])) + lam_init
            w_o = da_w_o[0].astype(BF16)
            attend = functools.partial(_attention, d=d, diff=True, lam=lam_full, subln=da_subln[0],
                                       post_scale=1.0 - lam_init)
            x = _proj_out(attend(qkv_l, qkv_l, qkv_c), x, mod_l, w_o)
            h_ctx = _proj_out(attend(qkv_c, None, qkv_c), h_ctx, mod_c, w_o)
        else:
            w_in = hg_w_in[0].astype(BF16)
            z_l = _proj_in(x, mod_l, g1, w_in, F32)
            z_c = _proj_in(h_ctx, mod_c, g1, w_in, F32)
            x = _hg_out(_hg_scan(z_l, z_c, lbs[i], d), z_l, hg_gnorm[0], x, mod_l, hg_w_o[0].astype(BF16))

        wr_pad = jnp.pad(moe_router[i].astype(F32), ((0, 0), (0, LANE - N_EXPERTS)))
        wg, wu, wd = moe_w_gate[i].astype(BF16), moe_w_up[i].astype(BF16), moe_w_down[i].astype(BF16)
        x = _moe(x, mod_l, norm2[i], wr_pad, wg, wu, wd, norm_f if last else None)
        if not last:
            h_ctx = _moe(h_ctx, mod_c, norm2[i], wr_pad, wg, wu, wd)
    return x
```

```python
import functools
import math

import jax
import jax.numpy as jnp
import numpy as np
from jax import lax
from jax.experimental import pallas as pl
from jax.experimental.pallas import tpu as pltpu

F32 = jnp.float32
BF16 = jnp.bfloat16
HIGHEST = lax.Precision.HIGHEST
EPS = 1e-6
NEG = -1e30

LANE = 128
GRID_W = 64
WIN_R = 8
WIN_C = 16
NA_ROWS = 4
HEAD_DIM = 64
QUERY_SCALE = HEAD_DIM ** -0.5 * math.log2(math.e)
ROPE_BASE = 10000.0
S5_GROUP = 16
S5_CHUNK = 16
HG_CHUNK = 256
HG_HEADS_PER_STEP = 2
HG_SMALL_LEVELS = 3
N_EXPERTS = 16
CAPACITY_FACTOR = 2
ROUTE_SAMPLES = 4
ROW_TILE = 512


def _cparams(n_axes, vmem_mb=48):
    return pltpu.CompilerParams(dimension_semantics=("arbitrary",) * n_axes,
                                vmem_limit_bytes=vmem_mb * 1024 * 1024)


def _dot(a, b):
    return jnp.dot(a, b, preferred_element_type=F32)


def _dot_nt(a, b):
    return lax.dot_general(a, b, (((1,), (1,)), ((), ())), preferred_element_type=F32)


def _normmod(x, g, shift, scale):
    y = x * lax.rsqrt(jnp.mean(x * x, axis=-1, keepdims=True) + EPS)
    return (y * g) * (1.0 + scale) + shift


def _silu(x):
    return x * jax.nn.sigmoid(x)


def _row_tile(n):
    return min(n, ROW_TILE)


def _mod_kernel(c_ref, w_ref, b_ref, o_ref):
    s = _silu(c_ref[...])
    o_ref[0] = jnp.dot(s, w_ref[0], precision=HIGHEST, preferred_element_type=F32) + b_ref[0]


def _modulation(cc, w_mod, b_mod):
    depth, d, d6 = w_mod.shape
    r = cc.shape[0]
    return pl.pallas_call(
        _mod_kernel, grid=(depth, d6 // d),
        in_specs=[pl.BlockSpec((r, d), lambda i, j: (0, 0)),
                  pl.BlockSpec((1, d, d), lambda i, j: (i, 0, j)),
                  pl.BlockSpec((1, 1, d), lambda i, j: (i, 0, j))],
        out_specs=pl.BlockSpec((1, r, d), lambda i, j: (i, 0, j)),
        out_shape=jax.ShapeDtypeStruct((depth, r, d6), F32),
        compiler_params=_cparams(2), name="modulation",
    )(cc, w_mod, b_mod.reshape(depth, 1, d6))


def _proj_in_kernel(x_ref, mod_ref, g_ref, w_ref, *rest, rope_cols, q_cols):
    o_ref = rest[-1]
    h = _normmod(x_ref[0], g_ref[...], mod_ref[0, 0:1, :], mod_ref[0, 1:2, :])
    y = _dot(h.astype(BF16), w_ref[...])
    if rope_cols:
        cos, s_up, s_dn = rest[0][...], rest[1][...], rest[2][...]
        for j in range(rope_cols // LANE):
            t = y[:, j * LANE:(j + 1) * LANE]
            t = t * cos + pltpu.roll(t, LANE - 16, 1) * s_up + pltpu.roll(t, 16, 1) * s_dn
            if j * LANE < q_cols:
                t = t * QUERY_SCALE
            o_ref[0, :, j * LANE:(j + 1) * LANE] = t.astype(o_ref.dtype)
        o_ref[0, :, rope_cols:] = y[:, rope_cols:].astype(o_ref.dtype)
    elif q_cols:
        o_ref[0, :, :q_cols] = (y[:, :q_cols] * QUERY_SCALE).astype(o_ref.dtype)
        o_ref[0, :, q_cols:] = y[:, q_cols:].astype(o_ref.dtype)
    else:
        o_ref[0] = y.astype(o_ref.dtype)


def _proj_in(x, mod, g, w, out_dtype, rope=None, rope_cols=0, q_cols=0):
    b, n, d = x.shape
    nout = w.shape[1]
    tm = _row_tile(n)
    in_specs = [pl.BlockSpec((1, tm, d), lambda i, j: (i, j, 0)),
                pl.BlockSpec((1, 6, d), lambda i, j: (i, 0, 0)),
                pl.BlockSpec((1, d), lambda i, j: (0, 0)),
                pl.BlockSpec((d, nout), lambda i, j: (0, 0))]
    args = [x, mod, g.reshape(1, d), w]
    if rope_cols:
        in_specs += [pl.BlockSpec((tm, LANE), lambda i, j: (j, 0))] * 3
        args += list(rope)
    return pl.pallas_call(
        functools.partial(_proj_in_kernel, rope_cols=rope_cols, q_cols=q_cols), grid=(b, n // tm),
        in_specs=in_specs,
        out_specs=pl.BlockSpec((1, tm, nout), lambda i, j: (i, j, 0)),
        out_shape=jax.ShapeDtypeStruct((b, n, nout), out_dtype),
        compiler_params=_cparams(2, 56), name="proj_in",
    )(*args)


def _rope_tables(n):
    t = jnp.arange(n)
    row = (t // GRID_W).astype(F32)
    col = (t % GRID_W).astype(F32)
    quarter = HEAD_DIM // 4
    inv = ROPE_BASE ** (-jnp.arange(quarter, dtype=F32) / quarter)
    lane = jnp.arange(LANE)
    pos = jnp.where(((lane % HEAD_DIM) // (HEAD_DIM // 2))[None, :] == 0, row[:, None], col[:, None])
    ang = pos * inv[lane % quarter][None, :]
    first = ((lane % (HEAD_DIM // 2)) < quarter)[None, :]
    cos, sin = jnp.cos(ang), jnp.sin(ang)
    return cos, jnp.where(first, -sin, 0.0), jnp.where(first, 0.0, sin)


def _proj_out_kernel(o_ref, x_ref, mod_ref, w_ref, out_ref):
    out_ref[0] = x_ref[0] + mod_ref[0, 2:3, :] * _dot(o_ref[0], w_ref[...])


def _proj_out(o, x, mod, w):
    b, n, d = x.shape
    tm = _row_tile(n)
    return pl.pallas_call(
        _proj_out_kernel, grid=(b, n // tm),
        in_specs=[pl.BlockSpec((1, tm, d), lambda i, j: (i, j, 0)),
                  pl.BlockSpec((1, tm, d), lambda i, j: (i, j, 0)),
                  pl.BlockSpec((1, 6, d), lambda i, j: (i, 0, 0)),
                  pl.BlockSpec((d, d), lambda i, j: (0, 0))],
        out_specs=pl.BlockSpec((1, tm, d), lambda i, j: (i, j, 0)),
        out_shape=jax.ShapeDtypeStruct((b, n, d), F32),
        compiler_params=_cparams(2), name="proj_out",
    )(o, x, mod, w)


def _softmax_parts(scores):
    m = scores[0].max(axis=-1, keepdims=True)
    for s in scores[1:]:
        m = jnp.maximum(m, s.max(axis=-1, keepdims=True))
    es = [jnp.exp2(s - m) for s in scores]
    z = es[0].sum(axis=-1, keepdims=True)
    for e in es[1:]:
        z = z + e.sum(axis=-1, keepdims=True)
    return es, z


def _attn_kernel(*refs, diff, has_lat, post_scale):
    refs = list(refs)
    q_ref = refs.pop(0)
    kv = [(refs.pop(0), refs.pop(0))] if has_lat else []
    kv.append((refs.pop(0), refs.pop(0)))
    if diff:
        lam_ref, sub_ref = refs.pop(0), refs.pop(0)
    o_ref = refs.pop(0)
    tq = q_ref.shape[1]
    lo = lax.broadcasted_iota(jnp.int32, (1, LANE), 1) < HEAD_DIM
    zero = jnp.zeros((), BF16)
    for t in range(q_ref.shape[2] // LANE):
        sl = slice(t * LANE, (t + 1) * LANE)
        qt = q_ref[0, :, sl]
        q2 = jnp.concatenate([jnp.where(lo, qt, zero), jnp.where(lo, zero, qt)], axis=0)
        es, z = _softmax_parts([_dot_nt(q2, k_ref[0, :, sl]) for k_ref, _ in kv])
        acc = jnp.zeros((2 * tq, LANE), F32)
        for e, (_, v_ref) in zip(es, kv):
            acc = acc + _dot(e.astype(BF16), v_ref[0, :, sl])
        acc = acc / z
        if diff:
            o = acc[:tq] - lam_ref[0, 0] * acc[tq:]
            o = o * lax.rsqrt(jnp.mean(o * o, axis=-1, keepdims=True) + EPS) * sub_ref[...] * post_scale
        else:
            o = jnp.where(lo, acc[:tq], acc[tq:])
        o_ref[0, :, sl] = o.astype(o_ref.dtype)


def _attention(qkv_q, qkv_lat, qkv_ctx, d, diff, lam=None, subln=None, post_scale=1.0):
    b, nq, _ = qkv_q.shape
    tq = min(nq, 256)
    has_lat = qkv_lat is not None
    in_specs = [pl.BlockSpec((1, tq, d), lambda i, j: (i, j, 0))]
    args = [qkv_q]
    for src in ([qkv_lat] if has_lat else []) + [qkv_ctx]:
        nk = src.shape[1]
        in_specs += [pl.BlockSpec((1, nk, d), lambda i, j: (i, 0, 1)),
                     pl.BlockSpec((1, nk, d), lambda i, j: (i, 0, 2))]
        args += [src, src]
    if diff:
        in_specs += [pl.BlockSpec(memory_space=pltpu.SMEM), pl.BlockSpec((1, LANE), lambda i, j: (0, 0))]
        args += [lam.reshape(1, 1), subln.reshape(1, LANE)]
    return pl.pallas_call(
        functools.partial(_attn_kernel, diff=diff, has_lat=has_lat, post_scale=post_scale),
        grid=(b, nq // tq), in_specs=in_specs,
        out_specs=pl.BlockSpec((1, tq, d), lambda i, j: (i, j, 0)),
        out_shape=jax.ShapeDtypeStruct((b, nq, d), BF16),
        compiler_params=_cparams(2, 56), name="attention",
    )(*args)


def _na_kernel(cls_ref, q_ref, k_ref, v_ref, kc_ref, vc_ref, bias_ref, o_ref, *, rows, wr, kr):
    del cls_ref
    kstart = jnp.clip(pl.program_id(1) * NA_ROWS - wr // 2, 0, rows - kr)
    start = pl.multiple_of(kstart * GRID_W, GRID_W)
    nk = kr * GRID_W
    tq = q_ref.shape[1]
    lo = lax.broadcasted_iota(jnp.int32, (1, LANE), 1) < HEAD_DIM
    zero = jnp.zeros((), BF16)
    for t in range(q_ref.shape[2] // LANE):
        sl = slice(t * LANE, (t + 1) * LANE)
        qt = q_ref[0, :, sl]
        kw, vw = k_ref[0, pl.ds(start, nk), sl], v_ref[0, pl.ds(start, nk), sl]
        kc, vc = kc_ref[0, :, sl], vc_ref[0, :, sl]
        q2 = jnp.concatenate([jnp.where(lo, qt, zero), jnp.where(lo, zero, qt)], axis=0)
        (el, ec), z = _softmax_parts([_dot_nt(q2, kw) + bias_ref[0, t], _dot_nt(q2, kc)])
        acc = (_dot(el.astype(BF16), vw) + _dot(ec.astype(BF16), vc)) / z
        o_ref[0, :, sl] = jnp.where(lo, acc[:tq], acc[tq:]).astype(o_ref.dtype)


def _na_layout(rows):
    wr = min(WIN_R, rows)
    kr = min(rows, wr + NA_ROWS - 1)
    classes, cls_of_block = [], []
    for blk in range(rows // NA_ROWS):
        kstart = min(max(blk * NA_ROWS - wr // 2, 0), rows - kr)
        key = []
        for qr in range(NA_ROWS):
            r = blk * NA_ROWS + qr
            r0 = min(max(r - wr // 2, 0), rows - wr)
            key.append(tuple((kstart + j - r + WIN_R - 1) if r0 <= kstart + j < r0 + wr else None for j in range(kr)))
        key = tuple(key)
        if key not in classes:
            classes.append(key)
        cls_of_block.append(classes.index(key))
    return wr, kr, classes, cls_of_block


def _na_bias_table(rpb, rows):
    heads = rpb.shape[0]
    _, _, classes, _ = _na_layout(rows)
    cq = jnp.arange(GRID_W)
    c0 = jnp.clip(cq - WIN_C // 2, 0, GRID_W - WIN_C)
    col_in = (cq[None, :] >= c0[:, None]) & (cq[None, :] < c0[:, None] + WIN_C)
    dc = jnp.clip(cq[None, :] - cq[:, None] + WIN_C - 1, 0, 2 * WIN_C - 2)
    dc_onehot = (dc[:, :, None] == jnp.arange(2 * WIN_C - 1)[None, None, :]).astype(F32)
    per_dr = jnp.einsum('hrd,qkd->hrqk', rpb.astype(F32) * math.log2(math.e), dc_onehot,
                        precision=HIGHEST)
    per_dr = jnp.where(col_in[None, None], per_dr, NEG)
    masked = jnp.full((heads, GRID_W, GRID_W), NEG, F32)
    tabs = []
    for key in classes:
        slabs = [jnp.concatenate([masked if dr is None else per_dr[:, dr] for dr in row], axis=-1) for row in key]
        tabs.append(jnp.concatenate(slabs, axis=1))
    tab = jnp.stack(tabs, axis=0)
    return tab.reshape(tab.shape[0], heads // 2, 2 * tab.shape[2], tab.shape[3])


def _na_attention(qkv_l, qkv_c, bias, d):
    b, n, _ = qkv_l.shape
    nc = qkv_c.shape[1]
    rows = n // GRID_W
    wr, kr, _, cls_of_block = _na_layout(rows)
    heads = d // HEAD_DIM
    tq = NA_ROWS * GRID_W
    grid_spec = pltpu.PrefetchScalarGridSpec(
        num_scalar_prefetch=1, grid=(b, rows // NA_ROWS),
        in_specs=[pl.BlockSpec((1, tq, d), lambda i, r, cls: (i, r, 0)),
                  pl.BlockSpec((1, n, d), lambda i, r, cls: (i, 0, 1)),
                  pl.BlockSpec((1, n, d), lambda i, r, cls: (i, 0, 2)),
                  pl.BlockSpec((1, nc, d), lambda i, r, cls: (i, 0, 1)),
                  pl.BlockSpec((1, nc, d), lambda i, r, cls: (i, 0, 2)),
                  pl.BlockSpec((1, heads // 2, 2 * tq, kr * GRID_W), lambda i, r, cls: (cls[r], 0, 0, 0),
                               pipeline_mode=pl.Buffered(1))],
        out_specs=pl.BlockSpec((1, tq, d), lambda i, r, cls: (i, r, 0)))
    return pl.pallas_call(
        functools.partial(_na_kernel, rows=rows, wr=wr, kr=kr),
        grid_spec=grid_spec,
        out_shape=jax.ShapeDtypeStruct((b, n, d), BF16),
        compiler_params=_cparams(2, 56), name="na_attention",
    )(jnp.asarray(cls_of_block, jnp.int32), qkv_l, qkv_l, qkv_l, qkv_c, qkv_c, bias)


def _normmod_kernel(x_ref, mod_ref, g_ref, o_ref):
    o_ref[0] = _normmod(x_ref[0], g_ref[...], mod_ref[0, 0:1, :], mod_ref[0, 1:2, :]).astype(o_ref.dtype)


def _normmod_call(x, mod, g):
    b, n, d = x.shape
    tm = _row_tile(n)
    return pl.pallas_call(
        _normmod_kernel, grid=(b, n // tm),
        in_specs=[pl.BlockSpec((1, tm, d), lambda i, j: (i, j, 0)),
                  pl.BlockSpec((1, 6, d), lambda i, j: (i, 0, 0)),
                  pl.BlockSpec((1, d), lambda i, j: (0, 0))],
        out_specs=pl.BlockSpec((1, tm, d), lambda i, j: (i, j, 0)),
        out_shape=jax.ShapeDtypeStruct((b, n, d), BF16),
        compiler_params=_cparams(2), name="normmod",
    )(x, mod, g.reshape(1, d))


def _s5_tables(a_re, a_im, log_dt, b_re, b_im, c_re, c_im):
    t_len = S5_CHUNK
    a_re, a_im = a_re.astype(F32), a_im.astype(F32)
    dt = jnp.exp(log_dt.astype(F32))[..., None]
    lr, li = a_re * dt, a_im * dt
    cos_li, sin_li = jnp.cos(li), jnp.sin(li)
    ab_im = jnp.exp(lr) * sin_li
    nr = jnp.expm1(lr) * cos_li - 2.0 * jnp.sin(0.5 * li) ** 2
    den = a_re * a_re + a_im * a_im
    fr = (nr * a_re + ab_im * a_im) / den
    fi = (ab_im * a_re - nr * a_im) / den
    b_re, b_im = b_re.astype(F32), b_im.astype(F32)
    bb_re = fr[..., None] * b_re - fi[..., None] * b_im
    bb_im = fr[..., None] * b_im + fi[..., None] * b_re
    tau = jnp.arange(t_len + 1, dtype=F32)[:, None, None, None]
    mag = jnp.exp(tau * lr)
    pr, pi = mag * jnp.cos(tau * li), mag * jnp.sin(tau * li)
    c_re, c_im = c_re.astype(F32), c_im.astype(F32)
    cw_re = c_re[None] * pr[:, :, :, None, :] - c_im[None] * pi[:, :, :, None, :]
    cw_im = c_re[None] * pi[:, :, :, None, :] + c_im[None] * pr[:, :, :, None, :]
    kern = (jnp.einsum('tdgkp,dgph->tdgkh', cw_re, bb_re, precision=HIGHEST)
            - jnp.einsum('tdgkp,dgph->tdgkh', cw_im, bb_im, precision=HIGHEST))
    g = a_re.shape[1]
    h = S5_GROUP
    diff = jnp.arange(t_len)[None, :] - jnp.arange(t_len)[:, None]
    lags = jnp.arange(t_len)[None, None, :]
    fwd_lag = (diff[:, :, None] == lags).astype(F32)
    bwd_lag = (-diff[:, :, None] == lags).astype(F32)
    k_f = jnp.einsum('sta,agkh->stgkh', fwd_lag, kern[:t_len, 0], precision=HIGHEST)
    k_b = jnp.einsum('sta,agkh->stgkh', bwd_lag, kern[:t_len, 1], precision=HIGHEST)
    m_both = (k_f + k_b).transpose(2, 0, 4, 1, 3).reshape(g, t_len * h, t_len * h)

    def in_map(d, p_re, p_im):
        wr_ = p_re[..., None] * bb_re[d][None] - p_im[..., None] * bb_im[d][None]
        wi_ = p_re[..., None] * bb_im[d][None] + p_im[..., None] * bb_re[d][None]
        to_rows = lambda w: w.transpose(1, 0, 3, 2).reshape(g, t_len * h, -1)
        return to_rows(wr_), to_rows(wi_)

    f_re, f_im = in_map(0, pr[:t_len, 0][::-1], pi[:t_len, 0][::-1])
    r_re, r_im = in_map(1, pr[:t_len, 1], pi[:t_len, 1])
    w_in = jnp.concatenate([f_re, f_im, f_im, f_re, r_re, r_im, r_im, r_re], axis=-1)

    def out_map(w_re, w_im):
        to_cols = lambda w: w.transpose(1, 3, 0, 2).reshape(g, -1, t_len * h)
        return jnp.concatenate([to_cols(w_re), -to_cols(w_im)], axis=1)

    w_out = jnp.concatenate([out_map(cw_re[1:, 0], cw_im[1:, 0]),
                             out_map(cw_re[1:, 1][::-1], cw_im[1:, 1][::-1])], axis=1)
    a_pow = jnp.stack([jnp.concatenate([pr[t_len, 0], pr[t_len, 0]], -1),
                       jnp.concatenate([-pi[t_len, 0], pi[t_len, 0]], -1),
                       jnp.concatenate([pr[t_len, 1], pr[t_len, 1]], -1),
                       jnp.concatenate([-pi[t_len, 1], pi[t_len, 1]], -1)], axis=1)
    return m_both.astype(BF16), w_in.astype(BF16), w_out.astype(BF16), a_pow


def _s5_core_kernel(uc_ref, ul_ref, m_ref, win_ref, wout_ref, a_ref, yc_ref, yl_ref, sall_ref, sin_ref,
                    *, bn, nc_c, nc_l):
    rc = nc_c * bn
    sin_ref[0:rc, :] = _dot(uc_ref[0], win_ref[0])
    sin_ref[rc:, :] = _dot(ul_ref[0], win_ref[0])
    a = a_ref[0]
    a1f, a2f, a1b, a2b = a[0:1], a[1:2], a[2:3], a[3:4]
    w = LANE
    zero = jnp.zeros((bn, w), F32)

    def rows_of(c):
        return pl.ds(pl.multiple_of(c * bn, bn), bn)

    def fwd(c, carry):
        s, sw = carry
        r = rows_of(c)
        sall_ref[r, 0:w] = s
        return (a1f * s + a2f * sw + sin_ref[r, 0:w], a1f * sw - a2f * s + sin_ref[r, w:2 * w])

    def bwd(c, carry):
        s, sw = carry
        r = rows_of(c)
        sall_ref[r, w:2 * w] = s
        return (a1b * s + a2b * sw + sin_ref[r, 2 * w:3 * w], a1b * sw - a2b * s + sin_ref[r, 3 * w:4 * w])

    lax.fori_loop(0, nc_c + nc_l, fwd, (zero, zero))
    carry = lax.fori_loop(0, nc_c, lambda i, cr: bwd(nc_c - 1 - i, cr), (zero, zero))
    lax.fori_loop(0, nc_l, lambda i, cr: bwd(nc_c + nc_l - 1 - i, cr), carry)
    yc_ref[0] = (_dot(uc_ref[0], m_ref[0]) + _dot(sall_ref[0:rc, :].astype(BF16), wout_ref[0])).astype(yc_ref.dtype)
    yl_ref[0] = (_dot(ul_ref[0], m_ref[0]) + _dot(sall_ref[rc:, :].astype(BF16), wout_ref[0])).astype(yl_ref.dtype)


def _s5_core(uc_rows, ul_rows, tables, bn):
    g, rc, width = uc_rows.shape
    rl = ul_rows.shape[1]
    m_both, w_in, w_out, a_pow = tables
    rows = lambda r: pl.BlockSpec((1, r, width), lambda i: (i, 0, 0))
    return pl.pallas_call(
        functools.partial(_s5_core_kernel, bn=bn, nc_c=rc // bn, nc_l=rl // bn), grid=(g,),
        in_specs=[rows(rc), rows(rl),
                  pl.BlockSpec((1, width, width), lambda i: (i, 0, 0)),
                  pl.BlockSpec((1, width, 4 * LANE), lambda i: (i, 0, 0)),
                  pl.BlockSpec((1, 2 * LANE, width), lambda i: (i, 0, 0)),
                  pl.BlockSpec((1, 4, LANE), lambda i: (i, 0, 0))],
        out_specs=[rows(rc), rows(rl)],
        out_shape=[jax.ShapeDtypeStruct((g, rc, width), BF16), jax.ShapeDtypeStruct((g, rl, width), BF16)],
        scratch_shapes=[pltpu.VMEM((rc + rl, 2 * LANE), F32), pltpu.VMEM((rc + rl, 4 * LANE), F32)],
        compiler_params=_cparams(1, 56), name="s5_core",
    )(uc_rows, ul_rows, m_both, w_in, w_out, a_pow)


def _s5_glu_kernel(x_ref, y_ref, mod_ref, g_ref, d_ref, w_ref, out_ref):
    x = x_ref[0]
    d = x.shape[1]
    u = _normmod(x, g_ref[...], mod_ref[0, 0:1, :], mod_ref[0, 1:2, :])
    z = jax.nn.gelu(y_ref[0] + d_ref[...] * u)
    zz = _dot(z.astype(BF16), w_ref[...])
    out_ref[0] = x + mod_ref[0, 2:3, :] * (zz[:, :d] * jax.nn.sigmoid(zz[:, d:]))


def _s5_glu(x, y, mod, g, dskip, w_glu):
    b, n, d = x.shape
    tm = _row_tile(n)
    return pl.pallas_call(
        _s5_glu_kernel, grid=(b, n // tm),
        in_specs=[pl.BlockSpec((1, tm, d), lambda i, j: (i, j, 0)),
                  pl.BlockSpec((1, tm, d), lambda i, j: (i, j, 0)),
                  pl.BlockSpec((1, 6, d), lambda i, j: (i, 0, 0)),
                  pl.BlockSpec((1, d), lambda i, j: (0, 0)),
                  pl.BlockSpec((1, d), lambda i, j: (0, 0)),
                  pl.BlockSpec((d, 2 * d), lambda i, j: (0, 0))],
        out_specs=pl.BlockSpec((1, tm, d), lambda i, j: (i, j, 0)),
        out_shape=jax.ShapeDtypeStruct((b, n, d), F32),
        compiler_params=_cparams(2), name="s5_glu",
    )(x, y, mod, g.reshape(1, d), dskip.reshape(1, d), w_glu)


def _s5_layer(x, h_ctx, mod_l, mod_c, g, tables, dskip, w_glu):
    b, n, d = x.shape
    n_c = h_ctx.shape[1]
    t_len, h = S5_CHUNK, S5_GROUP
    groups = d // h
    nc_l, nc_c = n // t_len, n_c // t_len

    def to_rows(u, nc):
        return u.reshape(b, nc, t_len, groups, h).transpose(3, 1, 0, 2, 4).reshape(groups, nc * b, t_len * h)

    def from_rows(y, nc):
        return y.reshape(groups, nc, b, t_len, h).transpose(2, 1, 3, 0, 4).reshape(b, nc * t_len, d)

    yc_rows, yl_rows = _s5_core(to_rows(_normmod_call(h_ctx, mod_c, g), nc_c),
                                to_rows(_normmod_call(x, mod_l, g), nc_l), tables, b)
    y_c, y_l = from_rows(yc_rows, nc_c), from_rows(yl_rows, nc_l)
    return (_s5_glu(h_ctx, y_c, mod_c, g, dskip, w_glu), _s5_glu(x, y_l, mod_l, g, dskip, w_glu))


def _hg_kernel(ql_ref, ffl_ref, fbl_ref, il_ref, ffc_ref, fbc_ref, ic_ref, lb_ref, wf_ref, wb_ref, mask_ref,
               o_ref, ob_ref, sf_ref, sb_ref, *, nc_c, nc_l):
    cn = HG_CHUNK
    levels = cn.bit_length() - 1
    small = min(levels, HG_SMALL_LEVELS)
    nh = HG_HEADS_PER_STEP
    lb = lb_ref[...]
    row = lax.broadcasted_iota(jnp.int32, (cn, 1), 0)

    def gates(ff):
        f = lb + (1.0 - lb) * jax.nn.sigmoid(ff)
        return jnp.log(f), 1.0 - f

    def dot3(w01, x):
        hi = x.astype(BF16)
        rest = x - hi.astype(F32)
        mid = rest.astype(BF16)
        low = (rest - mid.astype(F32)).astype(BF16)
        return _dot(w01, hi) + _dot(w01, mid) + _dot(w01, low)

    def visit(rows, q_ref, ff_ref, i_ref, w_ref, s_ref, out_ref, d_idx):
        fwd = d_idx == 0
        lf, k = gates(ff_ref[0, rows, :])
        v = _silu(i_ref[0, rows, :])
        n_blocks = 1 if out_ref is None else 1 + small
        sums = dot3(w_ref[0:n_blocks * cn, :], lf)
        cum = sums[0:cn]
        total = cum[cn - 1:cn] if fwd else cum[0:1]
        kd = (k * jnp.exp(total - cum)).astype(BF16)
        decay = jnp.exp(total)
        vb = v.astype(BF16)
        if out_ref is not None:
            q = q_ref[0, rows, :]
            qd = (q * jnp.exp(cum)).astype(BF16)
            qk = q * k
            scaled = []
            for lv in range(levels):
                bs = cn >> (lv + 1)
                is_query = ((row // bs) % 2 == 1) if fwd else ((row // bs) % 2 == 0)
                if lv < levels - small:
                    pairs = cum.reshape(cn // (2 * bs), 2 * bs, cum.shape[1])
                    a_row = bs - 1 if fwd else bs
                    anchor = jnp.broadcast_to(pairs[:, a_row:a_row + 1, :], pairs.shape).reshape(cum.shape)
                    to_anchor = jnp.where(is_query, cum - anchor, anchor - cum)
                else:
                    blk = 1 + lv - (levels - small)
                    to_anchor = sums[blk * cn:(blk + 1) * cn]
                scaled.append((jnp.where(is_query, q, k) * jnp.exp(to_anchor)).astype(BF16))
            outs = []
            for hh in range(nh):
                sl = slice(hh * LANE, (hh + 1) * LANE)
                att = jnp.zeros((cn, cn), F32)
                for lv in range(levels):
                    x = scaled[lv][:, sl]
                    att = att + jnp.where(mask_ref[d_idx * levels + lv] > 0.0, _dot_nt(x, x), 0.0)
                same_token = jnp.sum(qk[:, sl], axis=-1, keepdims=True) * v[:, sl]
                outs.append(_dot_nt(qd[:, sl], s_ref[hh].astype(BF16)) + _dot(att.astype(BF16), vb[:, sl]) + same_token)
            dst = out_ref.at[0] if len(out_ref.shape) == 3 else out_ref
            dst[rows, :] = jnp.concatenate(outs, axis=-1)
        for hh in range(nh):
            sl = slice(hh * LANE, (hh + 1) * LANE)
            s_ref[hh] = s_ref[hh] * decay[:, sl] + _dot(v[:, sl].T.astype(BF16), kd[:, sl])

    def rows_of(c):
        return pl.ds(pl.multiple_of(c * cn, cn), cn)

    sf_ref[...] = jnp.zeros(sf_ref.shape, F32)
    sb_ref[...] = jnp.zeros(sb_ref.shape, F32)

    def ctx_body(j, carry):
        visit(rows_of(j), None, ffc_ref, ic_ref, wf_ref, sf_ref, None, 0)
        visit(rows_of(nc_c - 1 - j), None, fbc_ref, ic_ref, wb_ref, sb_ref, None, 1)
        return carry

    def lat_body(j, carry):
        visit(rows_of(j), ql_ref, ffl_ref, il_ref, wf_ref, sf_ref, o_ref, 0)
        visit(rows_of(nc_l - 1 - j), ql_ref, fbl_ref, il_ref, wb_ref, sb_ref, ob_ref, 1)
        return carry

    lax.fori_loop(0, nc_c, ctx_body, 0)
    lax.fori_loop(0, nc_l, lat_body, 0)
    o_ref[0] = o_ref[0] + ob_ref[...]


def _hg_operators():
    cn = HG_CHUNK
    levels = cn.bit_length() - 1
    t = np.arange(cn)[:, None]
    r = np.arange(cn)[None, :]
    small = min(levels, HG_SMALL_LEVELS)
    w = np.zeros((2, 1 + small, cn, cn), np.float32)
    mask = np.zeros((2, levels, cn, cn), np.float32)
    w[0, 0] = r <= t
    w[1, 0] = r >= t
    for lv in range(levels):
        bs = cn >> (lv + 1)
        parent = t // (2 * bs) * (2 * bs)
        second = (t // bs) % 2 == 1
        blk = 1 + lv - (levels - small)
        if blk >= 1:
            anchor_f = parent + bs - 1
            w[0, blk] = np.where(second, (r > anchor_f) & (r <= t), (r > t) & (r <= anchor_f))
            anchor_b = parent + bs
            w[1, blk] = np.where(second, (r >= anchor_b) & (r < t), (r >= t) & (r < anchor_b))
        same_parent = (t // (2 * bs)) == (r // (2 * bs))
        key_second = (r // bs) % 2 == 1
        mask[0, lv] = same_parent & second & ~key_second
        mask[1, lv] = same_parent & ~second & key_second
    return (jnp.asarray(w.reshape(2, (1 + small) * cn, cn), BF16),
            jnp.asarray(mask.reshape(2 * levels, cn, cn), F32))


def _hg_scan(z_l, z_c, lb, d):
    b, n, _ = z_l.shape
    n_c = z_c.shape[1]
    width = HG_HEADS_PER_STEP * LANE
    steps = d // width
    cn = HG_CHUNK
    w, mask = _hg_operators()
    col = lambda k: (lambda i, h: (i, 0, k * steps + h))
    lat = lambda k: pl.BlockSpec((1, n, width), col(k))
    ctx = lambda k: pl.BlockSpec((1, n_c, width), col(k))
    const = lambda a: pl.BlockSpec(a.shape, lambda i, h: (0,) * a.ndim)
    return pl.pallas_call(
        functools.partial(_hg_kernel, nc_c=n_c // cn, nc_l=n // cn), grid=(b, steps),
        in_specs=[lat(0), lat(1), lat(2), lat(3), ctx(1), ctx(2), ctx(3),
                  pl.BlockSpec((1, width), lambda i, h: (0, h)),
                  const(w[0]), const(w[1]), const(mask)],
        out_specs=pl.BlockSpec((1, n, width), lambda i, h: (i, 0, h)),
        out_shape=jax.ShapeDtypeStruct((b, n, d), F32),
        scratch_shapes=[pltpu.VMEM((n, width), F32),
                        pltpu.VMEM((HG_HEADS_PER_STEP, LANE, LANE), F32),
                        pltpu.VMEM((HG_HEADS_PER_STEP, LANE, LANE), F32)],
        compiler_params=_cparams(2), name="hgrn2_scan",
    )(z_l, z_l, z_l, z_l, z_c, z_c, z_c, lb.reshape(1, d), w[0], w[1], mask)


def _hg_out_kernel(o_ref, z_ref, gn_ref, x_ref, mod_ref, w_ref, out_ref):
    o, gate = o_ref[0], z_ref[0]
    parts = []
    for h in range(o.shape[1] // LANE):
        sl = slice(h * LANE, (h + 1) * LANE)
        t = o[:, sl]
        t = t * lax.rsqrt(jnp.mean(t * t, axis=-1, keepdims=True) + EPS) * gn_ref[...]
        parts.append((t * _silu(gate[:, sl])).astype(BF16))
    y = _dot(jnp.concatenate(parts, axis=-1), w_ref[...])
    out_ref[0] = x_ref[0] + mod_ref[0, 2:3, :] * y


def _hg_out(o, z, gnorm, x, mod, w):
    b, n, d = x.shape
    tm = _row_tile(n)
    return pl.pallas_call(
        _hg_out_kernel, grid=(b, n // tm),
        in_specs=[pl.BlockSpec((1, tm, d), lambda i, j: (i, j, 0)),
                  pl.BlockSpec((1, tm, d), lambda i, j: (i, j, 4)),
                  pl.BlockSpec((1, LANE), lambda i, j: (0, 0)),
                  pl.BlockSpec((1, tm, d), lambda i, j: (i, j, 0)),
                  pl.BlockSpec((1, 6, d), lambda i, j: (i, 0, 0)),
                  pl.BlockSpec((d, d), lambda i, j: (0, 0))],
        out_specs=pl.BlockSpec((1, tm, d), lambda i, j: (i, j, 0)),
        out_shape=jax.ShapeDtypeStruct((b, n, d), F32),
        compiler_params=_cparams(2), name="hgrn2_out",
    )(o, z, gnorm.reshape(1, LANE), x, mod, w)


def _moe_pre_kernel(x_ref, mod_ref, g_ref, wr_ref, h_ref, aff_ref):
    h = _normmod(x_ref[0], g_ref[...], mod_ref[0, 3:4, :], mod_ref[0, 4:5, :])
    h_ref[0] = h.astype(BF16)
    logits = jnp.dot(h, wr_ref[...], precision=HIGHEST, preferred_element_type=F32)
    lane = lax.broadcasted_iota(jnp.int32, logits.shape, 1)
    logits = jnp.where(lane < N_EXPERTS, logits, NEG)
    e = jnp.exp(logits - logits.max(axis=-1, keepdims=True))
    aff_ref[0] = e / e.sum(axis=-1, keepdims=True)


def _moe_pre(x, mod, g, wr_pad):
    b, n, d = x.shape
    tm = _row_tile(n)
    return pl.pallas_call(
        _moe_pre_kernel, grid=(b, n // tm),
        in_specs=[pl.BlockSpec((1, tm, d), lambda i, j: (i, j, 0)),
                  pl.BlockSpec((1, 6, d), lambda i, j: (i, 0, 0)),
                  pl.BlockSpec((1, d), lambda i, j: (0, 0)),
                  pl.BlockSpec((d, LANE), lambda i, j: (0, 0))],
        out_specs=[pl.BlockSpec((1, tm, d), lambda i, j: (i, j, 0)),
                   pl.BlockSpec((1, tm, LANE), lambda i, j: (i, j, 0))],
        out_shape=[jax.ShapeDtypeStruct((b, n, d), BF16), jax.ShapeDtypeStruct((b, n, LANE), F32)],
        compiler_params=_cparams(2), name="moe_pre",
    )(x, mod, g.reshape(1, d), wr_pad)


def _route_kernel(aff_ref, tri_ref, pos_ref, selw_ref, *, cap):
    groups = LANE // N_EXPERTS
    rg = aff_ref.shape[1] // groups
    samples = range(aff_ref.shape[0])
    denses = []
    for s in samples:
        dense = aff_ref[s, 0:rg, :]
        for g in range(1, groups):
            dense = dense + pltpu.roll(aff_ref[s, g * rg:(g + 1) * rg, :], g * N_EXPERTS, 1)
        denses.append(dense)
    all_bits = [lax.bitcast_convert_type(dense, jnp.int32) for dense in denses]
    lane = lax.broadcasted_iota(jnp.int32, (1, LANE), 1)
    tri = tri_ref[...]

    def indicator(mask):
        return jnp.where(mask, jnp.ones((), F32), jnp.zeros((), F32))

    def over_groups(row):
        for shift in (LANE // 2, LANE // 4, LANE // 8):
            row = row + pltpu.roll(row, shift, 1)
        return row

    def before_groups(row):
        out = jnp.zeros_like(row)
        for j in range(1, groups):
            out = out + jnp.where(lane >= j * N_EXPERTS, pltpu.roll(row, j * N_EXPERTS, 1), 0.0)
        return out

    def count(mask):
        return over_groups(jnp.sum(indicator(mask), axis=0, keepdims=True))

    def prefix(mask):
        x = indicator(mask)
        return _dot(tri, x.astype(BF16)) + before_groups(jnp.sum(x, axis=0, keepdims=True))

    def search(i, thrs):
        bit = jnp.left_shift(jnp.int32(1), 30 - i)
        return tuple(jnp.where(count(bits >= (thr | bit)) >= cap, thr | bit, thr) for bits, thr in zip(all_bits, thrs))

    thrs = lax.fori_loop(0, 31, search, tuple(jnp.zeros((1, LANE), jnp.int32) for _ in samples))
    for s, dense, bits, thr in zip(samples, denses, all_bits, thrs):
        above, tie = bits > thr, bits == thr
        need = cap - count(above)
        tie_rank = prefix(tie)
        sel = above | (tie & (tie_rank <= need))
        pos_ref[s] = jnp.where(sel, prefix(above) + jnp.minimum(tie_rank, need) - 1.0, -1.0)
        selw_ref[s] = jnp.where(sel, dense, 0.0)


def _route(aff, cap):
    b, n, _ = aff.shape
    rg = n // (LANE // N_EXPERTS)
    idx = jnp.arange(rg)
    tri = (idx[None, :] <= idx[:, None]).astype(BF16)
    sb = math.gcd(b, ROUTE_SAMPLES)
    out_spec = pl.BlockSpec((sb, rg, LANE), lambda i: (i, 0, 0))
    return pl.pallas_call(
        functools.partial(_route_kernel, cap=cap), grid=(b // sb,),
        in_specs=[pl.BlockSpec((sb, n, LANE), lambda i: (i, 0, 0)), pl.BlockSpec((rg, rg), lambda i: (0, 0))],
        out_specs=[out_spec, out_spec],
        out_shape=[jax.ShapeDtypeStruct((b, rg, LANE), F32)] * 2,
        compiler_params=_cparams(1), name="moe_route",
    )(aff, tri)


def _gather_kernel(pos_ref, h_ref, xs_ref, *, cap):
    h = h_ref[0]
    slot = lax.broadcasted_iota(jnp.int32, (cap, h.shape[0]), 0).astype(F32)
    for e in range(N_EXPERTS):
        onehot = jnp.where(pos_ref[0, e:e + 1, :] == slot, 1.0, 0.0).astype(BF16)
        xs_ref[e, 0] = _dot(onehot, h).astype(BF16)


def _gather(pos_rows, h, cap):
    b, n, d = h.shape
    return pl.pallas_call(
        functools.partial(_gather_kernel, cap=cap), grid=(b,),
        in_specs=[pl.BlockSpec((1, N_EXPERTS, n), lambda i: (i, 0, 0)),
                  pl.BlockSpec((1, n, d), lambda i: (i, 0, 0))],
        out_specs=pl.BlockSpec((N_EXPERTS, 1, cap, d), lambda i: (0, i, 0, 0)),
        out_shape=jax.ShapeDtypeStruct((N_EXPERTS, b, cap, d), BF16),
        compiler_params=_cparams(1), name="moe_gather",
    )(pos_rows, h)


def _ffn_kernel(xs_ref, wg_ref, wu_ref, wd_ref, o_ref):
    xs = xs_ref[0]
    hid = _silu(_dot(xs, wg_ref[0])) * _dot(xs, wu_ref[0])
    o_ref[0] = _dot(hid.astype(BF16), wd_ref[0]).astype(BF16)


def _ffn(xs, wg, wu, wd):
    e, m, d = xs.shape
    ff = wg.shape[2]
    tm = _row_tile(m)
    return pl.pallas_call(
        _ffn_kernel, grid=(e, m // tm),
        in_specs=[pl.BlockSpec((1, tm, d), lambda i, j: (i, j, 0)),
                  pl.BlockSpec((1, d, ff), lambda i, j: (i, 0, 0)),
                  pl.BlockSpec((1, d, ff), lambda i, j: (i, 0, 0)),
                  pl.BlockSpec((1, ff, d), lambda i, j: (i, 0, 0))],
        out_specs=pl.BlockSpec((1, tm, d), lambda i, j: (i, j, 0)),
        out_shape=jax.ShapeDtypeStruct((e, m, d), BF16),
        compiler_params=_cparams(2, 56), name="moe_ffn",
    )(xs, wg, wu, wd)


def _combine_kernel(out_ref, pos_ref, selw_ref, x_ref, mod_ref, *rest, cap, final):
    o_ref = rest[-1]
    x = x_ref[0]
    pos, selw = pos_ref[0], selw_ref[0]
    slot = lax.broadcasted_iota(jnp.int32, (x.shape[0], cap), 1).astype(F32)
    y = jnp.zeros(x.shape, F32)
    for e in range(N_EXPERTS):
        weighted = jnp.where(pos[:, e:e + 1] == slot, selw[:, e:e + 1], 0.0).astype(BF16)
        y = y + _dot(weighted, out_ref[e, 0])
    x = x + mod_ref[0, 5:6, :] * y
    if final:
        x = x * lax.rsqrt(jnp.mean(x * x, axis=-1, keepdims=True) + EPS) * rest[0][...]
    o_ref[0] = x


def _combine(out, pos, selw, x, mod, cap, norm_f):
    b, n, d = x.shape
    tm = _row_tile(n)
    final = norm_f is not None
    in_specs = [pl.BlockSpec((N_EXPERTS, 1, cap, d), lambda i, j: (0, i, 0, 0)),
                pl.BlockSpec((1, tm, N_EXPERTS), lambda i, j: (i, j, 0)),
                pl.BlockSpec((1, tm, N_EXPERTS), lambda i, j: (i, j, 0)),
                pl.BlockSpec((1, tm, d), lambda i, j: (i, j, 0)),
                pl.BlockSpec((1, 6, d), lambda i, j: (i, 0, 0))]
    args = [out, pos, selw, x, mod]
    if final:
        in_specs.append(pl.BlockSpec((1, d), lambda i, j: (0, 0)))
        args.append(norm_f.reshape(1, d))
    return pl.pallas_call(
        functools.partial(_combine_kernel, cap=cap, final=final), grid=(b, n // tm),
        in_specs=in_specs,
        out_specs=pl.BlockSpec((1, tm, d), lambda i, j: (i, j, 0)),
        out_shape=jax.ShapeDtypeStruct((b, n, d), F32),
        compiler_params=_cparams(2), name="moe_combine",
    )(*args)


def _moe(x, mod, g, wr_pad, wg, wu, wd, norm_f=None):
    b, n, d = x.shape
    cap = CAPACITY_FACTOR * n // N_EXPERTS
    groups = LANE // N_EXPERTS
    h, aff = _moe_pre(x, mod, g, wr_pad)
    pos, selw = _route(aff, cap)
    split = lambda a: a.reshape(b, n // groups, groups, N_EXPERTS)
    by_token = lambda a: split(a).transpose(0, 2, 1, 3).reshape(b, n, N_EXPERTS)
    by_expert = lambda a: split(a).transpose(0, 3, 2, 1).reshape(b, N_EXPERTS, n)
    xs = _gather(by_expert(pos), h, cap)
    out = _ffn(xs.reshape(N_EXPERTS, b * cap, d), wg, wu, wd).reshape(N_EXPERTS, b, cap, d)
    return _combine(out, by_token(pos), by_token(selw), x, mod, cap, norm_f)


def kernel(x, c, ctx, c_ctx, w_mod, b_mod, norm1, norm2, s5_a_re, s5_a_im, s5_log_dt, s5_b_re, s5_b_im, s5_c_re, s5_c_im, s5_d, s5_w_glu, na_w_qkv, na_w_o, na_rpb, da_w_qkv, da_w_o, da_lambda, da_subln, hg_w_in, hg_w_o, hg_gnorm, hg_lower_bounds, moe_router, moe_w_gate, moe_w_up, moe_w_down, norm_f):
    b, n, d = x.shape
    depth = w_mod.shape[0]
    assert depth == 4, "layer i uses mixer i; only the last layer drops the context stream"

    rows_pad = -(b + 1) % 8
    cc = jnp.concatenate([c, c_ctx[None, :], jnp.zeros((rows_pad, d), F32)], axis=0)
    mod_all = _modulation(cc, w_mod, b_mod)
    p_lb = jax.nn.softmax(hg_lower_bounds.astype(F32), axis=0)
    lbs = jnp.cumsum(p_lb, axis=0) - p_lb[0]
    h_ctx = ctx

    for i in range(depth):
        last = i == depth - 1
        mod_l = mod_all[i, :b].reshape(b, 6, d)
        mod_c = jnp.broadcast_to(mod_all[i, b].reshape(1, 6, d), (b, 6, d))
        g1 = norm1[i]
        if i == 0:
            tables = _s5_tables(s5_a_re[0], s5_a_im[0], s5_log_dt[0], s5_b_re[0], s5_b_im[0], s5_c_re[0], s5_c_im[0])
            h_ctx, x = _s5_layer(x, h_ctx, mod_l, mod_c, g1, tables, s5_d[0], s5_w_glu[0].astype(BF16))
        elif i == 1:
            w_qkv = na_w_qkv[0].astype(BF16)
            qkv_l = _proj_in(x, mod_l, g1, w_qkv, BF16, q_cols=d)
            qkv_c = _proj_in(h_ctx, mod_c, g1, w_qkv, BF16, q_cols=d)
            bias = _na_bias_table(na_rpb[0], n // GRID_W)
            w_o = na_w_o[0].astype(BF16)
            x = _proj_out(_na_attention(qkv_l, qkv_c, bias, d), x, mod_l, w_o)
            h_ctx = _proj_out(_attention(qkv_c, None, qkv_c, d, diff=False), h_ctx, mod_c, w_o)
        elif i == 2:
            w_qkv = da_w_qkv[0].astype(BF16)
            qkv_l = _proj_in(x, mod_l, g1, w_qkv, BF16, rope=_rope_tables(n), rope_cols=2 * d, q_cols=d)
            qkv_c = _proj_in(h_ctx, mod_c, g1, w_qkv, BF16, q_cols=d)
            lam = da_lambda[0].astype(F32)
            lam_init = 0.8 - 0.6 * math.exp(-0.3 * i)
            lam_full = jnp.exp(jnp.sum(lam[0] * lam[1])) - jnp.exp(jnp.sum(lam[2] * lam[3])) + lam_init
            w_o = da_w_o[0].astype(BF16)
            attend = functools.partial(_attention, d=d, diff=True, lam=lam_full, subln=da_subln[0],
                                       post_scale=1.0 - lam_init)
            x = _proj_out(attend(qkv_l, qkv_l, qkv_c), x, mod_l, w_o)
            h_ctx = _proj_out(attend(qkv_c, None, qkv_c), h_ctx, mod_c, w_o)
        else:
            w_in = hg_w_in[0].astype(BF16)
            z_l = _proj_in(x, mod_l, g1, w_in, F32)
            z_c = _proj_in(h_ctx, mod_c, g1, w_in, F32)
            x = _hg_out(_hg_scan(z_l, z_c, lbs[i], d), z_l, hg_gnorm[0], x, mod_l, hg_w_o[0].astype(BF16))

        wr_pad = jnp.pad(moe_router[i].astype(F32), ((0, 0), (0, LANE - N_EXPERTS)))
        wg, wu, wd = moe_w_gate[i].astype(BF16), moe_w_up[i].astype(BF16), moe_w_down[i].astype(BF16)
        x = _moe(x, mod_l, norm2[i], wr_pad, wg, wu, wd, norm_f if last else None)
        if not last:
            h_ctx = _moe(h_ctx, mod_c, norm2[i], wr_pad, wg, wu, wd)
    return x
```

```python
import functools
import math

import jax
import jax.numpy as jnp
import numpy as np
from jax import lax
from jax.experimental import pallas as pl
from jax.experimental.pallas import tpu as pltpu

F32 = jnp.float32
BF16 = jnp.bfloat16
HIGHEST = lax.Precision.HIGHEST
EPS = 1e-6
NEG = -1e30

LANE = 128
GRID_W = 64
WIN_R = 8
WIN_C = 16
NA_ROWS = 4
HEAD_DIM = 64
QUERY_SCALE = HEAD_DIM ** -0.5 * math.log2(math.e)
ROPE_BASE = 10000.0
S5_GROUP = 16
S5_CHUNK = 16
HG_CHUNK = 256
HG_HEADS_PER_STEP = 2
HG_SMALL_LEVELS = 3
N_EXPERTS = 16
CAPACITY_FACTOR = 2
ROUTE_SAMPLES = 4
ROW_TILE = 512


def _cparams(n_axes, vmem_mb=48):
    return pltpu.CompilerParams(dimension_semantics=("arbitrary",) * n_axes,
                                vmem_limit_bytes=vmem_mb * 1024 * 1024)


def _dot(a, b):
    return jnp.dot(a, b, preferred_element_type=F32)


def _dot_nt(a, b):
    return lax.dot_general(a, b, (((1,), (1,)), ((), ())), preferred_element_type=F32)


def _normmod(x, g, shift, scale):
    y = x * lax.rsqrt(jnp.mean(x * x, axis=-1, keepdims=True) + EPS)
    return (y * g) * (1.0 + scale) + shift


def _silu(x):
    return x * jax.nn.sigmoid(x)


def _row_tile(n):
    return min(n, ROW_TILE)


def _mod_kernel(c_ref, w_ref, b_ref, o_ref):
    s = _silu(c_ref[...])
    o_ref[0] = jnp.dot(s, w_ref[0], precision=HIGHEST, preferred_element_type=F32) + b_ref[0]


def _modulation(cc, w_mod, b_mod):
    depth, d, d6 = w_mod.shape
    r = cc.shape[0]
    return pl.pallas_call(
        _mod_kernel, grid=(depth, d6 // d),
        in_specs=[pl.BlockSpec((r, d), lambda i, j: (0, 0)),
                  pl.BlockSpec((1, d, d), lambda i, j: (i, 0, j)),
                  pl.BlockSpec((1, 1, d), lambda i, j: (i, 0, j))],
        out_specs=pl.BlockSpec((1, r, d), lambda i, j: (i, 0, j)),
        out_shape=jax.ShapeDtypeStruct((depth, r, d6), F32),
        compiler_params=_cparams(2), name="modulation",
    )(cc, w_mod, b_mod.reshape(depth, 1, d6))


def _proj_in_kernel(x_ref, mod_ref, g_ref, w_ref, *rest, rope_cols, q_cols):
    o_ref = rest[-1]
    h = _normmod(x_ref[0], g_ref[...], mod_ref[0, 0:1, :], mod_ref[0, 1:2, :])
    y = _dot(h.astype(BF16), w_ref[...])
    if rope_cols:
        cos, s_up, s_dn = rest[0][...], rest[1][...], rest[2][...]
        for j in range(rope_cols // LANE):
            t = y[:, j * LANE:(j + 1) * LANE]
            t = t * cos + pltpu.roll(t, LANE - 16, 1) * s_up + pltpu.roll(t, 16, 1) * s_dn
            if j * LANE < q_cols:
                t = t * QUERY_SCALE
            o_ref[0, :, j * LANE:(j + 1) * LANE] = t.astype(o_ref.dtype)
        o_ref[0, :, rope_cols:] = y[:, rope_cols:].astype(o_ref.dtype)
    elif q_cols:
        o_ref[0, :, :q_cols] = (y[:, :q_cols] * QUERY_SCALE).astype(o_ref.dtype)
        o_ref[0, :, q_cols:] = y[:, q_cols:].astype(o_ref.dtype)
    else:
        o_ref[0] = y.astype(o_ref.dtype)


def _proj_in(x, mod, g, w, out_dtype, rope=None, rope_cols=0, q_cols=0):
    b, n, d = x.shape
    nout = w.shape[1]
    tm = _row_tile(n)
    in_specs = [pl.BlockSpec((1, tm, d), lambda i, j: (i, j, 0)),
                pl.BlockSpec((1, 6, d), lambda i, j: (i, 0, 0)),
                pl.BlockSpec((1, d), lambda i, j: (0, 0)),
                pl.BlockSpec((d, nout), lambda i, j: (0, 0))]
    args = [x, mod, g.reshape(1, d), w]
    if rope_cols:
        in_specs += [pl.BlockSpec((tm, LANE), lambda i, j: (j, 0))] * 3
        args += list(rope)
    return pl.pallas_call(
        functools.partial(_proj_in_kernel, rope_cols=rope_cols, q_cols=q_cols), grid=(b, n // tm),
        in_specs=in_specs,
        out_specs=pl.BlockSpec((1, tm, nout), lambda i, j: (i, j, 0)),
        out_shape=jax.ShapeDtypeStruct((b, n, nout), out_dtype),
        compiler_params=_cparams(2, 56), name="proj_in",
    )(*args)


def _rope_tables(n):
    t = jnp.arange(n)
    row = (t // GRID_W).astype(F32)
    col = (t % GRID_W).astype(F32)
    quarter = HEAD_DIM // 4
    inv = ROPE_BASE ** (-jnp.arange(quarter, dtype=F32) / quarter)
    lane = jnp.arange(LANE)
    pos = jnp.where(((lane % HEAD_DIM) // (HEAD_DIM // 2))[None, :] == 0, row[:, None], col[:, None])
    ang = pos * inv[lane % quarter][None, :]
    first = ((lane % (HEAD_DIM // 2)) < quarter)[None, :]
    cos, sin = jnp.cos(ang), jnp.sin(ang)
    return cos, jnp.where(first, -sin, 0.0), jnp.where(first, 0.0, sin)


def _proj_out_kernel(o_ref, x_ref, mod_ref, w_ref, out_ref):
    out_ref[0] = x_ref[0] + mod_ref[0, 2:3, :] * _dot(o_ref[0], w_ref[...])


def _proj_out(o, x, mod, w):
    b, n, d = x.shape
    tm = _row_tile(n)
    return pl.pallas_call(
        _proj_out_kernel, grid=(b, n // tm),
        in_specs=[pl.BlockSpec((1, tm, d), lambda i, j: (i, j, 0)),
                  pl.BlockSpec((1, tm, d), lambda i, j: (i, j, 0)),
                  pl.BlockSpec((1, 6, d), lambda i, j: (i, 0, 0)),
                  pl.BlockSpec((d, d), lambda i, j: (0, 0))],
        out_specs=pl.BlockSpec((1, tm, d), lambda i, j: (i, j, 0)),
        out_shape=jax.ShapeDtypeStruct((b, n, d), F32),
        compiler_params=_cparams(2), name="proj_out",
    )(o, x, mod, w)


def _softmax_parts(scores):
    m = scores[0].max(axis=-1, keepdims=True)
    for s in scores[1:]:
        m = jnp.maximum(m, s.max(axis=-1, keepdims=True))
    es = [jnp.exp2(s - m) for s in scores]
    z = es[0].sum(axis=-1, keepdims=True)
    for e in es[1:]:
        z = z + e.sum(axis=-1, keepdims=True)
    return es, z


def _attn_kernel(*refs, diff, has_lat, post_scale):
    refs = list(refs)
    q_ref = refs.pop(0)
    kv = [(refs.pop(0), refs.pop(0))] if has_lat else []
    kv.append((refs.pop(0), refs.pop(0)))
    if diff:
        lam_ref, sub_ref = refs.pop(0), refs.pop(0)
    o_ref = refs.pop(0)
    tq = q_ref.shape[1]
    lo = lax.broadcasted_iota(jnp.int32, (1, LANE), 1) < HEAD_DIM
    zero = jnp.zeros((), BF16)
    for t in range(q_ref.shape[2] // LANE):
        sl = slice(t * LANE, (t + 1) * LANE)
        qt = q_ref[0, :, sl]
        q2 = jnp.concatenate([jnp.where(lo, qt, zero), jnp.where(lo, zero, qt)], axis=0)
        es, z = _softmax_parts([_dot_nt(q2, k_ref[0, :, sl]) for k_ref, _ in kv])
        acc = jnp.zeros((2 * tq, LANE), F32)
        for e, (_, v_ref) in zip(es, kv):
            acc = acc + _dot(e.astype(BF16), v_ref[0, :, sl])
        acc = acc / z
        if diff:
            o = acc[:tq] - lam_ref[0, 0] * acc[tq:]
            o = o * lax.rsqrt(jnp.mean(o * o, axis=-1, keepdims=True) + EPS) * sub_ref[...] * post_scale
        else:
            o = jnp.where(lo, acc[:tq], acc[tq:])
        o_ref[0, :, sl] = o.astype(o_ref.dtype)


def _attention(qkv_q, qkv_lat, qkv_ctx, d, diff, lam=None, subln=None, post_scale=1.0):
    b, nq, _ = qkv_q.shape
    tq = min(nq, 256)
    has_lat = qkv_lat is not None
    in_specs = [pl.BlockSpec((1, tq, d), lambda i, j: (i, j, 0))]
    args = [qkv_q]
    for src in ([qkv_lat] if has_lat else []) + [qkv_ctx]:
        nk = src.shape[1]
        in_specs += [pl.BlockSpec((1, nk, d), lambda i, j: (i, 0, 1)),
                     pl.BlockSpec((1, nk, d), lambda i, j: (i, 0, 2))]
        args += [src, src]
    if diff:
        in_specs += [pl.BlockSpec(memory_space=pltpu.SMEM), pl.BlockSpec((1, LANE), lambda i, j: (0, 0))]
        args += [lam.reshape(1, 1), subln.reshape(1, LANE)]
    return pl.pallas_call(
        functools.partial(_attn_kernel, diff=diff, has_lat=has_lat, post_scale=post_scale),
        grid=(b, nq // tq), in_specs=in_specs,
        out_specs=pl.BlockSpec((1, tq, d), lambda i, j: (i, j, 0)),
        out_shape=jax.ShapeDtypeStruct((b, nq, d), BF16),
        compiler_params=_cparams(2, 56), name="attention",
    )(*args)


def _na_kernel(cls_ref, q_ref, k_ref, v_ref, kc_ref, vc_ref, bias_ref, o_ref, *, rows, wr, kr):
    del cls_ref
    kstart = jnp.clip(pl.program_id(1) * NA_ROWS - wr // 2, 0, rows - kr)
    start = pl.multiple_of(kstart * GRID_W, GRID_W)
    nk = kr * GRID_W
    tq = q_ref.shape[1]
    lo = lax.broadcasted_iota(jnp.int32, (1, LANE), 1) < HEAD_DIM
    zero = jnp.zeros((), BF16)
    for t in range(q_ref.shape[2] // LANE):
        sl = slice(t * LANE, (t + 1) * LANE)
        qt = q_ref[0, :, sl]
        kw, vw = k_ref[0, pl.ds(start, nk), sl], v_ref[0, pl.ds(start, nk), sl]
        kc, vc = kc_ref[0, :, sl], vc_ref[0, :, sl]
        q2 = jnp.concatenate([jnp.where(lo, qt, zero), jnp.where(lo, zero, qt)], axis=0)
        (el, ec), z = _softmax_parts([_dot_nt(q2, kw) + bias_ref[0, t], _dot_nt(q2, kc)])
        acc = (_dot(el.astype(BF16), vw) + _dot(ec.astype(BF16), vc)) / z
        o_ref[0, :, sl] = jnp.where(lo, acc[:tq], acc[tq:]).astype(o_ref.dtype)


def _na_layout(rows):
    wr = min(WIN_R, rows)
    kr = min(rows, wr + NA_ROWS - 1)
    classes, cls_of_block = [], []
    for blk in range(rows // NA_ROWS):
        kstart = min(max(blk * NA_ROWS - wr // 2, 0), rows - kr)
        key = []
        for qr in range(NA_ROWS):
            r = blk * NA_ROWS + qr
            r0 = min(max(r - wr // 2, 0), rows - wr)
            key.append(tuple((kstart + j - r + WIN_R - 1) if r0 <= kstart + j < r0 + wr else None for j in range(kr)))
        key = tuple(key)
        if key not in classes:
            classes.append(key)
        cls_of_block.append(classes.index(key))
    return wr, kr, classes, cls_of_block


def _na_bias_table(rpb, rows):
    heads = rpb.shape[0]
    _, _, classes, _ = _na_layout(rows)
    cq = jnp.arange(GRID_W)
    c0 = jnp.clip(cq - WIN_C // 2, 0, GRID_W - WIN_C)
    col_in = (cq[None, :] >= c0[:, None]) & (cq[None, :] < c0[:, None] + WIN_C)
    dc = jnp.clip(cq[None, :] - cq[:, None] + WIN_C - 1, 0, 2 * WIN_C - 2)
    dc_onehot = (dc[:, :, None] == jnp.arange(2 * WIN_C - 1)[None, None, :]).astype(F32)
    per_dr = jnp.einsum('hrd,qkd->hrqk', rpb.astype(F32) * math.log2(math.e), dc_onehot,
                        precision=HIGHEST)
    per_dr = jnp.where(col_in[None, None], per_dr, NEG)
    masked = jnp.full((heads, GRID_W, GRID_W), NEG, F32)
    tabs = []
    for key in classes:
        slabs = [jnp.concatenate([masked if dr is None else per_dr[:, dr] for dr in row], axis=-1) for row in key]
        tabs.append(jnp.concatenate(slabs, axis=1))
    tab = jnp.stack(tabs, axis=0)
    return tab.reshape(tab.shape[0], heads // 2, 2 * tab.shape[2], tab.shape[3])


def _na_attention(qkv_l, qkv_c, bias, d):
    b, n, _ = qkv_l.shape
    nc = qkv_c.shape[1]
    rows = n // GRID_W
    wr, kr, _, cls_of_block = _na_layout(rows)
    heads = d // HEAD_DIM
    tq = NA_ROWS * GRID_W
    grid_spec = pltpu.PrefetchScalarGridSpec(
        num_scalar_prefetch=1, grid=(b, rows // NA_ROWS),
        in_specs=[pl.BlockSpec((1, tq, d), lambda i, r, cls: (i, r, 0)),
                  pl.BlockSpec((1, n, d), lambda i, r, cls: (i, 0, 1)),
                  pl.BlockSpec((1, n, d), lambda i, r, cls: (i, 0, 2)),
                  pl.BlockSpec((1, nc, d), lambda i, r, cls: (i, 0, 1)),
                  pl.BlockSpec((1, nc, d), lambda i, r, cls: (i, 0, 2)),
                  pl.BlockSpec((1, heads // 2, 2 * tq, kr * GRID_W), lambda i, r, cls: (cls[r], 0, 0, 0),
                               pipeline_mode=pl.Buffered(1))],
        out_specs=pl.BlockSpec((1, tq, d), lambda i, r, cls: (i, r, 0)))
    return pl.pallas_call(
        functools.partial(_na_kernel, rows=rows, wr=wr, kr=kr),
        grid_spec=grid_spec,
        out_shape=jax.ShapeDtypeStruct((b, n, d), BF16),
        compiler_params=_cparams(2, 56), name="na_attention",
    )(jnp.asarray(cls_of_block, jnp.int32), qkv_l, qkv_l, qkv_l, qkv_c, qkv_c, bias)


def _normmod_kernel(x_ref, mod_ref, g_ref, o_ref):
    o_ref[0] = _normmod(x_ref[0], g_ref[...], mod_ref[0, 0:1, :], mod_ref[0, 1:2, :]).astype(o_ref.dtype)


def _normmod_call(x, mod, g):
    b, n, d = x.shape
    tm = _row_tile(n)
    return pl.pallas_call(
        _normmod_kernel, grid=(b, n // tm),
        in_specs=[pl.BlockSpec((1, tm, d), lambda i, j: (i, j, 0)),
                  pl.BlockSpec((1, 6, d), lambda i, j: (i, 0, 0)),
                  pl.BlockSpec((1, d), lambda i, j: (0, 0))],
        out_specs=pl.BlockSpec((1, tm, d), lambda i, j: (i, j, 0)),
        out_shape=jax.ShapeDtypeStruct((b, n, d), BF16),
        compiler_params=_cparams(2), name="normmod",
    )(x, mod, g.reshape(1, d))


def _s5_tables(a_re, a_im, log_dt, b_re, b_im, c_re, c_im):
    t_len = S5_CHUNK
    a_re, a_im = a_re.astype(F32), a_im.astype(F32)
    dt = jnp.exp(log_dt.astype(F32))[..., None]
    lr, li = a_re * dt, a_im * dt
    cos_li, sin_li = jnp.cos(li), jnp.sin(li)
    ab_im = jnp.exp(lr) * sin_li
    nr = jnp.expm1(lr) * cos_li - 2.0 * jnp.sin(0.5 * li) ** 2
    den = a_re * a_re + a_im * a_im
    fr = (nr * a_re + ab_im * a_im) / den
    fi = (ab_im * a_re - nr * a_im) / den
    b_re, b_im = b_re.astype(F32), b_im.astype(F32)
    bb_re = fr[..., None] * b_re - fi[..., None] * b_im
    bb_im = fr[..., None] * b_im + fi[..., None] * b_re
    tau = jnp.arange(t_len + 1, dtype=F32)[:, None, None, None]
    mag = jnp.exp(tau * lr)
    pr, pi = mag * jnp.cos(tau * li), mag * jnp.sin(tau * li)
    c_re, c_im = c_re.astype(F32), c_im.astype(F32)
    cw_re = c_re[None] * pr[:, :, :, None, :] - c_im[None] * pi[:, :, :, None, :]
    cw_im = c_re[None] * pi[:, :, :, None, :] + c_im[None] * pr[:, :, :, None, :]
    kern = (jnp.einsum('tdgkp,dgph->tdgkh', cw_re, bb_re, precision=HIGHEST)
            - jnp.einsum('tdgkp,dgph->tdgkh', cw_im, bb_im, precision=HIGHEST))
    g = a_re.shape[1]
    h = S5_GROUP
    diff = jnp.arange(t_len)[None, :] - jnp.arange(t_len)[:, None]
    lags = jnp.arange(t_len)[None, None, :]
    fwd_lag = (diff[:, :, None] == lags).astype(F32)
    bwd_lag = (-diff[:, :, None] == lags).astype(F32)
    k_f = jnp.einsum('sta,agkh->stgkh', fwd_lag, kern[:t_len, 0], precision=HIGHEST)
    k_b = jnp.einsum('sta,agkh->stgkh', bwd_lag, kern[:t_len, 1], precision=HIGHEST)
    m_both = (k_f + k_b).transpose(2, 0, 4, 1, 3).reshape(g, t_len * h, t_len * h)

    def in_map(d, p_re, p_im):
        wr_ = p_re[..., None] * bb_re[d][None] - p_im[..., None] * bb_im[d][None]
        wi_ = p_re[..., None] * bb_im[d][None] + p_im[..., None] * bb_re[d][None]
        to_rows = lambda w: w.transpose(1, 0, 3, 2).reshape(g, t_len * h, -1)
        return to_rows(wr_), to_rows(wi_)

    f_re, f_im = in_map(0, pr[:t_len, 0][::-1], pi[:t_len, 0][::-1])
    r_re, r_im = in_map(1, pr[:t_len, 1], pi[:t_len, 1])
    w_in = jnp.concatenate([f_re, f_im, f_im, f_re, r_re, r_im, r_im, r_re], axis=-1)

    def out_map(w_re, w_im):
        to_cols = lambda w: w.transpose(1, 3, 0, 2).reshape(g, -1, t_len * h)
        return jnp.concatenate([to_cols(w_re), -to_cols(w_im)], axis=1)

    w_out = jnp.concatenate([out_map(cw_re[1:, 0], cw_im[1:, 0]),
                             out_map(cw_re[1:, 1][::-1], cw_im[1:, 1][::-1])], axis=1)
    a_pow = jnp.stack([jnp.concatenate([pr[t_len, 0], pr[t_len, 0]], -1),
                       jnp.concatenate([-pi[t_len, 0], pi[t_len, 0]], -1),
                       jnp.concatenate([pr[t_len, 1], pr[t_len, 1]], -1),
                       jnp.concatenate([-pi[t_len, 1], pi[t_len, 1]], -1)], axis=1)
    return m_both.astype(BF16), w_in.astype(BF16), w_out.astype(BF16), a_pow


def _s5_core_kernel(uc_ref, ul_ref, m_ref, win_ref, wout_ref, a_ref, yc_ref, yl_ref, sall_ref, sin_ref,
                    *, bn, nc_c, nc_l):
    rc = nc_c * bn
    sin_ref[0:rc, :] = _dot(uc_ref[0], win_ref[0])
    sin_ref[rc:, :] = _dot(ul_ref[0], win_ref[0])
    a = a_ref[0]
    a1f, a2f, a1b, a2b = a[0:1], a[1:2], a[2:3], a[3:4]
    w = LANE
    zero = jnp.zeros((bn, w), F32)

    def rows_of(c):
        return pl.ds(pl.multiple_of(c * bn, bn), bn)

    def fwd(c, carry):
        s, sw = carry
        r = rows_of(c)
        sall_ref[r, 0:w] = s
        return (a1f * s + a2f * sw + sin_ref[r, 0:w], a1f * sw - a2f * s + sin_ref[r, w:2 * w])

    def bwd(c, carry):
        s, sw = carry
        r = rows_of(c)
        sall_ref[r, w:2 * w] = s
        return (a1b * s + a2b * sw + sin_ref[r, 2 * w:3 * w], a1b * sw - a2b * s + sin_ref[r, 3 * w:4 * w])

    lax.fori_loop(0, nc_c + nc_l, fwd, (zero, zero))
    carry = lax.fori_loop(0, nc_c, lambda i, cr: bwd(nc_c - 1 - i, cr), (zero, zero))
    lax.fori_loop(0, nc_l, lambda i, cr: bwd(nc_c + nc_l - 1 - i, cr), carry)
    yc_ref[0] = (_dot(uc_ref[0], m_ref[0]) + _dot(sall_ref[0:rc, :].astype(BF16), wout_ref[0])).astype(yc_ref.dtype)
    yl_ref[0] = (_dot(ul_ref[0], m_ref[0]) + _dot(sall_ref[rc:, :].astype(BF16), wout_ref[0])).astype(yl_ref.dtype)


def _s5_core(uc_rows, ul_rows, tables, bn):
    g, rc, width = uc_rows.shape
    rl = ul_rows.shape[1]
    m_both, w_in, w_out, a_pow = tables
    rows = lambda r: pl.BlockSpec((1, r, width), lambda i: (i, 0, 0))
    return pl.pallas_call(
        functools.partial(_s5_core_kernel, bn=bn, nc_c=rc // bn, nc_l=rl // bn), grid=(g,),
        in_specs=[rows(rc), rows(rl),
                  pl.BlockSpec((1, width, width), lambda i: (i, 0, 0)),
                  pl.BlockSpec((1, width, 4 * LANE), lambda i: (i, 0, 0)),
                  pl.BlockSpec((1, 2 * LANE, width), lambda i: (i, 0, 0)),
                  pl.BlockSpec((1, 4, LANE), lambda i: (i, 0, 0))],
        out_specs=[rows(rc), rows(rl)],
        out_shape=[jax.ShapeDtypeStruct((g, rc, width), BF16), jax.ShapeDtypeStruct((g, rl, width), BF16)],
        scratch_shapes=[pltpu.VMEM((rc + rl, 2 * LANE), F32), pltpu.VMEM((rc + rl, 4 * LANE), F32)],
        compiler_params=_cparams(1, 56), name="s5_core",
    )(uc_rows, ul_rows, m_both, w_in, w_out, a_pow)


def _s5_glu_kernel(x_ref, y_ref, mod_ref, g_ref, d_ref, w_ref, out_ref):
    x = x_ref[0]
    d = x.shape[1]
    u = _normmod(x, g_ref[...], mod_ref[0, 0:1, :], mod_ref[0, 1:2, :])
    z = jax.nn.gelu(y_ref[0] + d_ref[...] * u)
    zz = _dot(z.astype(BF16), w_ref[...])
    out_ref[0] = x + mod_ref[0, 2:3, :] * (zz[:, :d] * jax.nn.sigmoid(zz[:, d:]))


def _s5_glu(x, y, mod, g, dskip, w_glu):
    b, n, d = x.shape
    tm = _row_tile(n)
    return pl.pallas_call(
        _s5_glu_kernel, grid=(b, n // tm),
        in_specs=[pl.BlockSpec((1, tm, d), lambda i, j: (i, j, 0)),
                  pl.BlockSpec((1, tm, d), lambda i, j: (i, j, 0)),
                  pl.BlockSpec((1, 6, d), lambda i, j: (i, 0, 0)),
                  pl.BlockSpec((1, d), lambda i, j: (0, 0)),
                  pl.BlockSpec((1, d), lambda i, j: (0, 0)),
                  pl.BlockSpec((d, 2 * d), lambda i, j: (0, 0))],
        out_specs=pl.BlockSpec((1, tm, d), lambda i, j: (i, j, 0)),
        out_shape=jax.ShapeDtypeStruct((b, n, d), F32),
        compiler_params=_cparams(2), name="s5_glu",
    )(x, y, mod, g.reshape(1, d), dskip.reshape(1, d), w_glu)


def _s5_layer(x, h_ctx, mod_l, mod_c, g, tables, dskip, w_glu):
    b, n, d = x.shape
    n_c = h_ctx.shape[1]
    t_len, h = S5_CHUNK, S5_GROUP
    groups = d // h
    nc_l, nc_c = n // t_len, n_c // t_len

    def to_rows(u, nc):
        return u.reshape(b, nc, t_len, groups, h).transpose(3, 1, 0, 2, 4).reshape(groups, nc * b, t_len * h)

    def from_rows(y, nc):
        return y.reshape(groups, nc, b, t_len, h).transpose(2, 1, 3, 0, 4).reshape(b, nc * t_len, d)

    yc_rows, yl_rows = _s5_core(to_rows(_normmod_call(h_ctx, mod_c, g), nc_c),
                                to_rows(_normmod_call(x, mod_l, g), nc_l), tables, b)
    y_c, y_l = from_rows(yc_rows, nc_c), from_rows(yl_rows, nc_l)
    return (_s5_glu(h_ctx, y_c, mod_c, g, dskip, w_glu), _s5_glu(x, y_l, mod_l, g, dskip, w_glu))


def _hg_kernel(ql_ref, ffl_ref, fbl_ref, il_ref, ffc_ref, fbc_ref, ic_ref, lb_ref, wf_ref, wb_ref, mask_ref,
               o_ref, ob_ref, sf_ref, sb_ref, *, nc_c, nc_l):
    cn = HG_CHUNK
    levels = cn.bit_length() - 1
    small = min(levels, HG_SMALL_LEVELS)
    nh = HG_HEADS_PER_STEP
    lb = lb_ref[...]
    row = lax.broadcasted_iota(jnp.int32, (cn, 1), 0)

    def gates(ff):
        f = lb + (1.0 - lb) * jax.nn.sigmoid(ff)
        return jnp.log(f), 1.0 - f

    def dot_hi_lo(w01, x):
        hi = x.astype(BF16)
        low = (x - hi.astype(F32)).astype(BF16)
        return _dot(w01, hi) + _dot(w01, low)

    def visit(rows, q_ref, ff_ref, i_ref, w_ref, s_ref, out_ref, d_idx):
        fwd = d_idx == 0
        lf, k = gates(ff_ref[0, rows, :])
        v = _silu(i_ref[0, rows, :])
        n_blocks = 1 if out_ref is None else 1 + small
        sums = dot_hi_lo(w_ref[0:n_blocks * cn, :], lf)
        cum = sums[0:cn]
        total = cum[cn - 1:cn] if fwd else cum[0:1]
        kd = (k * jnp.exp(total - cum)).astype(BF16)
        decay = jnp.exp(total)
        vb = v.astype(BF16)
        if out_ref is not None:
            q = q_ref[0, rows, :]
            qd = (q * jnp.exp(cum)).astype(BF16)
            qk = q * k
            scaled = []
            for lv in range(levels):
                bs = cn >> (lv + 1)
                is_query = ((row // bs) % 2 == 1) if fwd else ((row // bs) % 2 == 0)
                if lv < levels - small:
                    pairs = cum.reshape(cn // (2 * bs), 2 * bs, cum.shape[1])
                    a_row = bs - 1 if fwd else bs
                    anchor = jnp.broadcast_to(pairs[:, a_row:a_row + 1, :], pairs.shape).reshape(cum.shape)
                    to_anchor = jnp.where(is_query, cum - anchor, anchor - cum)
                else:
                    blk = 1 + lv - (levels - small)
                    to_anchor = sums[blk * cn:(blk + 1) * cn]
                scaled.append((jnp.where(is_query, q, k) * jnp.exp(to_anchor)).astype(BF16))
            outs = []
            for hh in range(nh):
                sl = slice(hh * LANE, (hh + 1) * LANE)
                att = jnp.zeros((cn, cn), BF16)
                for lv in range(levels):
                    x = scaled[lv][:, sl]
                    att = att + _dot_nt(x, x).astype(BF16) * mask_ref[d_idx * levels + lv]
                same_token = jnp.sum(qk[:, sl], axis=-1, keepdims=True) * v[:, sl]
                outs.append(_dot_nt(qd[:, sl], s_ref[hh].astype(BF16)) + _dot(att, vb[:, sl]) + same_token)
            dst = out_ref.at[0] if len(out_ref.shape) == 3 else out_ref
            dst[rows, :] = jnp.concatenate(outs, axis=-1)
        for hh in range(nh):
            sl = slice(hh * LANE, (hh + 1) * LANE)
            s_ref[hh] = s_ref[hh] * decay[:, sl] + _dot(v[:, sl].T.astype(BF16), kd[:, sl])

    def rows_of(c):
        return pl.ds(pl.multiple_of(c * cn, cn), cn)

    sf_ref[...] = jnp.zeros(sf_ref.shape, F32)
    sb_ref[...] = jnp.zeros(sb_ref.shape, F32)

    def ctx_body(j, carry):
        visit(rows_of(j), None, ffc_ref, ic_ref, wf_ref, sf_ref, None, 0)
        visit(rows_of(nc_c - 1 - j), None, fbc_ref, ic_ref, wb_ref, sb_ref, None, 1)
        return carry

    def lat_body(j, carry):
        visit(rows_of(j), ql_ref, ffl_ref, il_ref, wf_ref, sf_ref, o_ref, 0)
        visit(rows_of(nc_l - 1 - j), ql_ref, fbl_ref, il_ref, wb_ref, sb_ref, ob_ref, 1)
        return carry

    lax.fori_loop(0, nc_c, ctx_body, 0)
    lax.fori_loop(0, nc_l, lat_body, 0)
    o_ref[0] = o_ref[0] + ob_ref[...]


def _hg_operators():
    cn = HG_CHUNK
    levels = cn.bit_length() - 1
    t = np.arange(cn)[:, None]
    r = np.arange(cn)[None, :]
    small = min(levels, HG_SMALL_LEVELS)
    w = np.zeros((2, 1 + small, cn, cn), np.float32)
    mask = np.zeros((2, levels, cn, cn), np.float32)
    w[0, 0] = r <= t
    w[1, 0] = r >= t
    for lv in range(levels):
        bs = cn >> (lv + 1)
        parent = t // (2 * bs) * (2 * bs)
        second = (t // bs) % 2 == 1
        blk = 1 + lv - (levels - small)
        if blk >= 1:
            anchor_f = parent + bs - 1
            w[0, blk] = np.where(second, (r > anchor_f) & (r <= t), (r > t) & (r <= anchor_f))
            anchor_b = parent + bs
            w[1, blk] = np.where(second, (r >= anchor_b) & (r < t), (r >= t) & (r < anchor_b))
        same_parent = (t // (2 * bs)) == (r // (2 * bs))
        key_second = (r // bs) % 2 == 1
        mask[0, lv] = same_parent & second & ~key_second
        mask[1, lv] = same_parent & ~second & key_second
    return (jnp.asarray(w.reshape(2, (1 + small) * cn, cn), BF16),
            jnp.asarray(mask.reshape(2 * levels, cn, cn), BF16))


def _hg_scan(z_l, z_c, lb, d):
    b, n, _ = z_l.shape
    n_c = z_c.shape[1]
    width = HG_HEADS_PER_STEP * LANE
    steps = d // width
    cn = HG_CHUNK
    w, mask = _hg_operators()
    col = lambda k: (lambda i, h: (i, 0, k * steps + h))
    lat = lambda k: pl.BlockSpec((1, n, width), col(k))
    ctx = lambda k: pl.BlockSpec((1, n_c, width), col(k))
    const = lambda a: pl.BlockSpec(a.shape, lambda i, h: (0,) * a.ndim)
    return pl.pallas_call(
        functools.partial(_hg_kernel, nc_c=n_c // cn, nc_l=n // cn), grid=(b, steps),
        in_specs=[lat(0), lat(1), lat(2), lat(3), ctx(1), ctx(2), ctx(3),
                  pl.BlockSpec((1, width), lambda i, h: (0, h)),
                  const(w[0]), const(w[1]), const(mask)],
        out_specs=pl.BlockSpec((1, n, width), lambda i, h: (i, 0, h)),
        out_shape=jax.ShapeDtypeStruct((b, n, d), F32),
        scratch_shapes=[pltpu.VMEM((n, width), F32),
                        pltpu.VMEM((HG_HEADS_PER_STEP, LANE, LANE), F32),
                        pltpu.VMEM((HG_HEADS_PER_STEP, LANE, LANE), F32)],
        compiler_params=_cparams(2), name="hgrn2_scan",
    )(z_l, z_l, z_l, z_l, z_c, z_c, z_c, lb.reshape(1, d), w[0], w[1], mask)


def _hg_out_kernel(o_ref, z_ref, gn_ref, x_ref, mod_ref, w_ref, out_ref):
    o, gate = o_ref[0], z_ref[0]
    parts = []
    for h in range(o.shape[1] // LANE):
        sl = slice(h * LANE, (h + 1) * LANE)
        t = o[:, sl]
        t = t * lax.rsqrt(jnp.mean(t * t, axis=-1, keepdims=True) + EPS) * gn_ref[...]
        parts.append((t * _silu(gate[:, sl])).astype(BF16))
    y = _dot(jnp.concatenate(parts, axis=-1), w_ref[...])
    out_ref[0] = x_ref[0] + mod_ref[0, 2:3, :] * y


def _hg_out(o, z, gnorm, x, mod, w):
    b, n, d = x.shape
    tm = _row_tile(n)
    return pl.pallas_call(
        _hg_out_kernel, grid=(b, n // tm),
        in_specs=[pl.BlockSpec((1, tm, d), lambda i, j: (i, j, 0)),
                  pl.BlockSpec((1, tm, d), lambda i, j: (i, j, 4)),
                  pl.BlockSpec((1, LANE), lambda i, j: (0, 0)),
                  pl.BlockSpec((1, tm, d), lambda i, j: (i, j, 0)),
                  pl.BlockSpec((1, 6, d), lambda i, j: (i, 0, 0)),
                  pl.BlockSpec((d, d), lambda i, j: (0, 0))],
        out_specs=pl.BlockSpec((1, tm, d), lambda i, j: (i, j, 0)),
        out_shape=jax.ShapeDtypeStruct((b, n, d), F32),
        compiler_params=_cparams(2), name="hgrn2_out",
    )(o, z, gnorm.reshape(1, LANE), x, mod, w)


def _moe_pre_kernel(x_ref, mod_ref, g_ref, wr_ref, h_ref, aff_ref):
    h = _normmod(x_ref[0], g_ref[...], mod_ref[0, 3:4, :], mod_ref[0, 4:5, :])
    h_hi = h.astype(BF16)
    h_ref[0] = h_hi
    h_lo = (h - h_hi.astype(F32)).astype(BF16)
    logits = _dot(h_hi, wr_ref[0]) + _dot(h_lo, wr_ref[0]) + _dot(h_hi, wr_ref[1])
    lane = lax.broadcasted_iota(jnp.int32, logits.shape, 1)
    logits = jnp.where(lane < N_EXPERTS, logits, NEG)
    e = jnp.exp(logits - logits.max(axis=-1, keepdims=True))
    aff_ref[0] = e / e.sum(axis=-1, keepdims=True)


def _moe_pre(x, mod, g, wr_pad):
    b, n, d = x.shape
    tm = _row_tile(n)
    return pl.pallas_call(
        _moe_pre_kernel, grid=(b, n // tm),
        in_specs=[pl.BlockSpec((1, tm, d), lambda i, j: (i, j, 0)),
                  pl.BlockSpec((1, 6, d), lambda i, j: (i, 0, 0)),
                  pl.BlockSpec((1, d), lambda i, j: (0, 0)),
                  pl.BlockSpec((2, d, LANE), lambda i, j: (0, 0, 0))],
        out_specs=[pl.BlockSpec((1, tm, d), lambda i, j: (i, j, 0)),
                   pl.BlockSpec((1, tm, LANE), lambda i, j: (i, j, 0))],
        out_shape=[jax.ShapeDtypeStruct((b, n, d), BF16), jax.ShapeDtypeStruct((b, n, LANE), F32)],
        compiler_params=_cparams(2), name="moe_pre",
    )(x, mod, g.reshape(1, d), wr_pad)


def _route_kernel(aff_ref, tri_ref, pos_ref, selw_ref, *, cap):
    groups = LANE // N_EXPERTS
    rg = aff_ref.shape[1] // groups
    samples = range(aff_ref.shape[0])
    denses = []
    for s in samples:
        dense = aff_ref[s, 0:rg, :]
        for g in range(1, groups):
            dense = dense + pltpu.roll(aff_ref[s, g * rg:(g + 1) * rg, :], g * N_EXPERTS, 1)
        denses.append(dense)
    all_bits = [lax.bitcast_convert_type(dense, jnp.int32) for dense in denses]
    lane = lax.broadcasted_iota(jnp.int32, (1, LANE), 1)
    tri = tri_ref[...]

    def indicator(mask):
        return jnp.where(mask, jnp.ones((), F32), jnp.zeros((), F32))

    def over_groups(row):
        for shift in (LANE // 2, LANE // 4, LANE // 8):
            row = row + pltpu.roll(row, shift, 1)
        return row

    def before_groups(row):
        out = jnp.zeros_like(row)
        for j in range(1, groups):
            out = out + jnp.where(lane >= j * N_EXPERTS, pltpu.roll(row, j * N_EXPERTS, 1), 0.0)
        return out

    def count(mask):
        return over_groups(jnp.sum(indicator(mask), axis=0, keepdims=True))

    def prefix(mask):
        x = indicator(mask)
        return _dot(tri, x.astype(BF16)) + before_groups(jnp.sum(x, axis=0, keepdims=True))

    def search(i, thrs):
        bit = jnp.left_shift(jnp.int32(1), 30 - i)
        return tuple(jnp.where(count(bits >= (thr | bit)) >= cap, thr | bit, thr) for bits, thr in zip(all_bits, thrs))

    thrs = lax.fori_loop(0, 31, search, tuple(jnp.zeros((1, LANE), jnp.int32) for _ in samples))
    for s, dense, bits, thr in zip(samples, denses, all_bits, thrs):
        above, tie = bits > thr, bits == thr
        need = cap - count(above)
        tie_rank = prefix(tie)
        sel = above | (tie & (tie_rank <= need))
        pos_ref[s] = jnp.where(sel, prefix(above) + jnp.minimum(tie_rank, need) - 1.0, -1.0)
        selw_ref[s] = jnp.where(sel, dense, 0.0)


def _route(aff, cap):
    b, n, _ = aff.shape
    rg = n // (LANE // N_EXPERTS)
    idx = jnp.arange(rg)
    tri = (idx[None, :] <= idx[:, None]).astype(BF16)
    sb = math.gcd(b, ROUTE_SAMPLES)
    out_spec = pl.BlockSpec((sb, rg, LANE), lambda i: (i, 0, 0))
    return pl.pallas_call(
        functools.partial(_route_kernel, cap=cap), grid=(b // sb,),
        in_specs=[pl.BlockSpec((sb, n, LANE), lambda i: (i, 0, 0)), pl.BlockSpec((rg, rg), lambda i: (0, 0))],
        out_specs=[out_spec, out_spec],
        out_shape=[jax.ShapeDtypeStruct((b, rg, LANE), F32)] * 2,
        compiler_params=_cparams(1), name="moe_route",
    )(aff, tri)


def _gather_kernel(pos_ref, h_ref, xs_ref, *, cap):
    h = h_ref[0]
    slot = lax.broadcasted_iota(jnp.int32, (cap, h.shape[0]), 0).astype(F32)
    for e in range(N_EXPERTS):
        onehot = jnp.where(pos_ref[0, e:e + 1, :] == slot, 1.0, 0.0).astype(BF16)
        xs_ref[e, 0] = _dot(onehot, h).astype(BF16)


def _gather(pos_rows, h, cap):
    b, n, d = h.shape
    return pl.pallas_call(
        functools.partial(_gather_kernel, cap=cap), grid=(b,),
        in_specs=[pl.BlockSpec((1, N_EXPERTS, n), lambda i: (i, 0, 0)),
                  pl.BlockSpec((1, n, d), lambda i: (i, 0, 0))],
        out_specs=pl.BlockSpec((N_EXPERTS, 1, cap, d), lambda i: (0, i, 0, 0)),
        out_shape=jax.ShapeDtypeStruct((N_EXPERTS, b, cap, d), BF16),
        compiler_params=_cparams(1), name="moe_gather",
    )(pos_rows, h)


def _ffn_kernel(xs_ref, wg_ref, wu_ref, wd_ref, o_ref):
    xs = xs_ref[0]
    hid = _silu(_dot(xs, wg_ref[0, 0])) * _dot(xs, wu_ref[0, 0])
    o_ref[0] = _dot(hid.astype(BF16), wd_ref[0, 0]).astype(BF16)


def _ffn(xs, wg, wu, wd, layer):
    e, m, d = xs.shape
    ff = wg.shape[3]
    tm = _row_tile(m)
    return pl.pallas_call(
        _ffn_kernel, grid=(e, m // tm),
        in_specs=[pl.BlockSpec((1, tm, d), lambda i, j: (i, j, 0)),
                  pl.BlockSpec((1, 1, d, ff), lambda i, j: (layer, i, 0, 0)),
                  pl.BlockSpec((1, 1, d, ff), lambda i, j: (layer, i, 0, 0)),
                  pl.BlockSpec((1, 1, ff, d), lambda i, j: (layer, i, 0, 0))],
        out_specs=pl.BlockSpec((1, tm, d), lambda i, j: (i, j, 0)),
        out_shape=jax.ShapeDtypeStruct((e, m, d), BF16),
        compiler_params=_cparams(2, 56), name="moe_ffn",
    )(xs, wg, wu, wd)


def _combine_kernel(out_ref, pos_ref, selw_ref, x_ref, mod_ref, *rest, cap, final):
    o_ref = rest[-1]
    x = x_ref[0]
    pos, selw = pos_ref[0], selw_ref[0]
    slot = lax.broadcasted_iota(jnp.int32, (x.shape[0], cap), 1).astype(F32)
    y = jnp.zeros(x.shape, F32)
    for e in range(N_EXPERTS):
        weighted = jnp.where(pos[:, e:e + 1] == slot, selw[:, e:e + 1], 0.0).astype(BF16)
        y = y + _dot(weighted, out_ref[e, 0])
    x = x + mod_ref[0, 5:6, :] * y
    if final:
        x = x * lax.rsqrt(jnp.mean(x * x, axis=-1, keepdims=True) + EPS) * rest[0][...]
    o_ref[0] = x


def _combine(out, pos, selw, x, mod, cap, norm_f):
    b, n, d = x.shape
    tm = _row_tile(n)
    final = norm_f is not None
    in_specs = [pl.BlockSpec((N_EXPERTS, 1, cap, d), lambda i, j: (0, i, 0, 0)),
                pl.BlockSpec((1, tm, N_EXPERTS), lambda i, j: (i, j, 0)),
                pl.BlockSpec((1, tm, N_EXPERTS), lambda i, j: (i, j, 0)),
                pl.BlockSpec((1, tm, d), lambda i, j: (i, j, 0)),
                pl.BlockSpec((1, 6, d), lambda i, j: (i, 0, 0))]
    args = [out, pos, selw, x, mod]
    if final:
        in_specs.append(pl.BlockSpec((1, d), lambda i, j: (0, 0)))
        args.append(norm_f.reshape(1, d))
    return pl.pallas_call(
        functools.partial(_combine_kernel, cap=cap, final=final), grid=(b, n // tm),
        in_specs=in_specs,
        out_specs=pl.BlockSpec((1, tm, d), lambda i, j: (i, j, 0)),
        out_shape=jax.ShapeDtypeStruct((b, n, d), F32),
        compiler_params=_cparams(2), name="moe_combine",
    )(*args)


def _router_parts(w_router):
    w = jnp.pad(w_router.astype(F32), ((0, 0), (0, LANE - N_EXPERTS)))
    hi = w.astype(BF16)
    return jnp.stack([hi, (w - hi.astype(F32)).astype(BF16)], axis=0)


def _moe(x, mod, g, wr_pad, wg, wu, wd, layer, norm_f=None):
    b, n, d = x.shape
    cap = CAPACITY_FACTOR * n // N_EXPERTS
    groups = LANE // N_EXPERTS
    h, aff = _moe_pre(x, mod, g, wr_pad)
    pos, selw = _route(aff, cap)
    split = lambda a: a.reshape(b, n // groups, groups, N_EXPERTS)
    by_token = lambda a: split(a).transpose(0, 2, 1, 3).reshape(b, n, N_EXPERTS)
    by_expert = lambda a: split(a).transpose(0, 3, 2, 1).reshape(b, N_EXPERTS, n)
    xs = _gather(by_expert(pos), h, cap)
    out = _ffn(xs.reshape(N_EXPERTS, b * cap, d), wg, wu, wd, layer).reshape(N_EXPERTS, b, cap, d)
    return _combine(out, by_token(pos), by_token(selw), x, mod, cap, norm_f)


def kernel(x, c, ctx, c_ctx, w_mod, b_mod, norm1, norm2, s5_a_re, s5_a_im, s5_log_dt, s5_b_re, s5_b_im, s5_c_re, s5_c_im, s5_d, s5_w_glu, na_w_qkv, na_w_o, na_rpb, da_w_qkv, da_w_o, da_lambda, da_subln, hg_w_in, hg_w_o, hg_gnorm, hg_lower_bounds, moe_router, moe_w_gate, moe_w_up, moe_w_down, norm_f):
    b, n, d = x.shape
    depth = w_mod.shape[0]
    assert depth == 4, "layer i uses mixer i; only the last layer drops the context stream"

    rows_pad = -(b + 1) % 8
    cc = jnp.concatenate([c, c_ctx[None, :], jnp.zeros((rows_pad, d), F32)], axis=0)
    mod_all = _modulation(cc, w_mod, b_mod)
    p_lb = jax.nn.softmax(hg_lower_bounds.astype(F32), axis=0)
    lbs = jnp.cumsum(p_lb, axis=0) - p_lb[0]
    wg, wu, wd = moe_w_gate.astype(BF16), moe_w_up.astype(BF16), moe_w_down.astype(BF16)
    h_ctx = ctx

    for i in range(depth):
        last = i == depth - 1
        mod_l = mod_all[i, :b].reshape(b, 6, d)
        mod_c = jnp.broadcast_to(mod_all[i, b].reshape(1, 6, d), (b, 6, d))
        g1 = norm1[i]
        if i == 0:
            tables = _s5_tables(s5_a_re[0], s5_a_im[0], s5_log_dt[0], s5_b_re[0], s5_b_im[0], s5_c_re[0], s5_c_im[0])
            h_ctx, x = _s5_layer(x, h_ctx, mod_l, mod_c, g1, tables, s5_d[0], s5_w_glu[0].astype(BF16))
        elif i == 1:
            w_qkv = na_w_qkv[0].astype(BF16)
            qkv_l = _proj_in(x, mod_l, g1, w_qkv, BF16, q_cols=d)
            qkv_c = _proj_in(h_ctx, mod_c, g1, w_qkv, BF16, q_cols=d)
            bias = _na_bias_table(na_rpb[0], n // GRID_W)
            w_o = na_w_o[0].astype(BF16)
            x = _proj_out(_na_attention(qkv_l, qkv_c, bias, d), x, mod_l, w_o)
            h_ctx = _proj_out(_attention(qkv_c, None, qkv_c, d, diff=False), h_ctx, mod_c, w_o)
        elif i == 2:
            w_qkv = da_w_qkv[0].astype(BF16)
            qkv_l = _proj_in(x, mod_l, g1, w_qkv, BF16, rope=_rope_tables(n), rope_cols=2 * d, q_cols=d)
            qkv_c = _proj_in(h_ctx, mod_c, g1, w_qkv, BF16, q_cols=d)
            lam = da_lambda[0].astype(F32)
            lam_init = 0.8 - 0.6 * math.exp(-0.3 * i)
            lam_full = jnp.exp(jnp.sum(lam[0] * lam[1])) - jnp.exp(jnp.sum(lam[2] * lam[3])) + lam_init
            w_o = da_w_o[0].astype(BF16)
            attend = functools.partial(_attention, d=d, diff=True, lam=lam_full, subln=da_subln[0],
                                       post_scale=1.0 - lam_init)
            x = _proj_out(attend(qkv_l, qkv_l, qkv_c), x, mod_l, w_o)
            h_ctx = _proj_out(attend(qkv_c, None, qkv_c), h_ctx, mod_c, w_o)
        else:
            w_in = hg_w_in[0].astype(BF16)
            z_l = _proj_in(x, mod_l, g1, w_in, F32)
            z_c = _proj_in(h_ctx, mod_c, g1, w_in, F32)
            x = _hg_out(_hg_scan(z_l, z_c, lbs[i], d), z_l, hg_gnorm[0], x, mod_l, hg_w_o[0].astype(BF16))

        wr = _router_parts(moe_router[i])
        x = _moe(x, mod_l, norm2[i], wr, wg, wu, wd, i, norm_f if last else None)
        if not last:
            h_ctx = _moe(h_ctx, mod_c, norm2[i], wr, wg, wu, wd, i)
    return x
```

```python
import functools
import math

import jax
import jax.numpy as jnp
import numpy as np
from jax import lax
from jax.experimental import pallas as pl
from jax.experimental.pallas import tpu as pltpu

F32 = jnp.float32
BF16 = jnp.bfloat16
HIGHEST = lax.Precision.HIGHEST
EPS = 1e-6
NEG = -1e30

LANE = 128
GRID_W = 64
WIN_R = 8
WIN_C = 16
NA_ROWS = 4
HEAD_DIM = 64
QUERY_SCALE = HEAD_DIM ** -0.5 * math.log2(math.e)
ROPE_BASE = 10000.0
S5_GROUP = 16
S5_CHUNK = 16
HG_CHUNK = 256
HG_HEADS_PER_STEP = 2
HG_SMALL_LEVELS = 3
N_EXPERTS = 16
CAPACITY_FACTOR = 2
ROUTE_SAMPLES = 4
ROW_TILE = 512


def _cparams(n_axes, vmem_mb=48):
    return pltpu.CompilerParams(dimension_semantics=("arbitrary",) * n_axes,
                                vmem_limit_bytes=vmem_mb * 1024 * 1024)


def _dot(a, b):
    return jnp.dot(a, b, preferred_element_type=F32)


def _dot_nt(a, b):
    return lax.dot_general(a, b, (((1,), (1,)), ((), ())), preferred_element_type=F32)


def _normmod(x, g, shift, scale):
    y = x * lax.rsqrt(jnp.mean(x * x, axis=-1, keepdims=True) + EPS)
    return (y * g) * (1.0 + scale) + shift


def _split_hi_lo(x):
    bits = lax.bitcast_convert_type(x, jnp.int32)
    hi = lax.bitcast_convert_type((bits + jnp.int32(0x8000)) & jnp.int32(-65536), F32)
    return hi.astype(BF16), (x - hi).astype(BF16)


def _silu(x):
    return x * jax.nn.sigmoid(x)


def _row_tile(n):
    return min(n, ROW_TILE)


def _mod_kernel(c_ref, w_ref, b_ref, o_ref):
    s = _silu(c_ref[...])
    o_ref[0] = jnp.dot(s, w_ref[0], precision=HIGHEST, preferred_element_type=F32) + b_ref[0]


def _modulation(cc, w_mod, b_mod):
    depth, d, d6 = w_mod.shape
    r = cc.shape[0]
    return pl.pallas_call(
        _mod_kernel, grid=(depth, d6 // d),
        in_specs=[pl.BlockSpec((r, d), lambda i, j: (0, 0)),
                  pl.BlockSpec((1, d, d), lambda i, j: (i, 0, j)),
                  pl.BlockSpec((1, 1, d), lambda i, j: (i, 0, j))],
        out_specs=pl.BlockSpec((1, r, d), lambda i, j: (i, 0, j)),
        out_shape=jax.ShapeDtypeStruct((depth, r, d6), F32),
        compiler_params=_cparams(2), name="modulation",
    )(cc, w_mod, b_mod.reshape(depth, 1, d6))


def _proj_in_kernel(x_ref, mod_ref, g_ref, w_ref, *rest, rope_cols, q_cols):
    o_ref = rest[-1]
    h = _normmod(x_ref[0], g_ref[...], mod_ref[0, 0:1, :], mod_ref[0, 1:2, :])
    y = _dot(h.astype(BF16), w_ref[...])
    if rope_cols:
        cos, s_up, s_dn = rest[0][...], rest[1][...], rest[2][...]
        for j in range(rope_cols // LANE):
            t = y[:, j * LANE:(j + 1) * LANE]
            t = t * cos + pltpu.roll(t, LANE - 16, 1) * s_up + pltpu.roll(t, 16, 1) * s_dn
            if j * LANE < q_cols:
                t = t * QUERY_SCALE
            o_ref[0, :, j * LANE:(j + 1) * LANE] = t.astype(o_ref.dtype)
        o_ref[0, :, rope_cols:] = y[:, rope_cols:].astype(o_ref.dtype)
    elif q_cols:
        o_ref[0, :, :q_cols] = (y[:, :q_cols] * QUERY_SCALE).astype(o_ref.dtype)
        o_ref[0, :, q_cols:] = y[:, q_cols:].astype(o_ref.dtype)
    else:
        o_ref[0] = y.astype(o_ref.dtype)


def _proj_in(x, mod, g, w, out_dtype, rope=None, rope_cols=0, q_cols=0):
    b, n, d = x.shape
    nout = w.shape[1]
    tm = _row_tile(n)
    in_specs = [pl.BlockSpec((1, tm, d), lambda i, j: (i, j, 0)),
                pl.BlockSpec((1, 6, d), lambda i, j: (i, 0, 0)),
                pl.BlockSpec((1, d), lambda i, j: (0, 0)),
                pl.BlockSpec((d, nout), lambda i, j: (0, 0))]
    args = [x, mod, g.reshape(1, d), w]
    if rope_cols:
        in_specs += [pl.BlockSpec((tm, LANE), lambda i, j: (j, 0))] * 3
        args += list(rope)
    return pl.pallas_call(
        functools.partial(_proj_in_kernel, rope_cols=rope_cols, q_cols=q_cols), grid=(b, n // tm),
        in_specs=in_specs,
        out_specs=pl.BlockSpec((1, tm, nout), lambda i, j: (i, j, 0)),
        out_shape=jax.ShapeDtypeStruct((b, n, nout), out_dtype),
        compiler_params=_cparams(2, 56), name="proj_in",
    )(*args)


def _rope_tables(n):
    t = jnp.arange(n)
    row = (t // GRID_W).astype(F32)
    col = (t % GRID_W).astype(F32)
    quarter = HEAD_DIM // 4
    inv = ROPE_BASE ** (-jnp.arange(quarter, dtype=F32) / quarter)
    lane = jnp.arange(LANE)
    pos = jnp.where(((lane % HEAD_DIM) // (HEAD_DIM // 2))[None, :] == 0, row[:, None], col[:, None])
    ang = pos * inv[lane % quarter][None, :]
    first = ((lane % (HEAD_DIM // 2)) < quarter)[None, :]
    cos, sin = jnp.cos(ang), jnp.sin(ang)
    return cos, jnp.where(first, -sin, 0.0), jnp.where(first, 0.0, sin)


def _proj_out_kernel(o_ref, x_ref, mod_ref, w_ref, out_ref):
    out_ref[0] = x_ref[0] + mod_ref[0, 2:3, :] * _dot(o_ref[0], w_ref[...])


def _proj_out(o, x, mod, w):
    b, n, d = x.shape
    tm = _row_tile(n)
    return pl.pallas_call(
        _proj_out_kernel, grid=(b, n // tm),
        in_specs=[pl.BlockSpec((1, tm, d), lambda i, j: (i, j, 0)),
                  pl.BlockSpec((1, tm, d), lambda i, j: (i, j, 0)),
                  pl.BlockSpec((1, 6, d), lambda i, j: (i, 0, 0)),
                  pl.BlockSpec((d, d), lambda i, j: (0, 0))],
        out_specs=pl.BlockSpec((1, tm, d), lambda i, j: (i, j, 0)),
        out_shape=jax.ShapeDtypeStruct((b, n, d), F32),
        compiler_params=_cparams(2), name="proj_out",
    )(o, x, mod, w)


def _softmax_parts(scores):
    m = scores[0].max(axis=-1, keepdims=True)
    for s in scores[1:]:
        m = jnp.maximum(m, s.max(axis=-1, keepdims=True))
    es = [jnp.exp2(s - m) for s in scores]
    z = es[0].sum(axis=-1, keepdims=True)
    for e in es[1:]:
        z = z + e.sum(axis=-1, keepdims=True)
    return es, z


def _attn_kernel(*refs, diff, has_lat, post_scale):
    refs = list(refs)
    q_ref = refs.pop(0)
    kv = [(refs.pop(0), refs.pop(0))] if has_lat else []
    kv.append((refs.pop(0), refs.pop(0)))
    if diff:
        lam_ref, sub_ref = refs.pop(0), refs.pop(0)
    o_ref = refs.pop(0)
    tq = q_ref.shape[1]
    lo = lax.broadcasted_iota(jnp.int32, (1, LANE), 1) < HEAD_DIM
    zero = jnp.zeros((), BF16)
    for t in range(q_ref.shape[2] // LANE):
        sl = slice(t * LANE, (t + 1) * LANE)
        qt = q_ref[0, :, sl]
        q2 = jnp.concatenate([jnp.where(lo, qt, zero), jnp.where(lo, zero, qt)], axis=0)
        es, z = _softmax_parts([_dot_nt(q2, k_ref[0, :, sl]) for k_ref, _ in kv])
        acc = jnp.zeros((2 * tq, LANE), F32)
        for e, (_, v_ref) in zip(es, kv):
            acc = acc + _dot(e.astype(BF16), v_ref[0, :, sl])
        acc = acc / z
        if diff:
            o = acc[:tq] - lam_ref[0, 0] * acc[tq:]
            o = o * lax.rsqrt(jnp.mean(o * o, axis=-1, keepdims=True) + EPS) * sub_ref[...] * post_scale
        else:
            o = jnp.where(lo, acc[:tq], acc[tq:])
        o_ref[0, :, sl] = o.astype(o_ref.dtype)


def _attention(qkv_q, qkv_lat, qkv_ctx, d, diff, lam=None, subln=None, post_scale=1.0):
    b, nq, _ = qkv_q.shape
    tq = min(nq, 256)
    has_lat = qkv_lat is not None
    in_specs = [pl.BlockSpec((1, tq, d), lambda i, j: (i, j, 0))]
    args = [qkv_q]
    for src in ([qkv_lat] if has_lat else []) + [qkv_ctx]:
        nk = src.shape[1]
        in_specs += [pl.BlockSpec((1, nk, d), lambda i, j: (i, 0, 1)),
                     pl.BlockSpec((1, nk, d), lambda i, j: (i, 0, 2))]
        args += [src, src]
    if diff:
        in_specs += [pl.BlockSpec(memory_space=pltpu.SMEM), pl.BlockSpec((1, LANE), lambda i, j: (0, 0))]
        args += [lam.reshape(1, 1), subln.reshape(1, LANE)]
    return pl.pallas_call(
        functools.partial(_attn_kernel, diff=diff, has_lat=has_lat, post_scale=post_scale),
        grid=(b, nq // tq), in_specs=in_specs,
        out_specs=pl.BlockSpec((1, tq, d), lambda i, j: (i, j, 0)),
        out_shape=jax.ShapeDtypeStruct((b, nq, d), BF16),
        compiler_params=_cparams(2, 56), name="attention",
    )(*args)


def _na_kernel(cls_ref, q_ref, k_ref, v_ref, kc_ref, vc_ref, bias_ref, o_ref, *, rows, wr, kr):
    del cls_ref
    kstart = jnp.clip(pl.program_id(1) * NA_ROWS - wr // 2, 0, rows - kr)
    start = pl.multiple_of(kstart * GRID_W, GRID_W)
    nk = kr * GRID_W
    tq = q_ref.shape[1]
    lo = lax.broadcasted_iota(jnp.int32, (1, LANE), 1) < HEAD_DIM
    zero = jnp.zeros((), BF16)
    for t in range(q_ref.shape[2] // LANE):
        sl = slice(t * LANE, (t + 1) * LANE)
        qt = q_ref[0, :, sl]
        kw, vw = k_ref[0, pl.ds(start, nk), sl], v_ref[0, pl.ds(start, nk), sl]
        kc, vc = kc_ref[0, :, sl], vc_ref[0, :, sl]
        q2 = jnp.concatenate([jnp.where(lo, qt, zero), jnp.where(lo, zero, qt)], axis=0)
        (el, ec), z = _softmax_parts([_dot_nt(q2, kw) + bias_ref[0, t], _dot_nt(q2, kc)])
        acc = (_dot(el.astype(BF16), vw) + _dot(ec.astype(BF16), vc)) / z
        o_ref[0, :, sl] = jnp.where(lo, acc[:tq], acc[tq:]).astype(o_ref.dtype)


def _na_layout(rows):
    wr = min(WIN_R, rows)
    kr = min(rows, wr + NA_ROWS - 1)
    classes, cls_of_block = [], []
    for blk in range(rows // NA_ROWS):
        kstart = min(max(blk * NA_ROWS - wr // 2, 0), rows - kr)
        key = []
        for qr in range(NA_ROWS):
            r = blk * NA_ROWS + qr
            r0 = min(max(r - wr // 2, 0), rows - wr)
            key.append(tuple((kstart + j - r + WIN_R - 1) if r0 <= kstart + j < r0 + wr else None for j in range(kr)))
        key = tuple(key)
        if key not in classes:
            classes.append(key)
        cls_of_block.append(classes.index(key))
    return wr, kr, classes, cls_of_block


def _na_bias_table(rpb, rows):
    heads = rpb.shape[0]
    _, _, classes, _ = _na_layout(rows)
    cq = jnp.arange(GRID_W)
    c0 = jnp.clip(cq - WIN_C // 2, 0, GRID_W - WIN_C)
    col_in = (cq[None, :] >= c0[:, None]) & (cq[None, :] < c0[:, None] + WIN_C)
    dc = jnp.clip(cq[None, :] - cq[:, None] + WIN_C - 1, 0, 2 * WIN_C - 2)
    dc_onehot = (dc[:, :, None] == jnp.arange(2 * WIN_C - 1)[None, None, :]).astype(F32)
    per_dr = jnp.einsum('hrd,qkd->hrqk', rpb.astype(F32) * math.log2(math.e), dc_onehot,
                        precision=HIGHEST)
    per_dr = jnp.where(col_in[None, None], per_dr, NEG)
    masked = jnp.full((heads, GRID_W, GRID_W), NEG, F32)
    tabs = []
    for key in classes:
        slabs = [jnp.concatenate([masked if dr is None else per_dr[:, dr] for dr in row], axis=-1) for row in key]
        tabs.append(jnp.concatenate(slabs, axis=1))
    tab = jnp.stack(tabs, axis=0)
    return tab.reshape(tab.shape[0], heads // 2, 2 * tab.shape[2], tab.shape[3])


def _na_attention(qkv_l, qkv_c, bias, d):
    b, n, _ = qkv_l.shape
    nc = qkv_c.shape[1]
    rows = n // GRID_W
    wr, kr, _, cls_of_block = _na_layout(rows)
    heads = d // HEAD_DIM
    tq = NA_ROWS * GRID_W
    grid_spec = pltpu.PrefetchScalarGridSpec(
        num_scalar_prefetch=1, grid=(b, rows // NA_ROWS),
        in_specs=[pl.BlockSpec((1, tq, d), lambda i, r, cls: (i, r, 0)),
                  pl.BlockSpec((1, n, d), lambda i, r, cls: (i, 0, 1)),
                  pl.BlockSpec((1, n, d), lambda i, r, cls: (i, 0, 2)),
                  pl.BlockSpec((1, nc, d), lambda i, r, cls: (i, 0, 1)),
                  pl.BlockSpec((1, nc, d), lambda i, r, cls: (i, 0, 2)),
                  pl.BlockSpec((1, heads // 2, 2 * tq, kr * GRID_W), lambda i, r, cls: (cls[r], 0, 0, 0),
                               pipeline_mode=pl.Buffered(1))],
        out_specs=pl.BlockSpec((1, tq, d), lambda i, r, cls: (i, r, 0)))
    return pl.pallas_call(
        functools.partial(_na_kernel, rows=rows, wr=wr, kr=kr),
        grid_spec=grid_spec,
        out_shape=jax.ShapeDtypeStruct((b, n, d), BF16),
        compiler_params=_cparams(2, 56), name="na_attention",
    )(jnp.asarray(cls_of_block, jnp.int32), qkv_l, qkv_l, qkv_l, qkv_c, qkv_c, bias)


def _normmod_kernel(x_ref, mod_ref, g_ref, o_ref):
    o_ref[0] = _normmod(x_ref[0], g_ref[...], mod_ref[0, 0:1, :], mod_ref[0, 1:2, :]).astype(o_ref.dtype)


def _normmod_call(x, mod, g):
    b, n, d = x.shape
    tm = _row_tile(n)
    return pl.pallas_call(
        _normmod_kernel, grid=(b, n // tm),
        in_specs=[pl.BlockSpec((1, tm, d), lambda i, j: (i, j, 0)),
                  pl.BlockSpec((1, 6, d), lambda i, j: (i, 0, 0)),
                  pl.BlockSpec((1, d), lambda i, j: (0, 0))],
        out_specs=pl.BlockSpec((1, tm, d), lambda i, j: (i, j, 0)),
        out_shape=jax.ShapeDtypeStruct((b, n, d), BF16),
        compiler_params=_cparams(2), name="normmod",
    )(x, mod, g.reshape(1, d))


def _s5_tables(a_re, a_im, log_dt, b_re, b_im, c_re, c_im):
    t_len = S5_CHUNK
    a_re, a_im = a_re.astype(F32), a_im.astype(F32)
    dt = jnp.exp(log_dt.astype(F32))[..., None]
    lr, li = a_re * dt, a_im * dt
    cos_li, sin_li = jnp.cos(li), jnp.sin(li)
    ab_im = jnp.exp(lr) * sin_li
    nr = jnp.expm1(lr) * cos_li - 2.0 * jnp.sin(0.5 * li) ** 2
    den = a_re * a_re + a_im * a_im
    fr = (nr * a_re + ab_im * a_im) / den
    fi = (ab_im * a_re - nr * a_im) / den
    b_re, b_im = b_re.astype(F32), b_im.astype(F32)
    bb_re = fr[..., None] * b_re - fi[..., None] * b_im
    bb_im = fr[..., None] * b_im + fi[..., None] * b_re
    tau = jnp.arange(t_len + 1, dtype=F32)[:, None, None, None]
    mag = jnp.exp(tau * lr)
    pr, pi = mag * jnp.cos(tau * li), mag * jnp.sin(tau * li)
    c_re, c_im = c_re.astype(F32), c_im.astype(F32)
    cw_re = c_re[None] * pr[:, :, :, None, :] - c_im[None] * pi[:, :, :, None, :]
    cw_im = c_re[None] * pi[:, :, :, None, :] + c_im[None] * pr[:, :, :, None, :]
    kern = (jnp.einsum('tdgkp,dgph->tdgkh', cw_re, bb_re, precision=HIGHEST)
            - jnp.einsum('tdgkp,dgph->tdgkh', cw_im, bb_im, precision=HIGHEST))
    g = a_re.shape[1]
    h = S5_GROUP
    diff = jnp.arange(t_len)[None, :] - jnp.arange(t_len)[:, None]
    lags = jnp.arange(t_len)[None, None, :]
    fwd_lag = (diff[:, :, None] == lags).astype(F32)
    bwd_lag = (-diff[:, :, None] == lags).astype(F32)
    k_f = jnp.einsum('sta,agkh->stgkh', fwd_lag, kern[:t_len, 0], precision=HIGHEST)
    k_b = jnp.einsum('sta,agkh->stgkh', bwd_lag, kern[:t_len, 1], precision=HIGHEST)
    m_both = (k_f + k_b).transpose(2, 0, 4, 1, 3).reshape(g, t_len * h, t_len * h)

    def in_map(d, p_re, p_im):
        wr_ = p_re[..., None] * bb_re[d][None] - p_im[..., None] * bb_im[d][None]
        wi_ = p_re[..., None] * bb_im[d][None] + p_im[..., None] * bb_re[d][None]
        to_rows = lambda w: w.transpose(1, 0, 3, 2).reshape(g, t_len * h, -1)
        return to_rows(wr_), to_rows(wi_)

    f_re, f_im = in_map(0, pr[:t_len, 0][::-1], pi[:t_len, 0][::-1])
    r_re, r_im = in_map(1, pr[:t_len, 1], pi[:t_len, 1])
    w_in = jnp.concatenate([f_re, f_im, f_im, f_re, r_re, r_im, r_im, r_re], axis=-1)

    def out_map(w_re, w_im):
        to_cols = lambda w: w.transpose(1, 3, 0, 2).reshape(g, -1, t_len * h)
        return jnp.concatenate([to_cols(w_re), -to_cols(w_im)], axis=1)

    w_out = jnp.concatenate([out_map(cw_re[1:, 0], cw_im[1:, 0]),
                             out_map(cw_re[1:, 1][::-1], cw_im[1:, 1][::-1])], axis=1)
    a_pow = jnp.stack([jnp.concatenate([pr[t_len, 0], pr[t_len, 0]], -1),
                       jnp.concatenate([-pi[t_len, 0], pi[t_len, 0]], -1),
                       jnp.concatenate([pr[t_len, 1], pr[t_len, 1]], -1),
                       jnp.concatenate([-pi[t_len, 1], pi[t_len, 1]], -1)], axis=1)
    return m_both.astype(BF16), w_in.astype(BF16), w_out.astype(BF16), a_pow


def _s5_core_kernel(uc_ref, ul_ref, m_ref, win_ref, wout_ref, a_ref, yc_ref, yl_ref, sall_ref, sin_ref,
                    *, bn, nc_c, nc_l):
    rc = nc_c * bn
    sin_ref[0:rc, :] = _dot(uc_ref[0], win_ref[0])
    sin_ref[rc:, :] = _dot(ul_ref[0], win_ref[0])
    a = a_ref[0]
    a1f, a2f, a1b, a2b = a[0:1], a[1:2], a[2:3], a[3:4]
    w = LANE
    zero = jnp.zeros((bn, w), F32)

    def rows_of(c):
        return pl.ds(pl.multiple_of(c * bn, bn), bn)

    def fwd(c, carry):
        s, sw = carry
        r = rows_of(c)
        sall_ref[r, 0:w] = s
        return (a1f * s + a2f * sw + sin_ref[r, 0:w], a1f * sw - a2f * s + sin_ref[r, w:2 * w])

    def bwd(c, carry):
        s, sw = carry
        r = rows_of(c)
        sall_ref[r, w:2 * w] = s
        return (a1b * s + a2b * sw + sin_ref[r, 2 * w:3 * w], a1b * sw - a2b * s + sin_ref[r, 3 * w:4 * w])

    lax.fori_loop(0, nc_c + nc_l, fwd, (zero, zero))
    carry = lax.fori_loop(0, nc_c, lambda i, cr: bwd(nc_c - 1 - i, cr), (zero, zero))
    lax.fori_loop(0, nc_l, lambda i, cr: bwd(nc_c + nc_l - 1 - i, cr), carry)
    yc_ref[0] = (_dot(uc_ref[0], m_ref[0]) + _dot(sall_ref[0:rc, :].astype(BF16), wout_ref[0])).astype(yc_ref.dtype)
    yl_ref[0] = (_dot(ul_ref[0], m_ref[0]) + _dot(sall_ref[rc:, :].astype(BF16), wout_ref[0])).astype(yl_ref.dtype)


def _s5_core(uc_rows, ul_rows, tables, bn):
    g, rc, width = uc_rows.shape
    rl = ul_rows.shape[1]
    m_both, w_in, w_out, a_pow = tables
    rows = lambda r: pl.BlockSpec((1, r, width), lambda i: (i, 0, 0))
    return pl.pallas_call(
        functools.partial(_s5_core_kernel, bn=bn, nc_c=rc // bn, nc_l=rl // bn), grid=(g,),
        in_specs=[rows(rc), rows(rl),
                  pl.BlockSpec((1, width, width), lambda i: (i, 0, 0)),
                  pl.BlockSpec((1, width, 4 * LANE), lambda i: (i, 0, 0)),
                  pl.BlockSpec((1, 2 * LANE, width), lambda i: (i, 0, 0)),
                  pl.BlockSpec((1, 4, LANE), lambda i: (i, 0, 0))],
        out_specs=[rows(rc), rows(rl)],
        out_shape=[jax.ShapeDtypeStruct((g, rc, width), BF16), jax.ShapeDtypeStruct((g, rl, width), BF16)],
        scratch_shapes=[pltpu.VMEM((rc + rl, 2 * LANE), F32), pltpu.VMEM((rc + rl, 4 * LANE), F32)],
        compiler_params=_cparams(1, 56), name="s5_core",
    )(uc_rows, ul_rows, m_both, w_in, w_out, a_pow)


def _s5_glu_kernel(x_ref, y_ref, mod_ref, g_ref, d_ref, w_ref, out_ref):
    x = x_ref[0]
    d = x.shape[1]
    u = _normmod(x, g_ref[...], mod_ref[0, 0:1, :], mod_ref[0, 1:2, :])
    z = jax.nn.gelu(y_ref[0] + d_ref[...] * u)
    zz = _dot(z.astype(BF16), w_ref[...])
    out_ref[0] = x + mod_ref[0, 2:3, :] * (zz[:, :d] * jax.nn.sigmoid(zz[:, d:]))


def _s5_glu(x, y, mod, g, dskip, w_glu):
    b, n, d = x.shape
    tm = _row_tile(n)
    return pl.pallas_call(
        _s5_glu_kernel, grid=(b, n // tm),
        in_specs=[pl.BlockSpec((1, tm, d), lambda i, j: (i, j, 0)),
                  pl.BlockSpec((1, tm, d), lambda i, j: (i, j, 0)),
                  pl.BlockSpec((1, 6, d), lambda i, j: (i, 0, 0)),
                  pl.BlockSpec((1, d), lambda i, j: (0, 0)),
                  pl.BlockSpec((1, d), lambda i, j: (0, 0)),
                  pl.BlockSpec((d, 2 * d), lambda i, j: (0, 0))],
        out_specs=pl.BlockSpec((1, tm, d), lambda i, j: (i, j, 0)),
        out_shape=jax.ShapeDtypeStruct((b, n, d), F32),
        compiler_params=_cparams(2), name="s5_glu",
    )(x, y, mod, g.reshape(1, d), dskip.reshape(1, d), w_glu)


def _s5_layer(x, h_ctx, mod_l, mod_c, g, tables, dskip, w_glu):
    b, n, d = x.shape
    n_c = h_ctx.shape[1]
    t_len, h = S5_CHUNK, S5_GROUP
    groups = d // h
    nc_l, nc_c = n // t_len, n_c // t_len

    def to_rows(u, nc):
        return u.reshape(b, nc, t_len, groups, h).transpose(3, 1, 0, 2, 4).reshape(groups, nc * b, t_len * h)

    def from_rows(y, nc):
        return y.reshape(groups, nc, b, t_len, h).transpose(2, 1, 3, 0, 4).reshape(b, nc * t_len, d)

    yc_rows, yl_rows = _s5_core(to_rows(_normmod_call(h_ctx, mod_c, g), nc_c),
                                to_rows(_normmod_call(x, mod_l, g), nc_l), tables, b)
    y_c, y_l = from_rows(yc_rows, nc_c), from_rows(yl_rows, nc_l)
    return (_s5_glu(h_ctx, y_c, mod_c, g, dskip, w_glu), _s5_glu(x, y_l, mod_l, g, dskip, w_glu))


def _hg_kernel(ql_ref, ffl_ref, fbl_ref, il_ref, ffc_ref, fbc_ref, ic_ref, lb_ref, wf_ref, wb_ref, mask_ref,
               o_ref, ob_ref, sf_ref, sb_ref, *, nc_c, nc_l):
    cn = HG_CHUNK
    levels = cn.bit_length() - 1
    small = min(levels, HG_SMALL_LEVELS)
    nh = HG_HEADS_PER_STEP
    lb = lb_ref[...]
    row = lax.broadcasted_iota(jnp.int32, (cn, 1), 0)

    def gates(ff):
        f = lb + (1.0 - lb) * jax.nn.sigmoid(ff)
        return jnp.log(f), 1.0 - f

    def dot_hi_lo(w01, x):
        hi, low = _split_hi_lo(x)
        return _dot(w01, hi) + _dot(w01, low)

    def visit(rows, q_ref, ff_ref, i_ref, w_ref, s_ref, out_ref, d_idx):
        fwd = d_idx == 0
        lf, k = gates(ff_ref[0, rows, :])
        v = _silu(i_ref[0, rows, :])
        n_blocks = 1 if out_ref is None else 1 + small
        sums = dot_hi_lo(w_ref[0:n_blocks * cn, :], lf)
        cum = sums[0:cn]
        total = cum[cn - 1:cn] if fwd else cum[0:1]
        kd = (k * jnp.exp(total - cum)).astype(BF16)
        decay = jnp.exp(total)
        vb = v.astype(BF16)
        if out_ref is not None:
            q = q_ref[0, rows, :]
            qd = (q * jnp.exp(cum)).astype(BF16)
            qk = q * k
            scaled = []
            for lv in range(levels):
                bs = cn >> (lv + 1)
                is_query = ((row // bs) % 2 == 1) if fwd else ((row // bs) % 2 == 0)
                if lv < levels - small:
                    pairs = cum.reshape(cn // (2 * bs), 2 * bs, cum.shape[1])
                    a_row = bs - 1 if fwd else bs
                    anchor = jnp.broadcast_to(pairs[:, a_row:a_row + 1, :], pairs.shape).reshape(cum.shape)
                    to_anchor = jnp.where(is_query, cum - anchor, anchor - cum)
                else:
                    blk = 1 + lv - (levels - small)
                    to_anchor = sums[blk * cn:(blk + 1) * cn]
                scaled.append((jnp.where(is_query, q, k) * jnp.exp(to_anchor)).astype(BF16))
            outs = []
            for hh in range(nh):
                sl = slice(hh * LANE, (hh + 1) * LANE)
                att = jnp.zeros((cn, cn), BF16)
                for lv in range(levels):
                    x = scaled[lv][:, sl]
                    att = att + _dot_nt(x, x).astype(BF16) * mask_ref[d_idx * levels + lv]
                same_token = jnp.sum(qk[:, sl], axis=-1, keepdims=True) * v[:, sl]
                outs.append(_dot_nt(qd[:, sl], s_ref[hh].astype(BF16)) + _dot(att, vb[:, sl]) + same_token)
            dst = out_ref.at[0] if len(out_ref.shape) == 3 else out_ref
            dst[rows, :] = jnp.concatenate(outs, axis=-1)
        for hh in range(nh):
            sl = slice(hh * LANE, (hh + 1) * LANE)
            s_ref[hh] = s_ref[hh] * decay[:, sl] + _dot(v[:, sl].T.astype(BF16), kd[:, sl])

    def rows_of(c):
        return pl.ds(pl.multiple_of(c * cn, cn), cn)

    sf_ref[...] = jnp.zeros(sf_ref.shape, F32)
    sb_ref[...] = jnp.zeros(sb_ref.shape, F32)

    def ctx_body(j, carry):
        visit(rows_of(j), None, ffc_ref, ic_ref, wf_ref, sf_ref, None, 0)
        visit(rows_of(nc_c - 1 - j), None, fbc_ref, ic_ref, wb_ref, sb_ref, None, 1)
        return carry

    def lat_body(j, carry):
        visit(rows_of(j), ql_ref, ffl_ref, il_ref, wf_ref, sf_ref, o_ref, 0)
        visit(rows_of(nc_l - 1 - j), ql_ref, fbl_ref, il_ref, wb_ref, sb_ref, ob_ref, 1)
        return carry

    lax.fori_loop(0, nc_c, ctx_body, 0)
    lax.fori_loop(0, nc_l, lat_body, 0)
    o_ref[0] = o_ref[0] + ob_ref[...]


def _hg_operators():
    cn = HG_CHUNK
    levels = cn.bit_length() - 1
    t = np.arange(cn)[:, None]
    r = np.arange(cn)[None, :]
    small = min(levels, HG_SMALL_LEVELS)
    w = np.zeros((2, 1 + small, cn, cn), np.float32)
    mask = np.zeros((2, levels, cn, cn), np.float32)
    w[0, 0] = r <= t
    w[1, 0] = r >= t
    for lv in range(levels):
        bs = cn >> (lv + 1)
        parent = t // (2 * bs) * (2 * bs)
        second = (t // bs) % 2 == 1
        blk = 1 + lv - (levels - small)
        if blk >= 1:
            anchor_f = parent + bs - 1
            w[0, blk] = np.where(second, (r > anchor_f) & (r <= t), (r > t) & (r <= anchor_f))
            anchor_b = parent + bs
            w[1, blk] = np.where(second, (r >= anchor_b) & (r < t), (r >= t) & (r < anchor_b))
        same_parent = (t // (2 * bs)) == (r // (2 * bs))
        key_second = (r // bs) % 2 == 1
        mask[0, lv] = same_parent & second & ~key_second
        mask[1, lv] = same_parent & ~second & key_second
    return (jnp.asarray(w.reshape(2, (1 + small) * cn, cn), BF16),
            jnp.asarray(mask.reshape(2 * levels, cn, cn), BF16))


def _hg_scan(z_l, z_c, lb, d):
    b, n, _ = z_l.shape
    n_c = z_c.shape[1]
    width = HG_HEADS_PER_STEP * LANE
    steps = d // width
    cn = HG_CHUNK
    w, mask = _hg_operators()
    col = lambda k: (lambda i, h: (i, 0, k * steps + h))
    lat = lambda k: pl.BlockSpec((1, n, width), col(k))
    ctx = lambda k: pl.BlockSpec((1, n_c, width), col(k))
    const = lambda a: pl.BlockSpec(a.shape, lambda i, h: (0,) * a.ndim)
    return pl.pallas_call(
        functools.partial(_hg_kernel, nc_c=n_c // cn, nc_l=n // cn), grid=(b, steps),
        in_specs=[lat(0), lat(1), lat(2), lat(3), ctx(1), ctx(2), ctx(3),
                  pl.BlockSpec((1, width), lambda i, h: (0, h)),
                  const(w[0]), const(w[1]), const(mask)],
        out_specs=pl.BlockSpec((1, n, width), lambda i, h: (i, 0, h)),
        out_shape=jax.ShapeDtypeStruct((b, n, d), F32),
        scratch_shapes=[pltpu.VMEM((n, width), F32),
                        pltpu.VMEM((HG_HEADS_PER_STEP, LANE, LANE), F32),
                        pltpu.VMEM((HG_HEADS_PER_STEP, LANE, LANE), F32)],
        compiler_params=_cparams(2), name="hgrn2_scan",
    )(z_l, z_l, z_l, z_l, z_c, z_c, z_c, lb.reshape(1, d), w[0], w[1], mask)


def _hg_out_kernel(o_ref, z_ref, gn_ref, x_ref, mod_ref, w_ref, out_ref):
    o, gate = o_ref[0], z_ref[0]
    parts = []
    for h in range(o.shape[1] // LANE):
        sl = slice(h * LANE, (h + 1) * LANE)
        t = o[:, sl]
        t = t * lax.rsqrt(jnp.mean(t * t, axis=-1, keepdims=True) + EPS) * gn_ref[...]
        parts.append((t * _silu(gate[:, sl])).astype(BF16))
    y = _dot(jnp.concatenate(parts, axis=-1), w_ref[...])
    out_ref[0] = x_ref[0] + mod_ref[0, 2:3, :] * y


def _hg_out(o, z, gnorm, x, mod, w):
    b, n, d = x.shape
    tm = _row_tile(n)
    return pl.pallas_call(
        _hg_out_kernel, grid=(b, n // tm),
        in_specs=[pl.BlockSpec((1, tm, d), lambda i, j: (i, j, 0)),
                  pl.BlockSpec((1, tm, d), lambda i, j: (i, j, 4)),
                  pl.BlockSpec((1, LANE), lambda i, j: (0, 0)),
                  pl.BlockSpec((1, tm, d), lambda i, j: (i, j, 0)),
                  pl.BlockSpec((1, 6, d), lambda i, j: (i, 0, 0)),
                  pl.BlockSpec((d, d), lambda i, j: (0, 0))],
        out_specs=pl.BlockSpec((1, tm, d), lambda i, j: (i, j, 0)),
        out_shape=jax.ShapeDtypeStruct((b, n, d), F32),
        compiler_params=_cparams(2), name="hgrn2_out",
    )(o, z, gnorm.reshape(1, LANE), x, mod, w)


def _moe_pre_kernel(x_ref, mod_ref, g_ref, wr_ref, h_ref, aff_ref):
    h = _normmod(x_ref[0], g_ref[...], mod_ref[0, 3:4, :], mod_ref[0, 4:5, :])
    h_hi, h_lo = _split_hi_lo(h)
    h_ref[0] = h_hi
    logits = _dot(h_hi, wr_ref[0]) + _dot(h_lo, wr_ref[0]) + _dot(h_hi, wr_ref[1])
    lane = lax.broadcasted_iota(jnp.int32, logits.shape, 1)
    logits = jnp.where(lane < N_EXPERTS, logits, NEG)
    e = jnp.exp(logits - logits.max(axis=-1, keepdims=True))
    aff_ref[0] = e / e.sum(axis=-1, keepdims=True)


def _moe_pre(x, mod, g, wr_pad):
    b, n, d = x.shape
    tm = _row_tile(n)
    return pl.pallas_call(
        _moe_pre_kernel, grid=(b, n // tm),
        in_specs=[pl.BlockSpec((1, tm, d), lambda i, j: (i, j, 0)),
                  pl.BlockSpec((1, 6, d), lambda i, j: (i, 0, 0)),
                  pl.BlockSpec((1, d), lambda i, j: (0, 0)),
                  pl.BlockSpec((2, d, LANE), lambda i, j: (0, 0, 0))],
        out_specs=[pl.BlockSpec((1, tm, d), lambda i, j: (i, j, 0)),
                   pl.BlockSpec((1, tm, LANE), lambda i, j: (i, j, 0))],
        out_shape=[jax.ShapeDtypeStruct((b, n, d), BF16), jax.ShapeDtypeStruct((b, n, LANE), F32)],
        compiler_params=_cparams(2), name="moe_pre",
    )(x, mod, g.reshape(1, d), wr_pad)


def _route_kernel(aff_ref, tri_ref, pos_ref, selw_ref, *, cap):
    groups = LANE // N_EXPERTS
    rg = aff_ref.shape[1] // groups
    samples = range(aff_ref.shape[0])
    denses = []
    for s in samples:
        dense = aff_ref[s, 0:rg, :]
        for g in range(1, groups):
            dense = dense + pltpu.roll(aff_ref[s, g * rg:(g + 1) * rg, :], g * N_EXPERTS, 1)
        denses.append(dense)
    all_bits = [lax.bitcast_convert_type(dense, jnp.int32) for dense in denses]
    lane = lax.broadcasted_iota(jnp.int32, (1, LANE), 1)
    tri = tri_ref[...]

    def indicator(mask):
        return jnp.where(mask, jnp.ones((), F32), jnp.zeros((), F32))

    def over_groups(row):
        for shift in (LANE // 2, LANE // 4, LANE // 8):
            row = row + pltpu.roll(row, shift, 1)
        return row

    def before_groups(row):
        out = jnp.zeros_like(row)
        for j in range(1, groups):
            out = out + jnp.where(lane >= j * N_EXPERTS, pltpu.roll(row, j * N_EXPERTS, 1), 0.0)
        return out

    def count(mask):
        return over_groups(jnp.sum(indicator(mask), axis=0, keepdims=True))

    def prefix(mask):
        x = indicator(mask)
        return _dot(tri, x.astype(BF16)) + before_groups(jnp.sum(x, axis=0, keepdims=True))

    def search(i, thrs):
        bit = jnp.left_shift(jnp.int32(1), 30 - i)
        return tuple(jnp.where(count(bits >= (thr | bit)) >= cap, thr | bit, thr) for bits, thr in zip(all_bits, thrs))

    thrs = lax.fori_loop(0, 31, search, tuple(jnp.zeros((1, LANE), jnp.int32) for _ in samples))
    for s, dense, bits, thr in zip(samples, denses, all_bits, thrs):
        above, tie = bits > thr, bits == thr
        need = cap - count(above)
        tie_rank = prefix(tie)
        sel = above | (tie & (tie_rank <= need))
        pos_ref[s] = jnp.where(sel, prefix(above) + jnp.minimum(tie_rank, need) - 1.0, -1.0)
        selw_ref[s] = jnp.where(sel, dense, 0.0)


def _route(aff, cap):
    b, n, _ = aff.shape
    rg = n // (LANE // N_EXPERTS)
    idx = jnp.arange(rg)
    tri = (idx[None, :] <= idx[:, None]).astype(BF16)
    sb = math.gcd(b, ROUTE_SAMPLES)
    out_spec = pl.BlockSpec((sb, rg, LANE), lambda i: (i, 0, 0))
    return pl.pallas_call(
        functools.partial(_route_kernel, cap=cap), grid=(b // sb,),
        in_specs=[pl.BlockSpec((sb, n, LANE), lambda i: (i, 0, 0)), pl.BlockSpec((rg, rg), lambda i: (0, 0))],
        out_specs=[out_spec, out_spec],
        out_shape=[jax.ShapeDtypeStruct((b, rg, LANE), F32)] * 2,
        compiler_params=_cparams(1), name="moe_route",
    )(aff, tri)


def _gather_kernel(pos_ref, h_ref, xs_ref, *, cap):
    h = h_ref[0]
    slot = lax.broadcasted_iota(jnp.int32, (cap, h.shape[0]), 0).astype(F32)
    for e in range(N_EXPERTS):
        onehot = jnp.where(pos_ref[0, e:e + 1, :] == slot, 1.0, 0.0).astype(BF16)
        xs_ref[e, 0] = _dot(onehot, h).astype(BF16)


def _gather(pos_rows, h, cap):
    b, n, d = h.shape
    return pl.pallas_call(
        functools.partial(_gather_kernel, cap=cap), grid=(b,),
        in_specs=[pl.BlockSpec((1, N_EXPERTS, n), lambda i: (i, 0, 0)),
                  pl.BlockSpec((1, n, d), lambda i: (i, 0, 0))],
        out_specs=pl.BlockSpec((N_EXPERTS, 1, cap, d), lambda i: (0, i, 0, 0)),
        out_shape=jax.ShapeDtypeStruct((N_EXPERTS, b, cap, d), BF16),
        compiler_params=_cparams(1), name="moe_gather",
    )(pos_rows, h)


def _ffn_kernel(xs_ref, wg_ref, wu_ref, wd_ref, o_ref):
    xs = xs_ref[0]
    hid = _silu(_dot(xs, wg_ref[0, 0])) * _dot(xs, wu_ref[0, 0])
    o_ref[0] = _dot(hid.astype(BF16), wd_ref[0, 0]).astype(BF16)


def _ffn(xs, wg, wu, wd, layer):
    e, m, d = xs.shape
    ff = wg.shape[3]
    tm = _row_tile(m)
    return pl.pallas_call(
        _ffn_kernel, grid=(e, m // tm),
        in_specs=[pl.BlockSpec((1, tm, d), lambda i, j: (i, j, 0)),
                  pl.BlockSpec((1, 1, d, ff), lambda i, j: (layer, i, 0, 0)),
                  pl.BlockSpec((1, 1, d, ff), lambda i, j: (layer, i, 0, 0)),
                  pl.BlockSpec((1, 1, ff, d), lambda i, j: (layer, i, 0, 0))],
        out_specs=pl.BlockSpec((1, tm, d), lambda i, j: (i, j, 0)),
        out_shape=jax.ShapeDtypeStruct((e, m, d), BF16),
        compiler_params=_cparams(2, 56), name="moe_ffn",
    )(xs, wg, wu, wd)


def _combine_kernel(out_ref, pos_ref, selw_ref, x_ref, mod_ref, *rest, cap, final):
    o_ref = rest[-1]
    x = x_ref[0]
    pos, selw = pos_ref[0], selw_ref[0]
    slot = lax.broadcasted_iota(jnp.int32, (x.shape[0], cap), 1).astype(F32)
    y = jnp.zeros(x.shape, F32)
    for e in range(N_EXPERTS):
        weighted = jnp.where(pos[:, e:e + 1] == slot, selw[:, e:e + 1], 0.0).astype(BF16)
        y = y + _dot(weighted, out_ref[e, 0])
    x = x + mod_ref[0, 5:6, :] * y
    if final:
        x = x * lax.rsqrt(jnp.mean(x * x, axis=-1, keepdims=True) + EPS) * rest[0][...]
    o_ref[0] = x


def _combine(out, pos, selw, x, mod, cap, norm_f):
    b, n, d = x.shape
    tm = _row_tile(n)
    final = norm_f is not None
    in_specs = [pl.BlockSpec((N_EXPERTS, 1, cap, d), lambda i, j: (0, i, 0, 0)),
                pl.BlockSpec((1, tm, N_EXPERTS), lambda i, j: (i, j, 0)),
                pl.BlockSpec((1, tm, N_EXPERTS), lambda i, j: (i, j, 0)),
                pl.BlockSpec((1, tm, d), lambda i, j: (i, j, 0)),
                pl.BlockSpec((1, 6, d), lambda i, j: (i, 0, 0))]
    args = [out, pos, selw, x, mod]
    if final:
        in_specs.append(pl.BlockSpec((1, d), lambda i, j: (0, 0)))
        args.append(norm_f.reshape(1, d))
    return pl.pallas_call(
        functools.partial(_combine_kernel, cap=cap, final=final), grid=(b, n // tm),
        in_specs=in_specs,
        out_specs=pl.BlockSpec((1, tm, d), lambda i, j: (i, j, 0)),
        out_shape=jax.ShapeDtypeStruct((b, n, d), F32),
        compiler_params=_cparams(2), name="moe_combine",
    )(*args)


def _router_parts(w_router):
    w = jnp.pad(w_router.astype(F32), ((0, 0), (0, LANE - N_EXPERTS)))
    return jnp.stack(_split_hi_lo(w), axis=0)


def _moe(x, mod, g, wr_pad, wg, wu, wd, layer, norm_f=None):
    b, n, d = x.shape
    cap = CAPACITY_FACTOR * n // N_EXPERTS
    groups = LANE // N_EXPERTS
    h, aff = _moe_pre(x, mod, g, wr_pad)
    pos, selw = _route(aff, cap)
    split = lambda a: a.reshape(b, n // groups, groups, N_EXPERTS)
    by_token = lambda a: split(a).transpose(0, 2, 1, 3).reshape(b, n, N_EXPERTS)
    by_expert = lambda a: split(a).transpose(0, 3, 2, 1).reshape(b, N_EXPERTS, n)
    xs = _gather(by_expert(pos), h, cap)
    out = _ffn(xs.reshape(N_EXPERTS, b * cap, d), wg, wu, wd, layer).reshape(N_EXPERTS, b, cap, d)
    return _combine(out, by_token(pos), by_token(selw), x, mod, cap, norm_f)


def kernel(x, c, ctx, c_ctx, w_mod, b_mod, norm1, norm2, s5_a_re, s5_a_im, s5_log_dt, s5_b_re, s5_b_im, s5_c_re, s5_c_im, s5_d, s5_w_glu, na_w_qkv, na_w_o, na_rpb, da_w_qkv, da_w_o, da_lambda, da_subln, hg_w_in, hg_w_o, hg_gnorm, hg_lower_bounds, moe_router, moe_w_gate, moe_w_up, moe_w_down, norm_f):
    b, n, d = x.shape
    depth = w_mod.shape[0]
    assert depth == 4, "layer i uses mixer i; only the last layer drops the context stream"

    rows_pad = -(b + 1) % 8
    cc = jnp.concatenate([c, c_ctx[None, :], jnp.zeros((rows_pad, d), F32)], axis=0)
    mod_all = _modulation(cc, w_mod, b_mod)
    p_lb = jax.nn.softmax(hg_lower_bounds.astype(F32), axis=0)
    lbs = jnp.cumsum(p_lb, axis=0) - p_lb[0]
    wg, wu, wd = moe_w_gate.astype(BF16), moe_w_up.astype(BF16), moe_w_down.astype(BF16)
    h_ctx = ctx

    for i in range(depth):
        last = i == depth - 1
        mod_l = mod_all[i, :b].reshape(b, 6, d)
        mod_c = jnp.broadcast_to(mod_all[i, b].reshape(1, 6, d), (b, 6, d))
        g1 = norm1[i]
        if i == 0:
            tables = _s5_tables(s5_a_re[0], s5_a_im[0], s5_log_dt[0], s5_b_re[0], s5_b_im[0], s5_c_re[0], s5_c_im[0])
            h_ctx, x = _s5_layer(x, h_ctx, mod_l, mod_c, g1, tables, s5_d[0], s5_w_glu[0].astype(BF16))
        elif i == 1:
            w_qkv = na_w_qkv[0].astype(BF16)
            qkv_l = _proj_in(x, mod_l, g1, w_qkv, BF16, q_cols=d)
            qkv_c = _proj_in(h_ctx, mod_c, g1, w_qkv, BF16, q_cols=d)
            bias = _na_bias_table(na_rpb[0], n // GRID_W)
            w_o = na_w_o[0].astype(BF16)
            x = _proj_out(_na_attention(qkv_l, qkv_c, bias, d), x, mod_l, w_o)
            h_ctx = _proj_out(_attention(qkv_c, None, qkv_c, d, diff=False), h_ctx, mod_c, w_o)
        elif i == 2:
            w_qkv = da_w_qkv[0].astype(BF16)
            qkv_l = _proj_in(x, mod_l, g1, w_qkv, BF16, rope=_rope_tables(n), rope_cols=2 * d, q_cols=d)
            qkv_c = _proj_in(h_ctx, mod_c, g1, w_qkv, BF16, q_cols=d)
            lam = da_lambda[0].astype(F32)
            lam_init = 0.8 - 0.6 * math.exp(-0.3 * i)
            lam_full = jnp.exp(jnp.sum(lam[0] * lam[1])) - jnp.exp(jnp.sum(lam[2] * lam[3])) + lam_init
            w_o = da_w_o[0].astype(BF16)
            attend = functools.partial(_attention, d=d, diff=True, lam=lam_full, subln=da_subln[0],
                                       post_scale=1.0 - lam_init)
            x = _proj_out(attend(qkv_l, qkv_l, qkv_c), x, mod_l, w_o)
            h_ctx = _proj_out(attend(qkv_c, None, qkv_c), h_ctx, mod_c, w_o)
        else:
            w_in = hg_w_in[0].astype(BF16)
            z_l = _proj_in(x, mod_l, g1, w_in, F32)
            z_c = _proj_in(h_ctx, mod_c, g1, w_in, F32)
            x = _hg_out(_hg_scan(z_l, z_c, lbs[i], d), z_l, hg_gnorm[0], x, mod_l, hg_w_o[0].astype(BF16))

        wr = _router_parts(moe_router[i])
        x = _moe(x, mod_l, norm2[i], wr, wg, wu, wd, i, norm_f if last else None)
        if not last:
            h_ctx = _moe(h_ctx, mod_c, norm2[i], wr, wg, wu, wd, i)
    return x
```

```python
import functools
import math

import jax
import jax.numpy as jnp
import numpy as np
from jax import lax
from jax.experimental import pallas as pl
from jax.experimental.pallas import tpu as pltpu

F32 = jnp.float32
BF16 = jnp.bfloat16
HIGHEST = lax.Precision.HIGHEST
EPS = 1e-6
NEG = -1e30

LANE = 128
GRID_W = 64
WIN_R = 8
WIN_C = 16
NA_ROWS = 4
HEAD_DIM = 64
QUERY_SCALE = HEAD_DIM ** -0.5 * math.log2(math.e)
ROPE_BASE = 10000.0
S5_GROUP = 16
S5_CHUNK = 16
HG_CHUNK = 256
HG_HEADS_PER_STEP = 2
HG_SMALL_LEVELS = 3
N_EXPERTS = 16
CAPACITY_FACTOR = 2
ROUTE_SAMPLES = 4
ROW_TILE = 512


def _cparams(n_axes, vmem_mb=48):
    return pltpu.CompilerParams(dimension_semantics=("arbitrary",) * n_axes,
                                vmem_limit_bytes=vmem_mb * 1024 * 1024)


def _dot(a, b):
    return jnp.dot(a, b, preferred_element_type=F32)


def _dot_nt(a, b):
    return lax.dot_general(a, b, (((1,), (1,)), ((), ())), preferred_element_type=F32)


def _normmod(x, g, shift, scale):
    y = x * lax.rsqrt(jnp.mean(x * x, axis=-1, keepdims=True) + EPS)
    return (y * g) * (1.0 + scale) + shift


def _split_hi_lo(x):
    bits = lax.bitcast_convert_type(x, jnp.int32)
    hi = lax.bitcast_convert_type((bits + jnp.int32(0x8000)) & jnp.int32(-65536), F32)
    return hi.astype(BF16), (x - hi).astype(BF16)


def _silu(x):
    return x * jax.nn.sigmoid(x)


def _row_tile(n):
    return min(n, ROW_TILE)


def _mod_kernel(c_ref, w_ref, b_ref, o_ref):
    s = _silu(c_ref[...])
    o_ref[0] = jnp.dot(s, w_ref[0], precision=HIGHEST, preferred_element_type=F32) + b_ref[0]


def _modulation(cc, w_mod, b_mod):
    depth, d, d6 = w_mod.shape
    r = cc.shape[0]
    return pl.pallas_call(
        _mod_kernel, grid=(depth, d6 // d),
        in_specs=[pl.BlockSpec((r, d), lambda i, j: (0, 0)),
                  pl.BlockSpec((1, d, d), lambda i, j: (i, 0, j)),
                  pl.BlockSpec((1, 1, d), lambda i, j: (i, 0, j))],
        out_specs=pl.BlockSpec((1, r, d), lambda i, j: (i, 0, j)),
        out_shape=jax.ShapeDtypeStruct((depth, r, d6), F32),
        compiler_params=_cparams(2), name="modulation",
    )(cc, w_mod, b_mod.reshape(depth, 1, d6))


def _proj_in_kernel(x_ref, mod_ref, g_ref, w_ref, *rest, rope_cols, q_cols):
    o_ref = rest[-1]
    h = _normmod(x_ref[0], g_ref[...], mod_ref[0, 0:1, :], mod_ref[0, 1:2, :])
    y = _dot(h.astype(BF16), w_ref[...])
    if rope_cols:
        cos, s_up, s_dn = rest[0][...], rest[1][...], rest[2][...]
        for j in range(rope_cols // LANE):
            t = y[:, j * LANE:(j + 1) * LANE]
            t = t * cos + pltpu.roll(t, LANE - 16, 1) * s_up + pltpu.roll(t, 16, 1) * s_dn
            if j * LANE < q_cols:
                t = t * QUERY_SCALE
            o_ref[0, :, j * LANE:(j + 1) * LANE] = t.astype(o_ref.dtype)
        o_ref[0, :, rope_cols:] = y[:, rope_cols:].astype(o_ref.dtype)
    elif q_cols:
        o_ref[0, :, :q_cols] = (y[:, :q_cols] * QUERY_SCALE).astype(o_ref.dtype)
        o_ref[0, :, q_cols:] = y[:, q_cols:].astype(o_ref.dtype)
    else:
        o_ref[0] = y.astype(o_ref.dtype)


def _proj_in(x, mod, g, w, out_dtype, rope=None, rope_cols=0, q_cols=0):
    b, n, d = x.shape
    nout = w.shape[1]
    tm = _row_tile(n)
    in_specs = [pl.BlockSpec((1, tm, d), lambda i, j: (i, j, 0)),
                pl.BlockSpec((1, 6, d), lambda i, j: (i, 0, 0)),
                pl.BlockSpec((1, d), lambda i, j: (0, 0)),
                pl.BlockSpec((d, nout), lambda i, j: (0, 0))]
    args = [x, mod, g.reshape(1, d), w]
    if rope_cols:
        in_specs += [pl.BlockSpec((tm, LANE), lambda i, j: (j, 0))] * 3
        args += list(rope)
    return pl.pallas_call(
        functools.partial(_proj_in_kernel, rope_cols=rope_cols, q_cols=q_cols), grid=(b, n // tm),
        in_specs=in_specs,
        out_specs=pl.BlockSpec((1, tm, nout), lambda i, j: (i, j, 0)),
        out_shape=jax.ShapeDtypeStruct((b, n, nout), out_dtype),
        compiler_params=_cparams(2, 56), name="proj_in",
    )(*args)


def _rope_tables(n):
    t = jnp.arange(n)
    row = (t // GRID_W).astype(F32)
    col = (t % GRID_W).astype(F32)
    quarter = HEAD_DIM // 4
    inv = ROPE_BASE ** (-jnp.arange(quarter, dtype=F32) / quarter)
    lane = jnp.arange(LANE)
    pos = jnp.where(((lane % HEAD_DIM) // (HEAD_DIM // 2))[None, :] == 0, row[:, None], col[:, None])
    ang = pos * inv[lane % quarter][None, :]
    first = ((lane % (HEAD_DIM // 2)) < quarter)[None, :]
    cos, sin = jnp.cos(ang), jnp.sin(ang)
    return cos, jnp.where(first, -sin, 0.0), jnp.where(first, 0.0, sin)


def _proj_out_kernel(o_ref, x_ref, mod_ref, w_ref, out_ref):
    out_ref[0] = x_ref[0] + mod_ref[0, 2:3, :] * _dot(o_ref[0], w_ref[...])


def _proj_out(o, x, mod, w):
    b, n, d = x.shape
    tm = _row_tile(n)
    return pl.pallas_call(
        _proj_out_kernel, grid=(b, n // tm),
        in_specs=[pl.BlockSpec((1, tm, d), lambda i, j: (i, j, 0)),
                  pl.BlockSpec((1, tm, d), lambda i, j: (i, j, 0)),
                  pl.BlockSpec((1, 6, d), lambda i, j: (i, 0, 0)),
                  pl.BlockSpec((d, d), lambda i, j: (0, 0))],
        out_specs=pl.BlockSpec((1, tm, d), lambda i, j: (i, j, 0)),
        out_shape=jax.ShapeDtypeStruct((b, n, d), F32),
        compiler_params=_cparams(2), name="proj_out",
    )(o, x, mod, w)


def _softmax_parts(scores):
    m = scores[0].max(axis=-1, keepdims=True)
    for s in scores[1:]:
        m = jnp.maximum(m, s.max(axis=-1, keepdims=True))
    es = [jnp.exp2(s - m) for s in scores]
    z = es[0].sum(axis=-1, keepdims=True)
    for e in es[1:]:
        z = z + e.sum(axis=-1, keepdims=True)
    return es, z


def _attn_kernel(*refs, diff, has_lat, post_scale):
    refs = list(refs)
    q_ref = refs.pop(0)
    kv = [(refs.pop(0), refs.pop(0))] if has_lat else []
    kv.append((refs.pop(0), refs.pop(0)))
    if diff:
        lam_ref, sub_ref = refs.pop(0), refs.pop(0)
    o_ref = refs.pop(0)
    tq = q_ref.shape[1]
    lo = lax.broadcasted_iota(jnp.int32, (1, LANE), 1) < HEAD_DIM
    zero = jnp.zeros((), BF16)
    for t in range(q_ref.shape[2] // LANE):
        sl = slice(t * LANE, (t + 1) * LANE)
        qt = q_ref[0, :, sl]
        q2 = jnp.concatenate([jnp.where(lo, qt, zero), jnp.where(lo, zero, qt)], axis=0)
        es, z = _softmax_parts([_dot_nt(q2, k_ref[0, :, sl]) for k_ref, _ in kv])
        acc = jnp.zeros((2 * tq, LANE), F32)
        for e, (_, v_ref) in zip(es, kv):
            acc = acc + _dot(e.astype(BF16), v_ref[0, :, sl])
        acc = acc / z
        if diff:
            o = acc[:tq] - lam_ref[0, 0] * acc[tq:]
            o = o * lax.rsqrt(jnp.mean(o * o, axis=-1, keepdims=True) + EPS) * sub_ref[...] * post_scale
        else:
            o = jnp.where(lo, acc[:tq], acc[tq:])
        o_ref[0, :, sl] = o.astype(o_ref.dtype)


def _attention(qkv_q, qkv_lat, qkv_ctx, d, diff, lam=None, subln=None, post_scale=1.0):
    b, nq, _ = qkv_q.shape
    tq = min(nq, 256)
    has_lat = qkv_lat is not None
    in_specs = [pl.BlockSpec((1, tq, d), lambda i, j: (i, j, 0))]
    args = [qkv_q]
    for src in ([qkv_lat] if has_lat else []) + [qkv_ctx]:
        nk = src.shape[1]
        in_specs += [pl.BlockSpec((1, nk, d), lambda i, j: (i, 0, 1)),
                     pl.BlockSpec((1, nk, d), lambda i, j: (i, 0, 2))]
        args += [src, src]
    if diff:
        in_specs += [pl.BlockSpec(memory_space=pltpu.SMEM), pl.BlockSpec((1, LANE), lambda i, j: (0, 0))]
        args += [lam.reshape(1, 1), subln.reshape(1, LANE)]
    return pl.pallas_call(
        functools.partial(_attn_kernel, diff=diff, has_lat=has_lat, post_scale=post_scale),
        grid=(b, nq // tq), in_specs=in_specs,
        out_specs=pl.BlockSpec((1, tq, d), lambda i, j: (i, j, 0)),
        out_shape=jax.ShapeDtypeStruct((b, nq, d), BF16),
        compiler_params=_cparams(2, 56), name="attention",
    )(*args)


def _na_kernel(cls_ref, q_ref, k_ref, v_ref, kc_ref, vc_ref, bias_ref, o_ref, *, rows, wr, kr):
    del cls_ref
    kstart = jnp.clip(pl.program_id(1) * NA_ROWS - wr // 2, 0, rows - kr)
    start = pl.multiple_of(kstart * GRID_W, GRID_W)
    nk = kr * GRID_W
    tq = q_ref.shape[1]
    lo = lax.broadcasted_iota(jnp.int32, (1, LANE), 1) < HEAD_DIM
    zero = jnp.zeros((), BF16)
    for t in range(q_ref.shape[2] // LANE):
        sl = slice(t * LANE, (t + 1) * LANE)
        qt = q_ref[0, :, sl]
        kw, vw = k_ref[0, pl.ds(start, nk), sl], v_ref[0, pl.ds(start, nk), sl]
        kc, vc = kc_ref[0, :, sl], vc_ref[0, :, sl]
        q2 = jnp.concatenate([jnp.where(lo, qt, zero), jnp.where(lo, zero, qt)], axis=0)
        (el, ec), z = _softmax_parts([_dot_nt(q2, kw) + bias_ref[0, t], _dot_nt(q2, kc)])
        acc = (_dot(el.astype(BF16), vw) + _dot(ec.astype(BF16), vc)) / z
        o_ref[0, :, sl] = jnp.where(lo, acc[:tq], acc[tq:]).astype(o_ref.dtype)


def _na_layout(rows):
    wr = min(WIN_R, rows)
    kr = min(rows, wr + NA_ROWS - 1)
    classes, cls_of_block = [], []
    for blk in range(rows // NA_ROWS):
        kstart = min(max(blk * NA_ROWS - wr // 2, 0), rows - kr)
        key = []
        for qr in range(NA_ROWS):
            r = blk * NA_ROWS + qr
            r0 = min(max(r - wr // 2, 0), rows - wr)
            key.append(tuple((kstart + j - r + WIN_R - 1) if r0 <= kstart + j < r0 + wr else None for j in range(kr)))
        key = tuple(key)
        if key not in classes:
            classes.append(key)
        cls_of_block.append(classes.index(key))
    return wr, kr, classes, cls_of_block


def _na_bias_table(rpb, rows):
    heads = rpb.shape[0]
    _, _, classes, _ = _na_layout(rows)
    cq = jnp.arange(GRID_W)
    c0 = jnp.clip(cq - WIN_C // 2, 0, GRID_W - WIN_C)
    col_in = (cq[None, :] >= c0[:, None]) & (cq[None, :] < c0[:, None] + WIN_C)
    dc = jnp.clip(cq[None, :] - cq[:, None] + WIN_C - 1, 0, 2 * WIN_C - 2)
    dc_onehot = (dc[:, :, None] == jnp.arange(2 * WIN_C - 1)[None, None, :]).astype(F32)
    per_dr = jnp.einsum('hrd,qkd->hrqk', rpb.astype(F32) * math.log2(math.e), dc_onehot,
                        precision=HIGHEST)
    per_dr = jnp.where(col_in[None, None], per_dr, NEG)
    masked = jnp.full((heads, GRID_W, GRID_W), NEG, F32)
    tabs = []
    for key in classes:
        slabs = [jnp.concatenate([masked if dr is None else per_dr[:, dr] for dr in row], axis=-1) for row in key]
        tabs.append(jnp.concatenate(slabs, axis=1))
    tab = jnp.stack(tabs, axis=0)
    return tab.reshape(tab.shape[0], heads // 2, 2 * tab.shape[2], tab.shape[3])


def _na_attention(qkv_l, qkv_c, bias, d):
    b, n, _ = qkv_l.shape
    nc = qkv_c.shape[1]
    rows = n // GRID_W
    wr, kr, _, cls_of_block = _na_layout(rows)
    heads = d // HEAD_DIM
    tq = NA_ROWS * GRID_W
    grid_spec = pltpu.PrefetchScalarGridSpec(
        num_scalar_prefetch=1, grid=(b, rows // NA_ROWS),
        in_specs=[pl.BlockSpec((1, tq, d), lambda i, r, cls: (i, r, 0)),
                  pl.BlockSpec((1, n, d), lambda i, r, cls: (i, 0, 1)),
                  pl.BlockSpec((1, n, d), lambda i, r, cls: (i, 0, 2)),
                  pl.BlockSpec((1, nc, d), lambda i, r, cls: (i, 0, 1)),
                  pl.BlockSpec((1, nc, d), lambda i, r, cls: (i, 0, 2)),
                  pl.BlockSpec((1, heads // 2, 2 * tq, kr * GRID_W), lambda i, r, cls: (cls[r], 0, 0, 0),
                               pipeline_mode=pl.Buffered(1))],
        out_specs=pl.BlockSpec((1, tq, d), lambda i, r, cls: (i, r, 0)))
    return pl.pallas_call(
        functools.partial(_na_kernel, rows=rows, wr=wr, kr=kr),
        grid_spec=grid_spec,
        out_shape=jax.ShapeDtypeStruct((b, n, d), BF16),
        compiler_params=_cparams(2, 56), name="na_attention",
    )(jnp.asarray(cls_of_block, jnp.int32), qkv_l, qkv_l, qkv_l, qkv_c, qkv_c, bias)


def _normmod_kernel(x_ref, mod_ref, g_ref, o_ref):
    o_ref[0] = _normmod(x_ref[0], g_ref[...], mod_ref[0, 0:1, :], mod_ref[0, 1:2, :]).astype(o_ref.dtype)


def _normmod_call(x, mod, g):
    b, n, d = x.shape
    tm = _row_tile(n)
    return pl.pallas_call(
        _normmod_kernel, grid=(b, n // tm),
        in_specs=[pl.BlockSpec((1, tm, d), lambda i, j: (i, j, 0)),
                  pl.BlockSpec((1, 6, d), lambda i, j: (i, 0, 0)),
                  pl.BlockSpec((1, d), lambda i, j: (0, 0))],
        out_specs=pl.BlockSpec((1, tm, d), lambda i, j: (i, j, 0)),
        out_shape=jax.ShapeDtypeStruct((b, n, d), BF16),
        compiler_params=_cparams(2), name="normmod",
    )(x, mod, g.reshape(1, d))


def _s5_tables(a_re, a_im, log_dt, b_re, b_im, c_re, c_im):
    t_len = S5_CHUNK
    a_re, a_im = a_re.astype(F32), a_im.astype(F32)
    dt = jnp.exp(log_dt.astype(F32))[..., None]
    lr, li = a_re * dt, a_im * dt
    cos_li, sin_li = jnp.cos(li), jnp.sin(li)
    ab_im = jnp.exp(lr) * sin_li
    nr = jnp.expm1(lr) * cos_li - 2.0 * jnp.sin(0.5 * li) ** 2
    den = a_re * a_re + a_im * a_im
    fr = (nr * a_re + ab_im * a_im) / den
    fi = (ab_im * a_re - nr * a_im) / den
    b_re, b_im = b_re.astype(F32), b_im.astype(F32)
    bb_re = fr[..., None] * b_re - fi[..., None] * b_im
    bb_im = fr[..., None] * b_im + fi[..., None] * b_re
    tau = jnp.arange(t_len + 1, dtype=F32)[:, None, None, None]
    mag = jnp.exp(tau * lr)
    pr, pi = mag * jnp.cos(tau * li), mag * jnp.sin(tau * li)
    c_re, c_im = c_re.astype(F32), c_im.astype(F32)
    cw_re = c_re[None] * pr[:, :, :, None, :] - c_im[None] * pi[:, :, :, None, :]
    cw_im = c_re[None] * pi[:, :, :, None, :] + c_im[None] * pr[:, :, :, None, :]
    kern = (jnp.einsum('tdgkp,dgph->tdgkh', cw_re, bb_re, precision=HIGHEST)
            - jnp.einsum('tdgkp,dgph->tdgkh', cw_im, bb_im, precision=HIGHEST))
    g = a_re.shape[1]
    h = S5_GROUP
    diff = jnp.arange(t_len)[None, :] - jnp.arange(t_len)[:, None]
    lags = jnp.arange(t_len)[None, None, :]
    fwd_lag = (diff[:, :, None] == lags).astype(F32)
    bwd_lag = (-diff[:, :, None] == lags).astype(F32)
    k_f = jnp.einsum('sta,agkh->stgkh', fwd_lag, kern[:t_len, 0], precision=HIGHEST)
    k_b = jnp.einsum('sta,agkh->stgkh', bwd_lag, kern[:t_len, 1], precision=HIGHEST)
    m_both = (k_f + k_b).transpose(2, 0, 4, 1, 3).reshape(g, t_len * h, t_len * h)

    def in_map(d, p_re, p_im):
        wr_ = p_re[..., None] * bb_re[d][None] - p_im[..., None] * bb_im[d][None]
        wi_ = p_re[..., None] * bb_im[d][None] + p_im[..., None] * bb_re[d][None]
        to_rows = lambda w: w.transpose(1, 0, 3, 2).reshape(g, t_len * h, -1)
        return to_rows(wr_), to_rows(wi_)

    f_re, f_im = in_map(0, pr[:t_len, 0][::-1], pi[:t_len, 0][::-1])
    r_re, r_im = in_map(1, pr[:t_len, 1], pi[:t_len, 1])
    w_in = jnp.concatenate([f_re, f_im, f_im, f_re, r_re, r_im, r_im, r_re], axis=-1)

    def out_map(w_re, w_im):
        to_cols = lambda w: w.transpose(1, 3, 0, 2).reshape(g, -1, t_len * h)
        return jnp.concatenate([to_cols(w_re), -to_cols(w_im)], axis=1)

    w_out = jnp.concatenate([out_map(cw_re[1:, 0], cw_im[1:, 0]),
                             out_map(cw_re[1:, 1][::-1], cw_im[1:, 1][::-1])], axis=1)
    a_pow = jnp.stack([jnp.concatenate([pr[t_len, 0], pr[t_len, 0]], -1),
                       jnp.concatenate([-pi[t_len, 0], pi[t_len, 0]], -1),
                       jnp.concatenate([pr[t_len, 1], pr[t_len, 1]], -1),
                       jnp.concatenate([-pi[t_len, 1], pi[t_len, 1]], -1)], axis=1)
    return m_both.astype(BF16), w_in.astype(BF16), w_out.astype(BF16), a_pow


def _s5_core_kernel(uc_ref, ul_ref, m_ref, win_ref, wout_ref, a_ref, yc_ref, yl_ref, sall_ref, sin_ref,
                    *, bn, nc_c, nc_l):
    rc = nc_c * bn
    sin_ref[0:rc, :] = _dot(uc_ref[0], win_ref[0])
    sin_ref[rc:, :] = _dot(ul_ref[0], win_ref[0])
    a = a_ref[0]
    a1f, a2f, a1b, a2b = a[0:1], a[1:2], a[2:3], a[3:4]
    w = LANE
    zero = jnp.zeros((bn, w), F32)

    def rows_of(c):
        return pl.ds(pl.multiple_of(c * bn, bn), bn)

    def fwd(c, carry):
        s, sw = carry
        r = rows_of(c)
        sall_ref[r, 0:w] = s
        return (a1f * s + a2f * sw + sin_ref[r, 0:w], a1f * sw - a2f * s + sin_ref[r, w:2 * w])

    def bwd(c, carry):
        s, sw = carry
        r = rows_of(c)
        sall_ref[r, w:2 * w] = s
        return (a1b * s + a2b * sw + sin_ref[r, 2 * w:3 * w], a1b * sw - a2b * s + sin_ref[r, 3 * w:4 * w])

    lax.fori_loop(0, nc_c + nc_l, fwd, (zero, zero))
    carry = lax.fori_loop(0, nc_c, lambda i, cr: bwd(nc_c - 1 - i, cr), (zero, zero))
    lax.fori_loop(0, nc_l, lambda i, cr: bwd(nc_c + nc_l - 1 - i, cr), carry)
    yc_ref[0] = (_dot(uc_ref[0], m_ref[0]) + _dot(sall_ref[0:rc, :].astype(BF16), wout_ref[0])).astype(yc_ref.dtype)
    yl_ref[0] = (_dot(ul_ref[0], m_ref[0]) + _dot(sall_ref[rc:, :].astype(BF16), wout_ref[0])).astype(yl_ref.dtype)


def _s5_core(uc_rows, ul_rows, tables, bn):
    g, rc, width = uc_rows.shape
    rl = ul_rows.shape[1]
    m_both, w_in, w_out, a_pow = tables
    rows = lambda r: pl.BlockSpec((1, r, width), lambda i: (i, 0, 0))
    return pl.pallas_call(
        functools.partial(_s5_core_kernel, bn=bn, nc_c=rc // bn, nc_l=rl // bn), grid=(g,),
        in_specs=[rows(rc), rows(rl),
                  pl.BlockSpec((1, width, width), lambda i: (i, 0, 0)),
                  pl.BlockSpec((1, width, 4 * LANE), lambda i: (i, 0, 0)),
                  pl.BlockSpec((1, 2 * LANE, width), lambda i: (i, 0, 0)),
                  pl.BlockSpec((1, 4, LANE), lambda i: (i, 0, 0))],
        out_specs=[rows(rc), rows(rl)],
        out_shape=[jax.ShapeDtypeStruct((g, rc, width), BF16), jax.ShapeDtypeStruct((g, rl, width), BF16)],
        scratch_shapes=[pltpu.VMEM((rc + rl, 2 * LANE), F32), pltpu.VMEM((rc + rl, 4 * LANE), F32)],
        compiler_params=_cparams(1, 56), name="s5_core",
    )(uc_rows, ul_rows, m_both, w_in, w_out, a_pow)


def _s5_glu_kernel(x_ref, y_ref, mod_ref, g_ref, d_ref, w_ref, out_ref):
    x = x_ref[0]
    d = x.shape[1]
    u = _normmod(x, g_ref[...], mod_ref[0, 0:1, :], mod_ref[0, 1:2, :])
    z = jax.nn.gelu(y_ref[0] + d_ref[...] * u)
    zz = _dot(z.astype(BF16), w_ref[...])
    out_ref[0] = x + mod_ref[0, 2:3, :] * (zz[:, :d] * jax.nn.sigmoid(zz[:, d:]))


def _s5_glu(x, y, mod, g, dskip, w_glu):
    b, n, d = x.shape
    tm = _row_tile(n)
    return pl.pallas_call(
        _s5_glu_kernel, grid=(b, n // tm),
        in_specs=[pl.BlockSpec((1, tm, d), lambda i, j: (i, j, 0)),
                  pl.BlockSpec((1, tm, d), lambda i, j: (i, j, 0)),
                  pl.BlockSpec((1, 6, d), lambda i, j: (i, 0, 0)),
                  pl.BlockSpec((1, d), lambda i, j: (0, 0)),
                  pl.BlockSpec((1, d), lambda i, j: (0, 0)),
                  pl.BlockSpec((d, 2 * d), lambda i, j: (0, 0))],
        out_specs=pl.BlockSpec((1, tm, d), lambda i, j: (i, j, 0)),
        out_shape=jax.ShapeDtypeStruct((b, n, d), F32),
        compiler_params=_cparams(2), name="s5_glu",
    )(x, y, mod, g.reshape(1, d), dskip.reshape(1, d), w_glu)


def _s5_layer(x, h_ctx, mod_l, mod_c, g, tables, dskip, w_glu):
    b, n, d = x.shape
    n_c = h_ctx.shape[1]
    t_len, h = S5_CHUNK, S5_GROUP
    groups = d // h
    nc_l, nc_c = n // t_len, n_c // t_len

    def to_rows(u, nc):
        return u.reshape(b, nc, t_len, groups, h).transpose(3, 1, 0, 2, 4).reshape(groups, nc * b, t_len * h)

    def from_rows(y, nc):
        return y.reshape(groups, nc, b, t_len, h).transpose(2, 1, 3, 0, 4).reshape(b, nc * t_len, d)

    yc_rows, yl_rows = _s5_core(to_rows(_normmod_call(h_ctx, mod_c, g), nc_c),
                                to_rows(_normmod_call(x, mod_l, g), nc_l), tables, b)
    y_c, y_l = from_rows(yc_rows, nc_c), from_rows(yl_rows, nc_l)
    return (_s5_glu(h_ctx, y_c, mod_c, g, dskip, w_glu), _s5_glu(x, y_l, mod_l, g, dskip, w_glu))


def _hg_kernel(ql_ref, ffl_ref, fbl_ref, il_ref, ffc_ref, fbc_ref, ic_ref, lb_ref, wf_ref, wb_ref, mask_ref,
               o_ref, ob_ref, sf_ref, sb_ref, *, nc_c, nc_l):
    cn = HG_CHUNK
    levels = cn.bit_length() - 1
    small = min(levels, HG_SMALL_LEVELS)
    nh = HG_HEADS_PER_STEP
    lb = lb_ref[...]
    row = lax.broadcasted_iota(jnp.int32, (cn, 1), 0)

    def gates(ff):
        f = lb + (1.0 - lb) * jax.nn.sigmoid(ff)
        return jnp.log(f), 1.0 - f

    def dot_hi_lo(w01, x):
        hi, low = _split_hi_lo(x)
        return _dot(w01, hi) + _dot(w01, low)

    def visit(rows, q_ref, ff_ref, i_ref, w_ref, s_ref, out_ref, d_idx):
        fwd = d_idx == 0
        lf, k = gates(ff_ref[0, rows, :])
        v = _silu(i_ref[0, rows, :])
        n_blocks = 1 if out_ref is None else 1 + small
        sums = dot_hi_lo(w_ref[0:n_blocks * cn, :], lf)
        cum = sums[0:cn]
        total = cum[cn - 1:cn] if fwd else cum[0:1]
        kd = (k * jnp.exp(total - cum)).astype(BF16)
        decay = jnp.exp(total)
        vb = v.astype(BF16)
        if out_ref is not None:
            q = q_ref[0, rows, :]
            qd = (q * jnp.exp(cum)).astype(BF16)
            qk = q * k
            scaled = []
            for lv in range(levels):
                bs = cn >> (lv + 1)
                is_query = ((row // bs) % 2 == 1) if fwd else ((row // bs) % 2 == 0)
                if lv < levels - small:
                    pairs = cum.reshape(cn // (2 * bs), 2 * bs, cum.shape[1])
                    a_row = bs - 1 if fwd else bs
                    anchor = jnp.broadcast_to(pairs[:, a_row:a_row + 1, :], pairs.shape).reshape(cum.shape)
                    to_anchor = jnp.where(is_query, cum - anchor, anchor - cum)
                else:
                    blk = 1 + lv - (levels - small)
                    to_anchor = sums[blk * cn:(blk + 1) * cn]
                scaled.append((jnp.where(is_query, q, k) * jnp.exp(to_anchor)).astype(BF16))
            outs = []
            for hh in range(nh):
                sl = slice(hh * LANE, (hh + 1) * LANE)
                att = jnp.zeros((cn, cn), BF16)
                for lv in range(levels):
                    x = scaled[lv][:, sl]
                    att = att + _dot_nt(x, x).astype(BF16) * mask_ref[d_idx * levels + lv]
                same_token = jnp.sum(qk[:, sl], axis=-1, keepdims=True) * v[:, sl]
                outs.append(_dot_nt(qd[:, sl], s_ref[hh].astype(BF16)) + _dot(att, vb[:, sl]) + same_token)
            dst = out_ref.at[0] if len(out_ref.shape) == 3 else out_ref
            dst[rows, :] = jnp.concatenate(outs, axis=-1)
        for hh in range(nh):
            sl = slice(hh * LANE, (hh + 1) * LANE)
            s_ref[hh] = s_ref[hh] * decay[:, sl] + _dot(v[:, sl].T.astype(BF16), kd[:, sl])

    def rows_of(c):
        return pl.ds(pl.multiple_of(c * cn, cn), cn)

    sf_ref[...] = jnp.zeros(sf_ref.shape, F32)
    sb_ref[...] = jnp.zeros(sb_ref.shape, F32)

    def ctx_body(j, carry):
        visit(rows_of(j), None, ffc_ref, ic_ref, wf_ref, sf_ref, None, 0)
        visit(rows_of(nc_c - 1 - j), None, fbc_ref, ic_ref, wb_ref, sb_ref, None, 1)
        return carry

    def lat_body(j, carry):
        visit(rows_of(j), ql_ref, ffl_ref, il_ref, wf_ref, sf_ref, o_ref, 0)
        visit(rows_of(nc_l - 1 - j), ql_ref, fbl_ref, il_ref, wb_ref, sb_ref, ob_ref, 1)
        return carry

    lax.fori_loop(0, nc_c, ctx_body, 0)
    lax.fori_loop(0, nc_l, lat_body, 0)
    o_ref[0] = o_ref[0] + ob_ref[...]


def _hg_operators():
    cn = HG_CHUNK
    levels = cn.bit_length() - 1
    t = np.arange(cn)[:, None]
    r = np.arange(cn)[None, :]
    small = min(levels, HG_SMALL_LEVELS)
    w = np.zeros((2, 1 + small, cn, cn), np.float32)
    mask = np.zeros((2, levels, cn, cn), np.float32)
    w[0, 0] = r <= t
    w[1, 0] = r >= t
    for lv in range(levels):
        bs = cn >> (lv + 1)
        parent = t // (2 * bs) * (2 * bs)
        second = (t // bs) % 2 == 1
        blk = 1 + lv - (levels - small)
        if blk >= 1:
            anchor_f = parent + bs - 1
            w[0, blk] = np.where(second, (r > anchor_f) & (r <= t), (r > t) & (r <= anchor_f))
            anchor_b = parent + bs
            w[1, blk] = np.where(second, (r >= anchor_b) & (r < t), (r >= t) & (r < anchor_b))
        same_parent = (t // (2 * bs)) == (r // (2 * bs))
        key_second = (r // bs) % 2 == 1
        mask[0, lv] = same_parent & second & ~key_second
        mask[1, lv] = same_parent & ~second & key_second
    return (jnp.asarray(w.reshape(2, (1 + small) * cn, cn), BF16),
            jnp.asarray(mask.reshape(2 * levels, cn, cn), BF16))


def _hg_scan(z_l, z_c, lb, d):
    b, n, _ = z_l.shape
    n_c = z_c.shape[1]
    width = HG_HEADS_PER_STEP * LANE
    steps = d // width
    cn = HG_CHUNK
    w, mask = _hg_operators()
    col = lambda k: (lambda i, h: (i, 0, k * steps + h))
    lat = lambda k: pl.BlockSpec((1, n, width), col(k))
    ctx = lambda k: pl.BlockSpec((1, n_c, width), col(k))
    const = lambda a: pl.BlockSpec(a.shape, lambda i, h: (0,) * a.ndim)
    return pl.pallas_call(
        functools.partial(_hg_kernel, nc_c=n_c // cn, nc_l=n // cn), grid=(b, steps),
        in_specs=[lat(0), lat(1), lat(2), lat(3), ctx(1), ctx(2), ctx(3),
                  pl.BlockSpec((1, width), lambda i, h: (0, h)),
                  const(w[0]), const(w[1]), const(mask)],
        out_specs=pl.BlockSpec((1, n, width), lambda i, h: (i, 0, h)),
        out_shape=jax.ShapeDtypeStruct((b, n, d), F32),
        scratch_shapes=[pltpu.VMEM((n, width), F32),
                        pltpu.VMEM((HG_HEADS_PER_STEP, LANE, LANE), F32),
                        pltpu.VMEM((HG_HEADS_PER_STEP, LANE, LANE), F32)],
        compiler_params=_cparams(2), name="hgrn2_scan",
    )(z_l, z_l, z_l, z_l, z_c, z_c, z_c, lb.reshape(1, d), w[0], w[1], mask)


def _hg_out_kernel(o_ref, z_ref, gn_ref, x_ref, mod_ref, w_ref, out_ref):
    o, gate = o_ref[0], z_ref[0]
    parts = []
    for h in range(o.shape[1] // LANE):
        sl = slice(h * LANE, (h + 1) * LANE)
        t = o[:, sl]
        t = t * lax.rsqrt(jnp.mean(t * t, axis=-1, keepdims=True) + EPS) * gn_ref[...]
        parts.append((t * _silu(gate[:, sl])).astype(BF16))
    y = _dot(jnp.concatenate(parts, axis=-1), w_ref[...])
    out_ref[0] = x_ref[0] + mod_ref[0, 2:3, :] * y


def _hg_out(o, z, gnorm, x, mod, w):
    b, n, d = x.shape
    tm = _row_tile(n)
    return pl.pallas_call(
        _hg_out_kernel, grid=(b, n // tm),
        in_specs=[pl.BlockSpec((1, tm, d), lambda i, j: (i, j, 0)),
                  pl.BlockSpec((1, tm, d), lambda i, j: (i, j, 4)),
                  pl.BlockSpec((1, LANE), lambda i, j: (0, 0)),
                  pl.BlockSpec((1, tm, d), lambda i, j: (i, j, 0)),
                  pl.BlockSpec((1, 6, d), lambda i, j: (i, 0, 0)),
                  pl.BlockSpec((d, d), lambda i, j: (0, 0))],
        out_specs=pl.BlockSpec((1, tm, d), lambda i, j: (i, j, 0)),
        out_shape=jax.ShapeDtypeStruct((b, n, d), F32),
        compiler_params=_cparams(2), name="hgrn2_out",
    )(o, z, gnorm.reshape(1, LANE), x, mod, w)


def _moe_pre_kernel(x_ref, mod_ref, g_ref, wr_ref, h_ref, aff_ref):
    h = _normmod(x_ref[0], g_ref[...], mod_ref[0, 3:4, :], mod_ref[0, 4:5, :])
    h_hi, h_lo = _split_hi_lo(h)
    h_ref[0] = h_hi
    logits = _dot(h_hi, wr_ref[0]) + _dot(h_lo, wr_ref[0]) + _dot(h_hi, wr_ref[1])
    lane = lax.broadcasted_iota(jnp.int32, logits.shape, 1)
    logits = jnp.where(lane < N_EXPERTS, logits, NEG)
    e = jnp.exp(logits - logits.max(axis=-1, keepdims=True))
    aff_ref[0] = e / e.sum(axis=-1, keepdims=True)


def _moe_pre(x, mod, g, wr_pad):
    b, n, d = x.shape
    tm = _row_tile(n)
    return pl.pallas_call(
        _moe_pre_kernel, grid=(b, n // tm),
        in_specs=[pl.BlockSpec((1, tm, d), lambda i, j: (i, j, 0)),
                  pl.BlockSpec((1, 6, d), lambda i, j: (i, 0, 0)),
                  pl.BlockSpec((1, d), lambda i, j: (0, 0)),
                  pl.BlockSpec((2, d, LANE), lambda i, j: (0, 0, 0))],
        out_specs=[pl.BlockSpec((1, tm, d), lambda i, j: (i, j, 0)),
                   pl.BlockSpec((1, tm, LANE), lambda i, j: (i, j, 0))],
        out_shape=[jax.ShapeDtypeStruct((b, n, d), BF16), jax.ShapeDtypeStruct((b, n, LANE), F32)],
        compiler_params=_cparams(2), name="moe_pre",
    )(x, mod, g.reshape(1, d), wr_pad)


def _route_kernel(aff_ref, tri_ref, pos_ref, selw_ref, start_ref, *, cap):
    groups = LANE // N_EXPERTS
    rg = aff_ref.shape[1] // groups
    samples = range(aff_ref.shape[0])
    denses = []
    for s in samples:
        dense = aff_ref[s, 0:rg, :]
        for g in range(1, groups):
            dense = dense + pltpu.roll(aff_ref[s, g * rg:(g + 1) * rg, :], g * N_EXPERTS, 1)
        denses.append(dense)
    all_bits = [lax.bitcast_convert_type(dense, jnp.int32) for dense in denses]
    lane = lax.broadcasted_iota(jnp.int32, (1, LANE), 1)
    tri = tri_ref[...]

    def indicator(mask):
        return jnp.where(mask, jnp.ones((), F32), jnp.zeros((), F32))

    def over_groups(row):
        for shift in (LANE // 2, LANE // 4, LANE // 8):
            row = row + pltpu.roll(row, shift, 1)
        return row

    def before_groups(row):
        out = jnp.zeros_like(row)
        for j in range(1, groups):
            out = out + jnp.where(lane >= j * N_EXPERTS, pltpu.roll(row, j * N_EXPERTS, 1), 0.0)
        return out

    def count(mask):
        return over_groups(jnp.sum(indicator(mask), axis=0, keepdims=True))

    def prefix(mask):
        x = indicator(mask)
        return _dot(tri, x.astype(BF16)) + before_groups(jnp.sum(x, axis=0, keepdims=True))

    def search(i, thrs):
        bit = jnp.left_shift(jnp.int32(1), 30 - i)
        return tuple(jnp.where(count(bits >= (thr | bit)) >= cap, thr | bit, thr) for bits, thr in zip(all_bits, thrs))

    thrs = lax.fori_loop(0, 31, search, tuple(jnp.zeros((1, LANE), jnp.int32) for _ in samples))
    for s, dense, bits, thr in zip(samples, denses, all_bits, thrs):
        above, tie = bits > thr, bits == thr
        need = cap - count(above)
        tie_rank = prefix(tie)
        sel = above | (tie & (tie_rank <= need))
        pos_ref[s] = jnp.where(sel, prefix(above) + jnp.minimum(tie_rank, need) - 1.0, -1.0)
        selw_ref[s] = jnp.where(sel, dense, 0.0)
        start_ref[s] = before_groups(jnp.sum(indicator(sel), axis=0, keepdims=True))


def _route(aff, cap):
    b, n, _ = aff.shape
    rg = n // (LANE // N_EXPERTS)
    idx = jnp.arange(rg)
    tri = (idx[None, :] <= idx[:, None]).astype(BF16)
    sb = math.gcd(b, ROUTE_SAMPLES)
    out_spec = pl.BlockSpec((sb, rg, LANE), lambda i: (i, 0, 0))
    return pl.pallas_call(
        functools.partial(_route_kernel, cap=cap), grid=(b // sb,),
        in_specs=[pl.BlockSpec((sb, n, LANE), lambda i: (i, 0, 0)), pl.BlockSpec((rg, rg), lambda i: (0, 0))],
        out_specs=[out_spec, out_spec, pl.BlockSpec((sb, 1, LANE), lambda i: (i, 0, 0))],
        out_shape=[jax.ShapeDtypeStruct((b, rg, LANE), F32)] * 2 + [jax.ShapeDtypeStruct((b, 1, LANE), F32)],
        compiler_params=_cparams(1), name="moe_route",
    )(aff, tri)


def _gather_kernel(pos_ref, h_ref, xs_ref, *, cap):
    h = h_ref[0]
    slot = lax.broadcasted_iota(jnp.int32, (cap, h.shape[0]), 0).astype(F32)
    for e in range(N_EXPERTS):
        onehot = jnp.where(pos_ref[0, e:e + 1, :] == slot, 1.0, 0.0).astype(BF16)
        xs_ref[e, 0] = _dot(onehot, h).astype(BF16)


def _gather(pos_rows, h, cap):
    b, n, d = h.shape
    return pl.pallas_call(
        functools.partial(_gather_kernel, cap=cap), grid=(b,),
        in_specs=[pl.BlockSpec((1, N_EXPERTS, n), lambda i: (i, 0, 0)),
                  pl.BlockSpec((1, n, d), lambda i: (i, 0, 0))],
        out_specs=pl.BlockSpec((N_EXPERTS, 1, cap, d), lambda i: (0, i, 0, 0)),
        out_shape=jax.ShapeDtypeStruct((N_EXPERTS, b, cap, d), BF16),
        compiler_params=_cparams(1), name="moe_gather",
    )(pos_rows, h)


def _ffn_kernel(xs_ref, wg_ref, wu_ref, wd_ref, o_ref):
    xs = xs_ref[0]
    hid = _silu(_dot(xs, wg_ref[0, 0])) * _dot(xs, wu_ref[0, 0])
    o_ref[0] = _dot(hid.astype(BF16), wd_ref[0, 0]).astype(BF16)


def _ffn(xs, wg, wu, wd, layer):
    e, m, d = xs.shape
    ff = wg.shape[3]
    tm = _row_tile(m)
    return pl.pallas_call(
        _ffn_kernel, grid=(e, m // tm),
        in_specs=[pl.BlockSpec((1, tm, d), lambda i, j: (i, j, 0)),
                  pl.BlockSpec((1, 1, d, ff), lambda i, j: (layer, i, 0, 0)),
                  pl.BlockSpec((1, 1, d, ff), lambda i, j: (layer, i, 0, 0)),
                  pl.BlockSpec((1, 1, ff, d), lambda i, j: (layer, i, 0, 0))],
        out_specs=pl.BlockSpec((1, tm, d), lambda i, j: (i, j, 0)),
        out_shape=jax.ShapeDtypeStruct((e, m, d), BF16),
        compiler_params=_cparams(2, 56), name="moe_ffn",
    )(xs, wg, wu, wd)


def _combine_kernel(*refs, cap, final, windowed):
    refs = list(refs)
    off_ref = refs.pop(0) if windowed else None
    out_ref, pos_ref, selw_ref, x_ref, mod_ref = refs[:5]
    o_ref = refs[-2] if windowed else refs[-1]
    x = x_ref[0]
    pos, selw = pos_ref[0], selw_ref[0]

    def all_slots():
        slot = lax.broadcasted_iota(jnp.int32, (x.shape[0], cap), 1).astype(F32)
        y = jnp.zeros(x.shape, F32)
        for e in range(N_EXPERTS):
            weighted = jnp.where(pos[:, e:e + 1] == slot, selw[:, e:e + 1], 0.0).astype(BF16)
            y = y + _dot(weighted, out_ref[e, 0])
        return y

    if windowed:
        y_ref = refs[-1]
        half = cap // 2
        i, g, last = pl.program_id(0), pl.program_id(1), pl.num_programs(1) - 1
        starts, fits = [], None
        for e in range(N_EXPERTS):
            lo = off_ref[i, g, e]
            hi = jnp.where(g < last, off_ref[i, jnp.minimum(g + 1, last), e], cap)
            w = jnp.minimum(lo // 16 * 16, cap - half)
            starts.append(w)
            ok = hi - w <= half
            fits = ok if fits is None else jnp.logical_and(fits, ok)

        @pl.when(fits)
        def _():
            pos_i = pos.astype(jnp.int32)
            slot = lax.broadcasted_iota(jnp.int32, (x.shape[0], half), 1)
            y = jnp.zeros(x.shape, F32)
            for e in range(0, N_EXPERTS, 2):
                lhs, rhs = [], []
                for k in (e, e + 1):
                    w = pl.multiple_of(starts[k], 16)
                    lhs.append(jnp.where(pos_i[:, k:k + 1] == slot + w, selw[:, k:k + 1], 0.0).astype(BF16))
                    rhs.append(out_ref[k, 0, pl.ds(w, half), :])
                y = y + _dot(jnp.concatenate(lhs, axis=1), jnp.concatenate(rhs, axis=0))
            y_ref[...] = y

        @pl.when(jnp.logical_not(fits))
        def _():
            y_ref[...] = all_slots()

        y = y_ref[...]
    else:
        y = all_slots()
    x = x + mod_ref[0, 5:6, :] * y
    if final:
        x = x * lax.rsqrt(jnp.mean(x * x, axis=-1, keepdims=True) + EPS) * refs[5][...]
    o_ref[0] = x


def _combine(out, pos, selw, starts, x, mod, cap, norm_f):
    b, n, d = x.shape
    groups = starts.shape[1]
    windowed = cap // 2 >= LANE and (n // groups) % 8 == 0
    tm = n // groups if windowed else _row_tile(n)
    final = norm_f is not None
    in_specs = [pl.BlockSpec((N_EXPERTS, 1, cap, d), lambda i, j, *_: (0, i, 0, 0)),
                pl.BlockSpec((1, tm, N_EXPERTS), lambda i, j, *_: (i, j, 0)),
                pl.BlockSpec((1, tm, N_EXPERTS), lambda i, j, *_: (i, j, 0)),
                pl.BlockSpec((1, tm, d), lambda i, j, *_: (i, j, 0)),
                pl.BlockSpec((1, 6, d), lambda i, j, *_: (i, 0, 0))]
    args = [out, pos, selw, x, mod]
    if final:
        in_specs.append(pl.BlockSpec((1, d), lambda i, j, *_: (0, 0)))
        args.append(norm_f.reshape(1, d))
    grid_spec = pltpu.PrefetchScalarGridSpec(
        num_scalar_prefetch=1 if windowed else 0, grid=(b, n // tm), in_specs=in_specs,
        out_specs=pl.BlockSpec((1, tm, d), lambda i, j, *_: (i, j, 0)),
        scratch_shapes=[pltpu.VMEM((tm, d), F32)] if windowed else [])
    return pl.pallas_call(
        functools.partial(_combine_kernel, cap=cap, final=final, windowed=windowed), grid_spec=grid_spec,
        out_shape=jax.ShapeDtypeStruct((b, n, d), F32),
        compiler_params=_cparams(2), name="moe_combine",
    )(*(([starts] if windowed else []) + args))


def _router_parts(w_router):
    w = jnp.pad(w_router.astype(F32), ((0, 0), (0, LANE - N_EXPERTS)))
    return jnp.stack(_split_hi_lo(w), axis=0)


def _moe(x, mod, g, wr_pad, wg, wu, wd, layer, norm_f=None):
    b, n, d = x.shape
    cap = CAPACITY_FACTOR * n // N_EXPERTS
    groups = LANE // N_EXPERTS
    h, aff = _moe_pre(x, mod, g, wr_pad)
    pos, selw, starts = _route(aff, cap)
    starts = starts.reshape(b, groups, N_EXPERTS).astype(jnp.int32)
    split = lambda a: a.reshape(b, n // groups, groups, N_EXPERTS)
    by_token = lambda a: split(a).transpose(0, 2, 1, 3).reshape(b, n, N_EXPERTS)
    by_expert = lambda a: split(a).transpose(0, 3, 2, 1).reshape(b, N_EXPERTS, n)
    xs = _gather(by_expert(pos), h, cap)
    out = _ffn(xs.reshape(N_EXPERTS, b * cap, d), wg, wu, wd, layer).reshape(N_EXPERTS, b, cap, d)
    return _combine(out, by_token(pos), by_token(selw), starts, x, mod, cap, norm_f)


def kernel(x, c, ctx, c_ctx, w_mod, b_mod, norm1, norm2, s5_a_re, s5_a_im, s5_log_dt, s5_b_re, s5_b_im, s5_c_re, s5_c_im, s5_d, s5_w_glu, na_w_qkv, na_w_o, na_rpb, da_w_qkv, da_w_o, da_lambda, da_subln, hg_w_in, hg_w_o, hg_gnorm, hg_lower_bounds, moe_router, moe_w_gate, moe_w_up, moe_w_down, norm_f):
    b, n, d = x.shape
    depth = w_mod.shape[0]
    assert depth == 4, "layer i uses mixer i; only the last layer drops the context stream"

    rows_pad = -(b + 1) % 8
    cc = jnp.concatenate([c, c_ctx[None, :], jnp.zeros((rows_pad, d), F32)], axis=0)
    mod_all = _modulation(cc, w_mod, b_mod)
    p_lb = jax.nn.softmax(hg_lower_bounds.astype(F32), axis=0)
    lbs = jnp.cumsum(p_lb, axis=0) - p_lb[0]
    wg, wu, wd = moe_w_gate.astype(BF16), moe_w_up.astype(BF16), moe_w_down.astype(BF16)
    h_ctx = ctx

    for i in range(depth):
        last = i == depth - 1
        mod_l = mod_all[i, :b].reshape(b, 6, d)
        mod_c = jnp.broadcast_to(mod_all[i, b].reshape(1, 6, d), (b, 6, d))
        g1 = norm1[i]
        if i == 0:
            tables = _s5_tables(s5_a_re[0], s5_a_im[0], s5_log_dt[0], s5_b_re[0], s5_b_im[0], s5_c_re[0], s5_c_im[0])
            h_ctx, x = _s5_layer(x, h_ctx, mod_l, mod_c, g1, tables, s5_d[0], s5_w_glu[0].astype(BF16))
        elif i == 1:
            w_qkv = na_w_qkv[0].astype(BF16)
            qkv_l = _proj_in(x, mod_l, g1, w_qkv, BF16, q_cols=d)
            qkv_c = _proj_in(h_ctx, mod_c, g1, w_qkv, BF16, q_cols=d)
            bias = _na_bias_table(na_rpb[0], n // GRID_W)
            w_o = na_w_o[0].astype(BF16)
            x = _proj_out(_na_attention(qkv_l, qkv_c, bias, d), x, mod_l, w_o)
            h_ctx = _proj_out(_attention(qkv_c, None, qkv_c, d, diff=False), h_ctx, mod_c, w_o)
        elif i == 2:
            w_qkv = da_w_qkv[0].astype(BF16)
            qkv_l = _proj_in(x, mod_l, g1, w_qkv, BF16, rope=_rope_tables(n), rope_cols=2 * d, q_cols=d)
            qkv_c = _proj_in(h_ctx, mod_c, g1, w_qkv, BF16, q_cols=d)
            lam = da_lambda[0].astype(F32)
            lam_init = 0.8 - 0.6 * math.exp(-0.3 * i)
            lam_full = jnp.exp(jnp.sum(lam[0] * lam[1])) - jnp.exp(jnp.sum(lam[2] * lam[3])) + lam_init
            w_o = da_w_o[0].astype(BF16)
            attend = functools.partial(_attention, d=d, diff=True, lam=lam_full, subln=da_subln[0],
                                       post_scale=1.0 - lam_init)
            x = _proj_out(attend(qkv_l, qkv_l, qkv_c), x, mod_l, w_o)
            h_ctx = _proj_out(attend(qkv_c, None, qkv_c), h_ctx, mod_c, w_o)
        else:
            w_in = hg_w_in[0].astype(BF16)
            z_l = _proj_in(x, mod_l, g1, w_in, F32)
            z_c = _proj_in(h_ctx, mod_c, g1, w_in, F32)
            x = _hg_out(_hg_scan(z_l, z_c, lbs[i], d), z_l, hg_gnorm[0], x, mod_l, hg_w_o[0].astype(BF16))

        wr = _router_parts(moe_router[i])
        x = _moe(x, mod_l, norm2[i], wr, wg, wu, wd, i, norm_f if last else None)
        if not last:
            h_ctx = _moe(h_ctx, mod_c, norm2[i], wr, wg, wu, wd, i)
    return x
```

```python
import functools
import math

import jax
import jax.numpy as jnp
import numpy as np
from jax import lax
from jax.experimental import pallas as pl
from jax.experimental.pallas import tpu as pltpu

F32 = jnp.float32
BF16 = jnp.bfloat16
HIGHEST = lax.Precision.HIGHEST
EPS = 1e-6
NEG = -1e30

LANE = 128
GRID_W = 64
WIN_R = 8
WIN_C = 16
ATTN_ROWS = 256
NA_ROWS = 4
HEAD_DIM = 64
QUERY_SCALE = HEAD_DIM ** -0.5 * math.log2(math.e)
ROPE_BASE = 10000.0
S5_GROUP = 16
S5_CHUNK = 16
HG_CHUNK = 256
HG_HEADS_PER_STEP = 2
HG_SMALL_LEVELS = 3
N_EXPERTS = 16
CAPACITY_FACTOR = 2
ROUTE_SAMPLES = 4
ROW_TILE = 512
EPILOGUE_ROWS = 256


def _cparams(n_axes, vmem_mb=48):
    return pltpu.CompilerParams(dimension_semantics=("arbitrary",) * n_axes,
                                vmem_limit_bytes=vmem_mb * 1024 * 1024)


def _dot(a, b):
    return jnp.dot(a, b, preferred_element_type=F32)


def _dot_nt(a, b):
    return lax.dot_general(a, b, (((1,), (1,)), ((), ())), preferred_element_type=F32)


def _normmod(x, g, shift, scale):
    y = x * lax.rsqrt(jnp.mean(x * x, axis=-1, keepdims=True) + EPS)
    return (y * g) * (1.0 + scale) + shift


def _split_hi_lo(x):
    bits = lax.bitcast_convert_type(x, jnp.int32)
    hi = lax.bitcast_convert_type((bits + jnp.int32(0x8000)) & jnp.int32(-65536), F32)
    return hi.astype(BF16), (x - hi).astype(BF16)


def _row_chunks(tm):
    step = min(tm, EPILOGUE_ROWS)
    return [slice(r, r + step) for r in range(0, tm, step)]


def _router_tail(x, rows, mod_ref, g2_ref, wr_ref, x_ref, h_ref, aff_ref):
    x_ref[0, rows, :] = x
    h = _normmod(x, g2_ref[...], mod_ref[0, 3:4, :], mod_ref[0, 4:5, :])
    h_hi, h_lo = _split_hi_lo(h)
    h_ref[0, rows, :] = h_hi
    both = _dot(h_hi, wr_ref[...]) + _dot(h_lo, wr_ref[...])
    logits = both[:, :LANE] + both[:, LANE:]
    lane = lax.broadcasted_iota(jnp.int32, logits.shape, 1)
    logits = jnp.where(lane < N_EXPERTS, logits, NEG)
    e = jnp.exp(logits - logits.max(axis=-1, keepdims=True))
    aff_ref[0, rows, :] = e / e.sum(axis=-1, keepdims=True)


def _router_specs(b, n, d, tm):
    tile = lambda w: pl.BlockSpec((1, tm, w), lambda i, j: (i, j, 0))
    in_specs = [pl.BlockSpec((1, d), lambda i, j: (0, 0)), pl.BlockSpec((d, 2 * LANE), lambda i, j: (0, 0))]
    out_shape = [jax.ShapeDtypeStruct((b, n, d), F32), jax.ShapeDtypeStruct((b, n, d), BF16),
                 jax.ShapeDtypeStruct((b, n, LANE), F32)]
    return in_specs, [tile(d), tile(d), tile(LANE)], out_shape


def _silu(x):
    return x * jax.nn.sigmoid(x)


def _row_tile(n):
    return min(n, ROW_TILE)


def _mod_kernel(c_ref, w_ref, b_ref, o_ref):
    s = _silu(c_ref[...])
    o_ref[0] = jnp.dot(s, w_ref[0], precision=HIGHEST, preferred_element_type=F32) + b_ref[0]


def _modulation(cc, w_mod, b_mod):
    depth, d, d6 = w_mod.shape
    r = cc.shape[0]
    return pl.pallas_call(
        _mod_kernel, grid=(depth, d6 // d),
        in_specs=[pl.BlockSpec((r, d), lambda i, j: (0, 0)),
                  pl.BlockSpec((1, d, d), lambda i, j: (i, 0, j)),
                  pl.BlockSpec((1, 1, d), lambda i, j: (i, 0, j))],
        out_specs=pl.BlockSpec((1, r, d), lambda i, j: (i, 0, j)),
        out_shape=jax.ShapeDtypeStruct((depth, r, d6), F32),
        compiler_params=_cparams(2), name="modulation",
    )(cc, w_mod, b_mod.reshape(depth, 1, d6))


def _proj_in_kernel(x_ref, mod_ref, g_ref, w_ref, *rest, rope_cols, q_cols):
    o_ref = rest[-1]
    h = _normmod(x_ref[0], g_ref[...], mod_ref[0, 0:1, :], mod_ref[0, 1:2, :])
    y = _dot(h.astype(BF16), w_ref[...])
    if rope_cols:
        cos, s_up, s_dn = rest[0][...], rest[1][...], rest[2][...]
        for j in range(rope_cols // LANE):
            t = y[:, j * LANE:(j + 1) * LANE]
            t = t * cos + pltpu.roll(t, LANE - 16, 1) * s_up + pltpu.roll(t, 16, 1) * s_dn
            if j * LANE < q_cols:
                t = t * QUERY_SCALE
            o_ref[0, :, j * LANE:(j + 1) * LANE] = t.astype(o_ref.dtype)
        o_ref[0, :, rope_cols:] = y[:, rope_cols:].astype(o_ref.dtype)
    elif q_cols:
        o_ref[0, :, :q_cols] = (y[:, :q_cols] * QUERY_SCALE).astype(o_ref.dtype)
        o_ref[0, :, q_cols:] = y[:, q_cols:].astype(o_ref.dtype)
    else:
        o_ref[0] = y.astype(o_ref.dtype)


def _proj_in(x, mod, g, w, out_dtype, rope=None, rope_cols=0, q_cols=0):
    b, n, d = x.shape
    nout = w.shape[1]
    tm = _row_tile(n)
    in_specs = [pl.BlockSpec((1, tm, d), lambda i, j: (i, j, 0)),
                pl.BlockSpec((1, 6, d), lambda i, j: (i, 0, 0)),
                pl.BlockSpec((1, d), lambda i, j: (0, 0)),
                pl.BlockSpec((d, nout), lambda i, j: (0, 0))]
    args = [x, mod, g.reshape(1, d), w]
    if rope_cols:
        in_specs += [pl.BlockSpec((tm, LANE), lambda i, j: (j, 0))] * 3
        args += list(rope)
    return pl.pallas_call(
        functools.partial(_proj_in_kernel, rope_cols=rope_cols, q_cols=q_cols), grid=(b, n // tm),
        in_specs=in_specs,
        out_specs=pl.BlockSpec((1, tm, nout), lambda i, j: (i, j, 0)),
        out_shape=jax.ShapeDtypeStruct((b, n, nout), out_dtype),
        compiler_params=_cparams(2, 56), name="proj_in",
    )(*args)


def _rope_tables(n):
    t = jnp.arange(n)
    row = (t // GRID_W).astype(F32)
    col = (t % GRID_W).astype(F32)
    quarter = HEAD_DIM // 4
    inv = ROPE_BASE ** (-jnp.arange(quarter, dtype=F32) / quarter)
    lane = jnp.arange(LANE)
    pos = jnp.where(((lane % HEAD_DIM) // (HEAD_DIM // 2))[None, :] == 0, row[:, None], col[:, None])
    ang = pos * inv[lane % quarter][None, :]
    first = ((lane % (HEAD_DIM // 2)) < quarter)[None, :]
    cos, sin = jnp.cos(ang), jnp.sin(ang)
    return cos, jnp.where(first, -sin, 0.0), jnp.where(first, 0.0, sin)


def _proj_out_kernel(o_ref, x_ref, mod_ref, w_ref, g2_ref, wr_ref, out_ref, h_ref, aff_ref):
    for rows in _row_chunks(x_ref.shape[1]):
        x = x_ref[0, rows, :] + mod_ref[0, 2:3, :] * _dot(o_ref[0, rows, :], w_ref[...])
        _router_tail(x, rows, mod_ref, g2_ref, wr_ref, out_ref, h_ref, aff_ref)


def _proj_out(o, x, mod, w, g2, wr):
    b, n, d = x.shape
    tm = _row_tile(n)
    r_in, r_out, r_shape = _router_specs(b, n, d, tm)
    return pl.pallas_call(
        _proj_out_kernel, grid=(b, n // tm),
        in_specs=[pl.BlockSpec((1, tm, d), lambda i, j: (i, j, 0)),
                  pl.BlockSpec((1, tm, d), lambda i, j: (i, j, 0)),
                  pl.BlockSpec((1, 6, d), lambda i, j: (i, 0, 0)),
                  pl.BlockSpec((d, d), lambda i, j: (0, 0))] + r_in,
        out_specs=r_out, out_shape=r_shape,
        compiler_params=_cparams(2), name="proj_out",
    )(o, x, mod, w, g2.reshape(1, d), wr)


def _softmax_parts(scores):
    m = scores[0].max(axis=-1, keepdims=True)
    for s in scores[1:]:
        m = jnp.maximum(m, s.max(axis=-1, keepdims=True))
    es = [jnp.exp2(s - m) for s in scores]
    z = es[0].sum(axis=-1, keepdims=True)
    for e in es[1:]:
        z = z + e.sum(axis=-1, keepdims=True)
    return es, z


def _attn_kernel(*refs, diff, has_lat, post_scale):
    refs = list(refs)
    q_ref = refs.pop(0)
    kv = [(refs.pop(0), refs.pop(0))] if has_lat else []
    kv.append((refs.pop(0), refs.pop(0)))
    if diff:
        lam_ref, sub_ref = refs.pop(0), refs.pop(0)
    o_ref = refs.pop(0)
    tq = q_ref.shape[1]
    lo = lax.broadcasted_iota(jnp.int32, (1, LANE), 1) < HEAD_DIM
    zero = jnp.zeros((), BF16)
    for t in range(q_ref.shape[2] // LANE):
        sl = slice(t * LANE, (t + 1) * LANE)
        qt = q_ref[0, :, sl]
        q2 = jnp.concatenate([jnp.where(lo, qt, zero), jnp.where(lo, zero, qt)], axis=0)
        es, z = _softmax_parts([_dot_nt(q2, k_ref[0, :, sl]) for k_ref, _ in kv])
        acc = jnp.zeros((2 * tq, LANE), F32)
        for e, (_, v_ref) in zip(es, kv):
            acc = acc + _dot(e.astype(BF16), v_ref[0, :, sl])
        acc = acc / z
        if diff:
            o = acc[:tq] - lam_ref[0, 0] * acc[tq:]
            o = o * lax.rsqrt(jnp.mean(o * o, axis=-1, keepdims=True) + EPS) * sub_ref[...] * post_scale
        else:
            o = jnp.where(lo, acc[:tq], acc[tq:])
        o_ref[0, :, sl] = o.astype(o_ref.dtype)


def _attention(qkv_q, qkv_lat, qkv_ctx, d, diff, lam=None, subln=None, post_scale=1.0):
    b, nq, _ = qkv_q.shape
    tq = min(nq, ATTN_ROWS)
    has_lat = qkv_lat is not None
    in_specs = [pl.BlockSpec((1, tq, d), lambda i, j: (i, j, 0))]
    args = [qkv_q]
    for src in ([qkv_lat] if has_lat else []) + [qkv_ctx]:
        nk = src.shape[1]
        in_specs += [pl.BlockSpec((1, nk, d), lambda i, j: (i, 0, 1)),
                     pl.BlockSpec((1, nk, d), lambda i, j: (i, 0, 2))]
        args += [src, src]
    if diff:
        in_specs += [pl.BlockSpec(memory_space=pltpu.SMEM), pl.BlockSpec((1, LANE), lambda i, j: (0, 0))]
        args += [lam.reshape(1, 1), subln.reshape(1, LANE)]
    return pl.pallas_call(
        functools.partial(_attn_kernel, diff=diff, has_lat=has_lat, post_scale=post_scale),
        grid=(b, nq // tq), in_specs=in_specs,
        out_specs=pl.BlockSpec((1, tq, d), lambda i, j: (i, j, 0)),
        out_shape=jax.ShapeDtypeStruct((b, nq, d), BF16),
        compiler_params=_cparams(2, 56), name="attention",
    )(*args)


def _na_kernel(cls_ref, q_ref, k_ref, v_ref, kc_ref, vc_ref, bias_ref, o_ref, *, rows, wr, kr):
    del cls_ref
    kstart = jnp.clip(pl.program_id(1) * NA_ROWS - wr // 2, 0, rows - kr)
    start = pl.multiple_of(kstart * GRID_W, GRID_W)
    nk = kr * GRID_W
    tq = q_ref.shape[1]
    lo = lax.broadcasted_iota(jnp.int32, (1, LANE), 1) < HEAD_DIM
    zero = jnp.zeros((), BF16)
    for t in range(q_ref.shape[2] // LANE):
        sl = slice(t * LANE, (t + 1) * LANE)
        qt = q_ref[0, :, sl]
        kw, vw = k_ref[0, pl.ds(start, nk), sl], v_ref[0, pl.ds(start, nk), sl]
        kc, vc = kc_ref[0, :, sl], vc_ref[0, :, sl]
        q2 = jnp.concatenate([jnp.where(lo, qt, zero), jnp.where(lo, zero, qt)], axis=0)
        (el, ec), z = _softmax_parts([_dot_nt(q2, kw) + bias_ref[0, t], _dot_nt(q2, kc)])
        acc = (_dot(el.astype(BF16), vw) + _dot(ec.astype(BF16), vc)) / z
        o_ref[0, :, sl] = jnp.where(lo, acc[:tq], acc[tq:]).astype(o_ref.dtype)


def _na_layout(rows):
    wr = min(WIN_R, rows)
    kr = min(rows, wr + NA_ROWS - 1)
    classes, cls_of_block = [], []
    for blk in range(rows // NA_ROWS):
        kstart = min(max(blk * NA_ROWS - wr // 2, 0), rows - kr)
        key = []
        for qr in range(NA_ROWS):
            r = blk * NA_ROWS + qr
            r0 = min(max(r - wr // 2, 0), rows - wr)
            key.append(tuple((kstart + j - r + WIN_R - 1) if r0 <= kstart + j < r0 + wr else None for j in range(kr)))
        key = tuple(key)
        if key not in classes:
            classes.append(key)
        cls_of_block.append(classes.index(key))
    return wr, kr, classes, cls_of_block


def _na_bias_table(rpb, rows):
    heads = rpb.shape[0]
    _, _, classes, _ = _na_layout(rows)
    cq = jnp.arange(GRID_W)
    c0 = jnp.clip(cq - WIN_C // 2, 0, GRID_W - WIN_C)
    col_in = (cq[None, :] >= c0[:, None]) & (cq[None, :] < c0[:, None] + WIN_C)
    dc = jnp.clip(cq[None, :] - cq[:, None] + WIN_C - 1, 0, 2 * WIN_C - 2)
    dc_onehot = (dc[:, :, None] == jnp.arange(2 * WIN_C - 1)[None, None, :]).astype(F32)
    per_dr = jnp.einsum('hrd,qkd->hrqk', rpb.astype(F32) * math.log2(math.e), dc_onehot,
                        precision=HIGHEST)
    per_dr = jnp.where(col_in[None, None], per_dr, NEG)
    masked = jnp.full((heads, GRID_W, GRID_W), NEG, F32)
    tabs = []
    for key in classes:
        slabs = [jnp.concatenate([masked if dr is None else per_dr[:, dr] for dr in row], axis=-1) for row in key]
        tabs.append(jnp.concatenate(slabs, axis=1))
    tab = jnp.stack(tabs, axis=0)
    return tab.reshape(tab.shape[0], heads // 2, 2 * tab.shape[2], tab.shape[3])


def _na_attention(qkv_l, qkv_c, bias, d):
    b, n, _ = qkv_l.shape
    nc = qkv_c.shape[1]
    rows = n // GRID_W
    wr, kr, _, cls_of_block = _na_layout(rows)
    heads = d // HEAD_DIM
    tq = NA_ROWS * GRID_W
    grid_spec = pltpu.PrefetchScalarGridSpec(
        num_scalar_prefetch=1, grid=(b, rows // NA_ROWS),
        in_specs=[pl.BlockSpec((1, tq, d), lambda i, r, cls: (i, r, 0)),
                  pl.BlockSpec((1, n, d), lambda i, r, cls: (i, 0, 1)),
                  pl.BlockSpec((1, n, d), lambda i, r, cls: (i, 0, 2)),
                  pl.BlockSpec((1, nc, d), lambda i, r, cls: (i, 0, 1)),
                  pl.BlockSpec((1, nc, d), lambda i, r, cls: (i, 0, 2)),
                  pl.BlockSpec((1, heads // 2, 2 * tq, kr * GRID_W), lambda i, r, cls: (cls[r], 0, 0, 0),
                               pipeline_mode=pl.Buffered(1))],
        out_specs=pl.BlockSpec((1, tq, d), lambda i, r, cls: (i, r, 0)))
    return pl.pallas_call(
        functools.partial(_na_kernel, rows=rows, wr=wr, kr=kr),
        grid_spec=grid_spec,
        out_shape=jax.ShapeDtypeStruct((b, n, d), BF16),
        compiler_params=_cparams(2, 56), name="na_attention",
    )(jnp.asarray(cls_of_block, jnp.int32), qkv_l, qkv_l, qkv_l, qkv_c, qkv_c, bias)


def _normmod_kernel(x_ref, mod_ref, g_ref, o_ref):
    o_ref[0] = _normmod(x_ref[0], g_ref[...], mod_ref[0, 0:1, :], mod_ref[0, 1:2, :]).astype(o_ref.dtype)


def _normmod_call(x, mod, g):
    b, n, d = x.shape
    tm = _row_tile(n)
    return pl.pallas_call(
        _normmod_kernel, grid=(b, n // tm),
        in_specs=[pl.BlockSpec((1, tm, d), lambda i, j: (i, j, 0)),
                  pl.BlockSpec((1, 6, d), lambda i, j: (i, 0, 0)),
                  pl.BlockSpec((1, d), lambda i, j: (0, 0))],
        out_specs=pl.BlockSpec((1, tm, d), lambda i, j: (i, j, 0)),
        out_shape=jax.ShapeDtypeStruct((b, n, d), BF16),
        compiler_params=_cparams(2), name="normmod",
    )(x, mod, g.reshape(1, d))


def _s5_tables(a_re, a_im, log_dt, b_re, b_im, c_re, c_im):
    t_len = S5_CHUNK
    a_re, a_im = a_re.astype(F32), a_im.astype(F32)
    dt = jnp.exp(log_dt.astype(F32))[..., None]
    lr, li = a_re * dt, a_im * dt
    cos_li, sin_li = jnp.cos(li), jnp.sin(li)
    ab_im = jnp.exp(lr) * sin_li
    nr = jnp.expm1(lr) * cos_li - 2.0 * jnp.sin(0.5 * li) ** 2
    den = a_re * a_re + a_im * a_im
    fr = (nr * a_re + ab_im * a_im) / den
    fi = (ab_im * a_re - nr * a_im) / den
    b_re, b_im = b_re.astype(F32), b_im.astype(F32)
    bb_re = fr[..., None] * b_re - fi[..., None] * b_im
    bb_im = fr[..., None] * b_im + fi[..., None] * b_re
    tau = jnp.arange(t_len + 1, dtype=F32)[:, None, None, None]
    mag = jnp.exp(tau * lr)
    pr, pi = mag * jnp.cos(tau * li), mag * jnp.sin(tau * li)
    c_re, c_im = c_re.astype(F32), c_im.astype(F32)
    cw_re = c_re[None] * pr[:, :, :, None, :] - c_im[None] * pi[:, :, :, None, :]
    cw_im = c_re[None] * pi[:, :, :, None, :] + c_im[None] * pr[:, :, :, None, :]
    kern = (jnp.einsum('tdgkp,dgph->tdgkh', cw_re, bb_re, precision=HIGHEST)
            - jnp.einsum('tdgkp,dgph->tdgkh', cw_im, bb_im, precision=HIGHEST))
    g = a_re.shape[1]
    h = S5_GROUP
    diff = jnp.arange(t_len)[None, :] - jnp.arange(t_len)[:, None]
    lags = jnp.arange(t_len)[None, None, :]
    fwd_lag = (diff[:, :, None] == lags).astype(F32)
    bwd_lag = (-diff[:, :, None] == lags).astype(F32)
    k_f = jnp.einsum('sta,agkh->stgkh', fwd_lag, kern[:t_len, 0], precision=HIGHEST)
    k_b = jnp.einsum('sta,agkh->stgkh', bwd_lag, kern[:t_len, 1], precision=HIGHEST)
    m_both = (k_f + k_b).transpose(2, 0, 4, 1, 3).reshape(g, t_len * h, t_len * h)

    def in_map(d, p_re, p_im):
        wr_ = p_re[..., None] * bb_re[d][None] - p_im[..., None] * bb_im[d][None]
        wi_ = p_re[..., None] * bb_im[d][None] + p_im[..., None] * bb_re[d][None]
        to_rows = lambda w: w.transpose(1, 0, 3, 2).reshape(g, t_len * h, -1)
        return to_rows(wr_), to_rows(wi_)

    f_re, f_im = in_map(0, pr[:t_len, 0][::-1], pi[:t_len, 0][::-1])
    r_re, r_im = in_map(1, pr[:t_len, 1], pi[:t_len, 1])
    w_in = jnp.concatenate([f_re, f_im, f_im, f_re, r_re, r_im, r_im, r_re], axis=-1)

    def out_map(w_re, w_im):
        to_cols = lambda w: w.transpose(1, 3, 0, 2).reshape(g, -1, t_len * h)
        return jnp.concatenate([to_cols(w_re), -to_cols(w_im)], axis=1)

    w_out = jnp.concatenate([out_map(cw_re[1:, 0], cw_im[1:, 0]),
                             out_map(cw_re[1:, 1][::-1], cw_im[1:, 1][::-1])], axis=1)
    a_pow = jnp.stack([jnp.concatenate([pr[t_len, 0], pr[t_len, 0]], -1),
                       jnp.concatenate([-pi[t_len, 0], pi[t_len, 0]], -1),
                       jnp.concatenate([pr[t_len, 1], pr[t_len, 1]], -1),
                       jnp.concatenate([-pi[t_len, 1], pi[t_len, 1]], -1)], axis=1)
    return m_both.astype(BF16), w_in.astype(BF16), w_out.astype(BF16), a_pow


def _s5_core_kernel(uc_ref, ul_ref, m_ref, win_ref, wout_ref, a_ref, yc_ref, yl_ref, sall_ref, sin_ref,
                    *, bn, nc_c, nc_l):
    rc = nc_c * bn
    sin_ref[0:rc, :] = _dot(uc_ref[0], win_ref[0])
    sin_ref[rc:, :] = _dot(ul_ref[0], win_ref[0])
    a = a_ref[0]
    a1f, a2f, a1b, a2b = a[0:1], a[1:2], a[2:3], a[3:4]
    w = LANE
    zero = jnp.zeros((bn, w), F32)

    def rows_of(c):
        return pl.ds(pl.multiple_of(c * bn, bn), bn)

    def fwd(c, carry):
        s, sw = carry
        r = rows_of(c)
        sall_ref[r, 0:w] = s
        return (a1f * s + a2f * sw + sin_ref[r, 0:w], a1f * sw - a2f * s + sin_ref[r, w:2 * w])

    def bwd(c, carry):
        s, sw = carry
        r = rows_of(c)
        sall_ref[r, w:2 * w] = s
        return (a1b * s + a2b * sw + sin_ref[r, 2 * w:3 * w], a1b * sw - a2b * s + sin_ref[r, 3 * w:4 * w])

    lax.fori_loop(0, nc_c + nc_l, fwd, (zero, zero))
    carry = lax.fori_loop(0, nc_c, lambda i, cr: bwd(nc_c - 1 - i, cr), (zero, zero))
    lax.fori_loop(0, nc_l, lambda i, cr: bwd(nc_c + nc_l - 1 - i, cr), carry)
    yc_ref[0] = (_dot(uc_ref[0], m_ref[0]) + _dot(sall_ref[0:rc, :].astype(BF16), wout_ref[0])).astype(yc_ref.dtype)
    yl_ref[0] = (_dot(ul_ref[0], m_ref[0]) + _dot(sall_ref[rc:, :].astype(BF16), wout_ref[0])).astype(yl_ref.dtype)


def _s5_core(uc_rows, ul_rows, tables, bn):
    g, rc, width = uc_rows.shape
    rl = ul_rows.shape[1]
    m_both, w_in, w_out, a_pow = tables
    rows = lambda r: pl.BlockSpec((1, r, width), lambda i: (i, 0, 0))
    return pl.pallas_call(
        functools.partial(_s5_core_kernel, bn=bn, nc_c=rc // bn, nc_l=rl // bn), grid=(g,),
        in_specs=[rows(rc), rows(rl),
                  pl.BlockSpec((1, width, width), lambda i: (i, 0, 0)),
                  pl.BlockSpec((1, width, 4 * LANE), lambda i: (i, 0, 0)),
                  pl.BlockSpec((1, 2 * LANE, width), lambda i: (i, 0, 0)),
                  pl.BlockSpec((1, 4, LANE), lambda i: (i, 0, 0))],
        out_specs=[rows(rc), rows(rl)],
        out_shape=[jax.ShapeDtypeStruct((g, rc, width), BF16), jax.ShapeDtypeStruct((g, rl, width), BF16)],
        scratch_shapes=[pltpu.VMEM((rc + rl, 2 * LANE), F32), pltpu.VMEM((rc + rl, 4 * LANE), F32)],
        compiler_params=_cparams(1, 56), name="s5_core",
    )(uc_rows, ul_rows, m_both, w_in, w_out, a_pow)


def _s5_glu_kernel(x_ref, y_ref, mod_ref, g_ref, d_ref, w_ref, g2_ref, wr_ref, out_ref, h_ref, aff_ref):
    d = x_ref.shape[2]
    for rows in _row_chunks(x_ref.shape[1]):
        x = x_ref[0, rows, :]
        u = _normmod(x, g_ref[...], mod_ref[0, 0:1, :], mod_ref[0, 1:2, :])
        z = jax.nn.gelu(y_ref[0, rows, :] + d_ref[...] * u)
        zz = _dot(z.astype(BF16), w_ref[...])
        x = x + mod_ref[0, 2:3, :] * (zz[:, :d] * jax.nn.sigmoid(zz[:, d:]))
        _router_tail(x, rows, mod_ref, g2_ref, wr_ref, out_ref, h_ref, aff_ref)


def _s5_glu(x, y, mod, g, dskip, w_glu, g2, wr):
    b, n, d = x.shape
    tm = _row_tile(n)
    r_in, r_out, r_shape = _router_specs(b, n, d, tm)
    return pl.pallas_call(
        _s5_glu_kernel, grid=(b, n // tm),
        in_specs=[pl.BlockSpec((1, tm, d), lambda i, j: (i, j, 0)),
                  pl.BlockSpec((1, tm, d), lambda i, j: (i, j, 0)),
                  pl.BlockSpec((1, 6, d), lambda i, j: (i, 0, 0)),
                  pl.BlockSpec((1, d), lambda i, j: (0, 0)),
                  pl.BlockSpec((1, d), lambda i, j: (0, 0)),
                  pl.BlockSpec((d, 2 * d), lambda i, j: (0, 0))] + r_in,
        out_specs=r_out, out_shape=r_shape,
        compiler_params=_cparams(2), name="s5_glu",
    )(x, y, mod, g.reshape(1, d), dskip.reshape(1, d), w_glu, g2.reshape(1, d), wr)


def _s5_layer(x, h_ctx, mod_l, mod_c, g, tables, dskip, w_glu, g2, wr):
    b, n, d = x.shape
    n_c = h_ctx.shape[1]
    t_len, h = S5_CHUNK, S5_GROUP
    groups = d // h
    nc_l, nc_c = n // t_len, n_c // t_len

    def to_rows(u, nc):
        return u.reshape(b, nc, t_len, groups, h).transpose(3, 1, 0, 2, 4).reshape(groups, nc * b, t_len * h)

    def from_rows(y, nc):
        return y.reshape(groups, nc, b, t_len, h).transpose(2, 1, 3, 0, 4).reshape(b, nc * t_len, d)

    yc_rows, yl_rows = _s5_core(to_rows(_normmod_call(h_ctx, mod_c, g), nc_c),
                                to_rows(_normmod_call(x, mod_l, g), nc_l), tables, b)
    y_c, y_l = from_rows(yc_rows, nc_c), from_rows(yl_rows, nc_l)
    return (_s5_glu(h_ctx, y_c, mod_c, g, dskip, w_glu, g2, wr), _s5_glu(x, y_l, mod_l, g, dskip, w_glu, g2, wr))


def _hg_kernel(ql_ref, ffl_ref, fbl_ref, il_ref, ffc_ref, fbc_ref, ic_ref, lb_ref, wf_ref, wb_ref, mask_ref,
               o_ref, ob_ref, sf_ref, sb_ref, *, nc_c, nc_l):
    cn = HG_CHUNK
    levels = cn.bit_length() - 1
    small = min(levels, HG_SMALL_LEVELS)
    nh = HG_HEADS_PER_STEP
    lb = lb_ref[...]
    row = lax.broadcasted_iota(jnp.int32, (cn, 1), 0)

    def gates(ff):
        f = lb + (1.0 - lb) * jax.nn.sigmoid(ff)
        return jnp.log(f), 1.0 - f

    def dot_hi_lo(w01, x):
        hi, low = _split_hi_lo(x)
        return _dot(w01, hi) + _dot(w01, low)

    def visit(rows, q_ref, ff_ref, i_ref, w_ref, s_ref, out_ref, d_idx):
        fwd = d_idx == 0
        lf, k = gates(ff_ref[0, rows, :])
        v = _silu(i_ref[0, rows, :])
        n_blocks = 1 if out_ref is None else 1 + small
        sums = dot_hi_lo(w_ref[0:n_blocks * cn, :], lf)
        cum = sums[0:cn]
        total = cum[cn - 1:cn] if fwd else cum[0:1]
        kd = (k * jnp.exp(total - cum)).astype(BF16)
        decay = jnp.exp(total)
        vb = v.astype(BF16)
        if out_ref is not None:
            q = q_ref[0, rows, :]
            qd = (q * jnp.exp(cum)).astype(BF16)
            qk = q * k
            scaled = []
            for lv in range(levels):
                bs = cn >> (lv + 1)
                is_query = ((row // bs) % 2 == 1) if fwd else ((row // bs) % 2 == 0)
                if lv < levels - small:
                    pairs = cum.reshape(cn // (2 * bs), 2 * bs, cum.shape[1])
                    a_row = bs - 1 if fwd else bs
                    anchor = jnp.broadcast_to(pairs[:, a_row:a_row + 1, :], pairs.shape).reshape(cum.shape)
                    to_anchor = jnp.where(is_query, cum - anchor, anchor - cum)
                else:
                    blk = 1 + lv - (levels - small)
                    to_anchor = sums[blk * cn:(blk + 1) * cn]
                scaled.append((jnp.where(is_query, q, k) * jnp.exp(to_anchor)).astype(BF16))
            outs = []
            for hh in range(nh):
                sl = slice(hh * LANE, (hh + 1) * LANE)
                att = jnp.zeros((cn, cn), BF16)
                for lv in range(levels):
                    x = scaled[lv][:, sl]
                    att = att + _dot_nt(x, x).astype(BF16) * mask_ref[d_idx * levels + lv]
                same_token = jnp.sum(qk[:, sl], axis=-1, keepdims=True) * v[:, sl]
                outs.append(_dot_nt(qd[:, sl], s_ref[hh].astype(BF16)) + _dot(att, vb[:, sl]) + same_token)
            dst = out_ref.at[0] if len(out_ref.shape) == 3 else out_ref
            dst[rows, :] = jnp.concatenate(outs, axis=-1)
        for hh in range(nh):
            sl = slice(hh * LANE, (hh + 1) * LANE)
            s_ref[hh] = s_ref[hh] * decay[:, sl] + _dot(v[:, sl].T.astype(BF16), kd[:, sl])

    def rows_of(c):
        return pl.ds(pl.multiple_of(c * cn, cn), cn)

    sf_ref[...] = jnp.zeros(sf_ref.shape, F32)
    sb_ref[...] = jnp.zeros(sb_ref.shape, F32)

    def ctx_body(j, carry):
        visit(rows_of(j), None, ffc_ref, ic_ref, wf_ref, sf_ref, None, 0)
        visit(rows_of(nc_c - 1 - j), None, fbc_ref, ic_ref, wb_ref, sb_ref, None, 1)
        return carry

    def lat_body(j, carry):
        visit(rows_of(j), ql_ref, ffl_ref, il_ref, wf_ref, sf_ref, o_ref, 0)
        visit(rows_of(nc_l - 1 - j), ql_ref, fbl_ref, il_ref, wb_ref, sb_ref, ob_ref, 1)
        return carry

    lax.fori_loop(0, nc_c, ctx_body, 0)
    lax.fori_loop(0, nc_l, lat_body, 0)
    o_ref[0] = o_ref[0] + ob_ref[...]


def _hg_operators():
    cn = HG_CHUNK
    levels = cn.bit_length() - 1
    t = np.arange(cn)[:, None]
    r = np.arange(cn)[None, :]
    small = min(levels, HG_SMALL_LEVELS)
    w = np.zeros((2, 1 + small, cn, cn), np.float32)
    mask = np.zeros((2, levels, cn, cn), np.float32)
    w[0, 0] = r <= t
    w[1, 0] = r >= t
    for lv in range(levels):
        bs = cn >> (lv + 1)
        parent = t // (2 * bs) * (2 * bs)
        second = (t // bs) % 2 == 1
        blk = 1 + lv - (levels - small)
        if blk >= 1:
            anchor_f = parent + bs - 1
            w[0, blk] = np.where(second, (r > anchor_f) & (r <= t), (r > t) & (r <= anchor_f))
            anchor_b = parent + bs
            w[1, blk] = np.where(second, (r >= anchor_b) & (r < t), (r >= t) & (r < anchor_b))
        same_parent = (t // (2 * bs)) == (r // (2 * bs))
        key_second = (r // bs) % 2 == 1
        mask[0, lv] = same_parent & second & ~key_second
        mask[1, lv] = same_parent & ~second & key_second
    return (jnp.asarray(w.reshape(2, (1 + small) * cn, cn), BF16),
            jnp.asarray(mask.reshape(2 * levels, cn, cn), BF16))


def _hg_scan(z_l, z_c, lb, d):
    b, n, _ = z_l.shape
    n_c = z_c.shape[1]
    width = HG_HEADS_PER_STEP * LANE
    steps = d // width
    cn = HG_CHUNK
    w, mask = _hg_operators()
    col = lambda k: (lambda i, h: (i, 0, k * steps + h))
    lat = lambda k: pl.BlockSpec((1, n, width), col(k))
    ctx = lambda k: pl.BlockSpec((1, n_c, width), col(k))
    const = lambda a: pl.BlockSpec(a.shape, lambda i, h: (0,) * a.ndim)
    return pl.pallas_call(
        functools.partial(_hg_kernel, nc_c=n_c // cn, nc_l=n // cn), grid=(b, steps),
        in_specs=[lat(0), lat(1), lat(2), lat(3), ctx(1), ctx(2), ctx(3),
                  pl.BlockSpec((1, width), lambda i, h: (0, h)),
                  const(w[0]), const(w[1]), const(mask)],
        out_specs=pl.BlockSpec((1, n, width), lambda i, h: (i, 0, h)),
        out_shape=jax.ShapeDtypeStruct((b, n, d), F32),
        scratch_shapes=[pltpu.VMEM((n, width), F32),
                        pltpu.VMEM((HG_HEADS_PER_STEP, LANE, LANE), F32),
                        pltpu.VMEM((HG_HEADS_PER_STEP, LANE, LANE), F32)],
        compiler_params=_cparams(2), name="hgrn2_scan",
    )(z_l, z_l, z_l, z_l, z_c, z_c, z_c, lb.reshape(1, d), w[0], w[1], mask)


def _hg_out_kernel(o_ref, z_ref, gn_ref, x_ref, mod_ref, w_ref, g2_ref, wr_ref, out_ref, h_ref, aff_ref):
    for rows in _row_chunks(x_ref.shape[1]):
        o, gate = o_ref[0, rows, :], z_ref[0, rows, :]
        parts = []
        for h in range(o.shape[1] // LANE):
            sl = slice(h * LANE, (h + 1) * LANE)
            t = o[:, sl]
            t = t * lax.rsqrt(jnp.mean(t * t, axis=-1, keepdims=True) + EPS) * gn_ref[...]
            parts.append((t * _silu(gate[:, sl])).astype(BF16))
        y = _dot(jnp.concatenate(parts, axis=-1), w_ref[...])
        x = x_ref[0, rows, :] + mod_ref[0, 2:3, :] * y
        _router_tail(x, rows, mod_ref, g2_ref, wr_ref, out_ref, h_ref, aff_ref)


def _hg_out(o, z, gnorm, x, mod, w, g2, wr):
    b, n, d = x.shape
    tm = _row_tile(n)
    r_in, r_out, r_shape = _router_specs(b, n, d, tm)
    return pl.pallas_call(
        _hg_out_kernel, grid=(b, n // tm),
        in_specs=[pl.BlockSpec((1, tm, d), lambda i, j: (i, j, 0)),
                  pl.BlockSpec((1, tm, d), lambda i, j: (i, j, 4)),
                  pl.BlockSpec((1, LANE), lambda i, j: (0, 0)),
                  pl.BlockSpec((1, tm, d), lambda i, j: (i, j, 0)),
                  pl.BlockSpec((1, 6, d), lambda i, j: (i, 0, 0)),
                  pl.BlockSpec((d, d), lambda i, j: (0, 0))] + r_in,
        out_specs=r_out, out_shape=r_shape,
        compiler_params=_cparams(2), name="hgrn2_out",
    )(o, z, gnorm.reshape(1, LANE), x, mod, w, g2.reshape(1, d), wr)


def _route_kernel(aff_ref, tri_ref, pos_ref, selw_ref, start_ref, *, cap):
    groups = LANE // N_EXPERTS
    rg = aff_ref.shape[1] // groups
    samples = range(aff_ref.shape[0])
    denses = []
    for s in samples:
        dense = aff_ref[s, 0:rg, :]
        for g in range(1, groups):
            dense = dense + pltpu.roll(aff_ref[s, g * rg:(g + 1) * rg, :], g * N_EXPERTS, 1)
        denses.append(dense)
    all_bits = [lax.bitcast_convert_type(dense, jnp.int32) for dense in denses]
    lane = lax.broadcasted_iota(jnp.int32, (1, LANE), 1)
    tri = tri_ref[...]

    def indicator(mask):
        return jnp.where(mask, jnp.ones((), F32), jnp.zeros((), F32))

    def over_groups(row):
        for shift in (LANE // 2, LANE // 4, LANE // 8):
            row = row + pltpu.roll(row, shift, 1)
        return row

    def before_groups(row):
        out = jnp.zeros_like(row)
        for j in range(1, groups):
            out = out + jnp.where(lane >= j * N_EXPERTS, pltpu.roll(row, j * N_EXPERTS, 1), 0.0)
        return out

    def count(mask):
        return over_groups(jnp.sum(indicator(mask), axis=0, keepdims=True))

    def prefix(mask):
        x = indicator(mask)
        return _dot(tri, x.astype(BF16)) + before_groups(jnp.sum(x, axis=0, keepdims=True))

    def search(i, thrs):
        bit = jnp.left_shift(jnp.int32(1), 30 - i)
        return tuple(jnp.where(count(bits >= (thr | bit)) >= cap, thr | bit, thr) for bits, thr in zip(all_bits, thrs))

    thrs = lax.fori_loop(0, 31, search, tuple(jnp.zeros((1, LANE), jnp.int32) for _ in samples))
    for s, dense, bits, thr in zip(samples, denses, all_bits, thrs):
        above, tie = bits > thr, bits == thr
        need = cap - count(above)
        tie_rank = prefix(tie)
        sel = above | (tie & (tie_rank <= need))
        pos_ref[s] = jnp.where(sel, prefix(above) + jnp.minimum(tie_rank, need) - 1.0, -1.0)
        selw_ref[s] = jnp.where(sel, dense, 0.0)
        start_ref[s] = before_groups(jnp.sum(indicator(sel), axis=0, keepdims=True))


def _route(aff, cap):
    b, n, _ = aff.shape
    rg = n // (LANE // N_EXPERTS)
    idx = jnp.arange(rg)
    tri = (idx[None, :] <= idx[:, None]).astype(BF16)
    sb = math.gcd(b, ROUTE_SAMPLES)
    out_spec = pl.BlockSpec((sb, rg, LANE), lambda i: (i, 0, 0))
    return pl.pallas_call(
        functools.partial(_route_kernel, cap=cap), grid=(b // sb,),
        in_specs=[pl.BlockSpec((sb, n, LANE), lambda i: (i, 0, 0)), pl.BlockSpec((rg, rg), lambda i: (0, 0))],
        out_specs=[out_spec, out_spec, pl.BlockSpec((sb, 1, LANE), lambda i: (i, 0, 0))],
        out_shape=[jax.ShapeDtypeStruct((b, rg, LANE), F32)] * 2 + [jax.ShapeDtypeStruct((b, 1, LANE), F32)],
        compiler_params=_cparams(1), name="moe_route",
    )(aff, tri)


def _gather_kernel(pos_ref, h_ref, xs_ref, *, cap):
    h = h_ref[0]
    slot = lax.broadcasted_iota(jnp.int32, (cap, h.shape[0]), 0).astype(F32)
    for e in range(N_EXPERTS):
        onehot = jnp.where(pos_ref[0, e:e + 1, :] == slot, 1.0, 0.0).astype(BF16)
        xs_ref[e, 0] = _dot(onehot, h).astype(BF16)


def _gather(pos_rows, h, cap):
    b, n, d = h.shape
    return pl.pallas_call(
        functools.partial(_gather_kernel, cap=cap), grid=(b,),
        in_specs=[pl.BlockSpec((1, N_EXPERTS, n), lambda i: (i, 0, 0)),
                  pl.BlockSpec((1, n, d), lambda i: (i, 0, 0))],
        out_specs=pl.BlockSpec((N_EXPERTS, 1, cap, d), lambda i: (0, i, 0, 0)),
        out_shape=jax.ShapeDtypeStruct((N_EXPERTS, b, cap, d), BF16),
        compiler_params=_cparams(1), name="moe_gather",
    )(pos_rows, h)


def _ffn_kernel(xs_ref, wg_ref, wu_ref, wd_ref, o_ref):
    xs = xs_ref[0]
    hid = _silu(_dot(xs, wg_ref[0, 0])) * _dot(xs, wu_ref[0, 0])
    o_ref[0] = _dot(hid.astype(BF16), wd_ref[0, 0]).astype(BF16)


def _ffn(xs, wg, wu, wd, layer):
    e, m, d = xs.shape
    ff = wg.shape[3]
    tm = _row_tile(m)
    return pl.pallas_call(
        _ffn_kernel, grid=(e, m // tm),
        in_specs=[pl.BlockSpec((1, tm, d), lambda i, j: (i, j, 0)),
                  pl.BlockSpec((1, 1, d, ff), lambda i, j: (layer, i, 0, 0)),
                  pl.BlockSpec((1, 1, d, ff), lambda i, j: (layer, i, 0, 0)),
                  pl.BlockSpec((1, 1, ff, d), lambda i, j: (layer, i, 0, 0))],
        out_specs=pl.BlockSpec((1, tm, d), lambda i, j: (i, j, 0)),
        out_shape=jax.ShapeDtypeStruct((e, m, d), BF16),
        compiler_params=_cparams(2, 56), name="moe_ffn",
    )(xs, wg, wu, wd)


def _combine_kernel(*refs, cap, final, windowed):
    refs = list(refs)
    off_ref = refs.pop(0) if windowed else None
    out_ref, pos_ref, selw_ref, x_ref, mod_ref = refs[:5]
    o_ref = refs[-2] if windowed else refs[-1]
    x = x_ref[0]
    pos, selw = pos_ref[0], selw_ref[0]

    def all_slots():
        slot = lax.broadcasted_iota(jnp.int32, (x.shape[0], cap), 1).astype(F32)
        y = jnp.zeros(x.shape, F32)
        for e in range(N_EXPERTS):
            weighted = jnp.where(pos[:, e:e + 1] == slot, selw[:, e:e + 1], 0.0).astype(BF16)
            y = y + _dot(weighted, out_ref[e, 0])
        return y

    if windowed:
        y_ref = refs[-1]
        half = cap // 2
        i, g, last = pl.program_id(0), pl.program_id(1), pl.num_programs(1) - 1
        starts, fits = [], None
        for e in range(N_EXPERTS):
            lo = off_ref[i, g, e]
            hi = jnp.where(g < last, off_ref[i, jnp.minimum(g + 1, last), e], cap)
            w = jnp.minimum(lo // 16 * 16, cap - half)
            starts.append(w)
            ok = hi - w <= half
            fits = ok if fits is None else jnp.logical_and(fits, ok)

        @pl.when(fits)
        def _():
            pos_i = pos.astype(jnp.int32)
            slot = lax.broadcasted_iota(jnp.int32, (x.shape[0], half), 1)
            y = jnp.zeros(x.shape, F32)
            for e in range(0, N_EXPERTS, 2):
                lhs, rhs = [], []
                for k in (e, e + 1):
                    w = pl.multiple_of(starts[k], 16)
                    lhs.append(jnp.where(pos_i[:, k:k + 1] == slot + w, selw[:, k:k + 1], 0.0).astype(BF16))
                    rhs.append(out_ref[k, 0, pl.ds(w, half), :])
                y = y + _dot(jnp.concatenate(lhs, axis=1), jnp.concatenate(rhs, axis=0))
            y_ref[...] = y

        @pl.when(jnp.logical_not(fits))
        def _():
            y_ref[...] = all_slots()

        y = y_ref[...]
    else:
        y = all_slots()
    x = x + mod_ref[0, 5:6, :] * y
    if final:
        x = x * lax.rsqrt(jnp.mean(x * x, axis=-1, keepdims=True) + EPS) * refs[5][...]
    o_ref[0] = x


def _combine(out, pos, selw, starts, x, mod, cap, norm_f):
    b, n, d = x.shape
    groups = starts.shape[1]
    windowed = cap // 2 >= LANE and (n // groups) % 8 == 0
    tm = n // groups if windowed else _row_tile(n)
    final = norm_f is not None
    in_specs = [pl.BlockSpec((N_EXPERTS, 1, cap, d), lambda i, j, *_: (0, i, 0, 0)),
                pl.BlockSpec((1, tm, N_EXPERTS), lambda i, j, *_: (i, j, 0)),
                pl.BlockSpec((1, tm, N_EXPERTS), lambda i, j, *_: (i, j, 0)),
                pl.BlockSpec((1, tm, d), lambda i, j, *_: (i, j, 0)),
                pl.BlockSpec((1, 6, d), lambda i, j, *_: (i, 0, 0))]
    args = [out, pos, selw, x, mod]
    if final:
        in_specs.append(pl.BlockSpec((1, d), lambda i, j, *_: (0, 0)))
        args.append(norm_f.reshape(1, d))
    grid_spec = pltpu.PrefetchScalarGridSpec(
        num_scalar_prefetch=1 if windowed else 0, grid=(b, n // tm), in_specs=in_specs,
        out_specs=pl.BlockSpec((1, tm, d), lambda i, j, *_: (i, j, 0)),
        scratch_shapes=[pltpu.VMEM((tm, d), F32)] if windowed else [])
    return pl.pallas_call(
        functools.partial(_combine_kernel, cap=cap, final=final, windowed=windowed), grid_spec=grid_spec,
        out_shape=jax.ShapeDtypeStruct((b, n, d), F32),
        compiler_params=_cparams(2), name="moe_combine",
    )(*(([starts] if windowed else []) + args))


def _router_parts(w_router):
    w = jnp.pad(w_router.astype(F32), ((0, 0), (0, LANE - N_EXPERTS)))
    return jnp.concatenate(_split_hi_lo(w), axis=1)


def _moe(x, h, aff, mod, wg, wu, wd, layer, norm_f=None):
    b, n, d = x.shape
    cap = CAPACITY_FACTOR * n // N_EXPERTS
    groups = LANE // N_EXPERTS
    pos, selw, starts = _route(aff, cap)
    starts = starts.reshape(b, groups, N_EXPERTS).astype(jnp.int32)
    split = lambda a: a.reshape(b, n // groups, groups, N_EXPERTS)
    by_token = lambda a: split(a).transpose(0, 2, 1, 3).reshape(b, n, N_EXPERTS)
    by_expert = lambda a: split(a).transpose(0, 3, 2, 1).reshape(b, N_EXPERTS, n)
    xs = _gather(by_expert(pos), h, cap)
    out = _ffn(xs.reshape(N_EXPERTS, b * cap, d), wg, wu, wd, layer).reshape(N_EXPERTS, b, cap, d)
    return _combine(out, by_token(pos), by_token(selw), starts, x, mod, cap, norm_f)


def kernel(x, c, ctx, c_ctx, w_mod, b_mod, norm1, norm2, s5_a_re, s5_a_im, s5_log_dt, s5_b_re, s5_b_im, s5_c_re, s5_c_im, s5_d, s5_w_glu, na_w_qkv, na_w_o, na_rpb, da_w_qkv, da_w_o, da_lambda, da_subln, hg_w_in, hg_w_o, hg_gnorm, hg_lower_bounds, moe_router, moe_w_gate, moe_w_up, moe_w_down, norm_f):
    b, n, d = x.shape
    depth = w_mod.shape[0]
    assert depth == 4, "layer i uses mixer i; only the last layer drops the context stream"

    rows_pad = -(b + 1) % 8
    cc = jnp.concatenate([c, c_ctx[None, :], jnp.zeros((rows_pad, d), F32)], axis=0)
    mod_all = _modulation(cc, w_mod, b_mod)
    p_lb = jax.nn.softmax(hg_lower_bounds.astype(F32), axis=0)
    lbs = jnp.cumsum(p_lb, axis=0) - p_lb[0]
    wg, wu, wd = moe_w_gate.astype(BF16), moe_w_up.astype(BF16), moe_w_down.astype(BF16)
    h_ctx = ctx

    for i in range(depth):
        last = i == depth - 1
        mod_l = mod_all[i, :b].reshape(b, 6, d)
        mod_c = jnp.broadcast_to(mod_all[i, b].reshape(1, 6, d), (b, 6, d))
        g1, g2 = norm1[i], norm2[i]
        wr = _router_parts(moe_router[i])
        moe_c = None
        if i == 0:
            tables = _s5_tables(s5_a_re[0], s5_a_im[0], s5_log_dt[0], s5_b_re[0], s5_b_im[0], s5_c_re[0], s5_c_im[0])
            moe_c, moe_l = _s5_layer(x, h_ctx, mod_l, mod_c, g1, tables, s5_d[0], s5_w_glu[0].astype(BF16), g2, wr)
        elif i == 1:
            w_qkv = na_w_qkv[0].astype(BF16)
            qkv_l = _proj_in(x, mod_l, g1, w_qkv, BF16, q_cols=d)
            qkv_c = _proj_in(h_ctx, mod_c, g1, w_qkv, BF16, q_cols=d)
            bias = _na_bias_table(na_rpb[0], n // GRID_W)
            w_o = na_w_o[0].astype(BF16)
            moe_l = _proj_out(_na_attention(qkv_l, qkv_c, bias, d), x, mod_l, w_o, g2, wr)
            moe_c = _proj_out(_attention(qkv_c, None, qkv_c, d, diff=False), h_ctx, mod_c, w_o, g2, wr)
        elif i == 2:
            w_qkv = da_w_qkv[0].astype(BF16)
            qkv_l = _proj_in(x, mod_l, g1, w_qkv, BF16, rope=_rope_tables(n), rope_cols=2 * d, q_cols=d)
            qkv_c = _proj_in(h_ctx, mod_c, g1, w_qkv, BF16, q_cols=d)
            lam = da_lambda[0].astype(F32)
            lam_init = 0.8 - 0.6 * math.exp(-0.3 * i)
            lam_full = jnp.exp(jnp.sum(lam[0] * lam[1])) - jnp.exp(jnp.sum(lam[2] * lam[3])) + lam_init
            w_o = da_w_o[0].astype(BF16)
            attend = functools.partial(_attention, d=d, diff=True, lam=lam_full, subln=da_subln[0],
                                       post_scale=1.0 - lam_init)
            moe_l = _proj_out(attend(qkv_l, qkv_l, qkv_c), x, mod_l, w_o, g2, wr)
            moe_c = _proj_out(attend(qkv_c, None, qkv_c), h_ctx, mod_c, w_o, g2, wr)
        else:
            w_in = hg_w_in[0].astype(BF16)
            z_l = _proj_in(x, mod_l, g1, w_in, F32)
            z_c = _proj_in(h_ctx, mod_c, g1, w_in, F32)
            moe_l = _hg_out(_hg_scan(z_l, z_c, lbs[i], d), z_l, hg_gnorm[0], x, mod_l, hg_w_o[0].astype(BF16), g2, wr)

        x = _moe(*moe_l, mod_l, wg, wu, wd, i, norm_f if last else None)
        if not last:
            h_ctx = _moe(*moe_c, mod_c, wg, wu, wd, i)
    return x
```

```python
import functools
import math

import jax
import jax.numpy as jnp
import numpy as np
from jax import lax
from jax.experimental import pallas as pl
from jax.experimental.pallas import tpu as pltpu

F32 = jnp.float32
BF16 = jnp.bfloat16
HIGHEST = lax.Precision.HIGHEST
EPS = 1e-6
NEG = -1e30

LANE = 128
GRID_W = 64
WIN_R = 8
WIN_C = 16
ATTN_ROWS = 256
NA_ROWS = 4
HEAD_DIM = 64
QUERY_SCALE = HEAD_DIM ** -0.5 * math.log2(math.e)
ROPE_BASE = 10000.0
S5_GROUP = 16
S5_CHUNK = 16
HG_CHUNK = 256
HG_HEADS_PER_STEP = 2
HG_SMALL_LEVELS = 3
N_EXPERTS = 16
CAPACITY_FACTOR = 2
ROUTE_SAMPLES = 4
ROW_TILE = 512
EPILOGUE_ROWS = 256


def _cparams(n_axes, vmem_mb=48):
    return pltpu.CompilerParams(dimension_semantics=("arbitrary",) * n_axes,
                                vmem_limit_bytes=vmem_mb * 1024 * 1024)


def _dot(a, b):
    return jnp.dot(a, b, preferred_element_type=F32)


def _dot_nt(a, b):
    return lax.dot_general(a, b, (((1,), (1,)), ((), ())), preferred_element_type=F32)


def _normmod(x, g, shift, scale):
    y = x * lax.rsqrt(jnp.mean(x * x, axis=-1, keepdims=True) + EPS)
    return (y * g) * (1.0 + scale) + shift


def _split_hi_lo(x):
    bits = lax.bitcast_convert_type(x, jnp.int32)
    hi = lax.bitcast_convert_type((bits + jnp.int32(0x8000)) & jnp.int32(-65536), F32)
    return hi.astype(BF16), (x - hi).astype(BF16)


def _row_chunks(tm):
    step = min(tm, EPILOGUE_ROWS)
    return [slice(r, r + step) for r in range(0, tm, step)]


def _router_tail(x, rows, mod_ref, g2_ref, wr_ref, x_ref, h_ref, aff_ref):
    x_ref[0, rows, :] = x
    h = _normmod(x, g2_ref[...], mod_ref[0, 3:4, :], mod_ref[0, 4:5, :])
    h_hi, h_lo = _split_hi_lo(h)
    h_ref[0, rows, :] = h_hi
    both = _dot(h_hi, wr_ref[...]) + _dot(h_lo, wr_ref[...])
    logits = both[:, :LANE] + both[:, LANE:]
    lane = lax.broadcasted_iota(jnp.int32, logits.shape, 1)
    logits = jnp.where(lane < N_EXPERTS, logits, NEG)
    e = jnp.exp(logits - logits.max(axis=-1, keepdims=True))
    aff_ref[0, rows, :] = e / e.sum(axis=-1, keepdims=True)


def _router_specs(b, n, d, tm):
    tile = lambda w: pl.BlockSpec((1, tm, w), lambda i, j: (i, j, 0))
    in_specs = [pl.BlockSpec((1, d), lambda i, j: (0, 0)), pl.BlockSpec((d, 2 * LANE), lambda i, j: (0, 0))]
    out_shape = [jax.ShapeDtypeStruct((b, n, d), F32), jax.ShapeDtypeStruct((b, n, d), BF16),
                 jax.ShapeDtypeStruct((b, n, LANE), F32)]
    return in_specs, [tile(d), tile(d), tile(LANE)], out_shape


def _silu(x):
    return x * jax.nn.sigmoid(x)


def _row_tile(n):
    return min(n, ROW_TILE)


def _mod_kernel(c_ref, w_ref, b_ref, o_ref):
    s = _silu(c_ref[...])
    o_ref[0] = jnp.dot(s, w_ref[0], precision=HIGHEST, preferred_element_type=F32) + b_ref[0]


def _modulation(cc, w_mod, b_mod):
    depth, d, d6 = w_mod.shape
    r = cc.shape[0]
    return pl.pallas_call(
        _mod_kernel, grid=(depth, d6 // d),
        in_specs=[pl.BlockSpec((r, d), lambda i, j: (0, 0)),
                  pl.BlockSpec((1, d, d), lambda i, j: (i, 0, j)),
                  pl.BlockSpec((1, 1, d), lambda i, j: (i, 0, j))],
        out_specs=pl.BlockSpec((1, r, d), lambda i, j: (i, 0, j)),
        out_shape=jax.ShapeDtypeStruct((depth, r, d6), F32),
        compiler_params=_cparams(2), name="modulation",
    )(cc, w_mod, b_mod.reshape(depth, 1, d6))


def _proj_in_kernel(x_ref, mod_ref, g_ref, w_ref, *rest, rope_cols, q_cols):
    o_ref = rest[-1]
    h = _normmod(x_ref[0], g_ref[...], mod_ref[0, 0:1, :], mod_ref[0, 1:2, :])
    y = _dot(h.astype(BF16), w_ref[...])
    if rope_cols:
        cos, s_up, s_dn = rest[0][...], rest[1][...], rest[2][...]
        for j in range(rope_cols // LANE):
            t = y[:, j * LANE:(j + 1) * LANE]
            t = t * cos + pltpu.roll(t, LANE - 16, 1) * s_up + pltpu.roll(t, 16, 1) * s_dn
            if j * LANE < q_cols:
                t = t * QUERY_SCALE
            o_ref[0, :, j * LANE:(j + 1) * LANE] = t.astype(o_ref.dtype)
        o_ref[0, :, rope_cols:] = y[:, rope_cols:].astype(o_ref.dtype)
    elif q_cols:
        o_ref[0, :, :q_cols] = (y[:, :q_cols] * QUERY_SCALE).astype(o_ref.dtype)
        o_ref[0, :, q_cols:] = y[:, q_cols:].astype(o_ref.dtype)
    else:
        o_ref[0] = y.astype(o_ref.dtype)


def _proj_in(x, mod, g, w, out_dtype, rope=None, rope_cols=0, q_cols=0):
    b, n, d = x.shape
    nout = w.shape[1]
    tm = _row_tile(n)
    in_specs = [pl.BlockSpec((1, tm, d), lambda i, j: (i, j, 0)),
                pl.BlockSpec((1, 6, d), lambda i, j: (i, 0, 0)),
                pl.BlockSpec((1, d), lambda i, j: (0, 0)),
                pl.BlockSpec((d, nout), lambda i, j: (0, 0))]
    args = [x, mod, g.reshape(1, d), w]
    if rope_cols:
        in_specs += [pl.BlockSpec((tm, LANE), lambda i, j: (j, 0))] * 3
        args += list(rope)
    return pl.pallas_call(
        functools.partial(_proj_in_kernel, rope_cols=rope_cols, q_cols=q_cols), grid=(b, n // tm),
        in_specs=in_specs,
        out_specs=pl.BlockSpec((1, tm, nout), lambda i, j: (i, j, 0)),
        out_shape=jax.ShapeDtypeStruct((b, n, nout), out_dtype),
        compiler_params=_cparams(2, 56), name="proj_in",
    )(*args)


def _rope_tables(n):
    t = jnp.arange(n)
    row = (t // GRID_W).astype(F32)
    col = (t % GRID_W).astype(F32)
    quarter = HEAD_DIM // 4
    inv = ROPE_BASE ** (-jnp.arange(quarter, dtype=F32) / quarter)
    lane = jnp.arange(LANE)
    pos = jnp.where(((lane % HEAD_DIM) // (HEAD_DIM // 2))[None, :] == 0, row[:, None], col[:, None])
    ang = pos * inv[lane % quarter][None, :]
    first = ((lane % (HEAD_DIM // 2)) < quarter)[None, :]
    cos, sin = jnp.cos(ang), jnp.sin(ang)
    return cos, jnp.where(first, -sin, 0.0), jnp.where(first, 0.0, sin)


def _proj_out_kernel(o_ref, x_ref, mod_ref, w_ref, g2_ref, wr_ref, out_ref, h_ref, aff_ref):
    for rows in _row_chunks(x_ref.shape[1]):
        x = x_ref[0, rows, :] + mod_ref[0, 2:3, :] * _dot(o_ref[0, rows, :], w_ref[...])
        _router_tail(x, rows, mod_ref, g2_ref, wr_ref, out_ref, h_ref, aff_ref)


def _proj_out(o, x, mod, w, g2, wr):
    b, n, d = x.shape
    tm = _row_tile(n)
    r_in, r_out, r_shape = _router_specs(b, n, d, tm)
    return pl.pallas_call(
        _proj_out_kernel, grid=(b, n // tm),
        in_specs=[pl.BlockSpec((1, tm, d), lambda i, j: (i, j, 0)),
                  pl.BlockSpec((1, tm, d), lambda i, j: (i, j, 0)),
                  pl.BlockSpec((1, 6, d), lambda i, j: (i, 0, 0)),
                  pl.BlockSpec((d, d), lambda i, j: (0, 0))] + r_in,
        out_specs=r_out, out_shape=r_shape,
        compiler_params=_cparams(2), name="proj_out",
    )(o, x, mod, w, g2.reshape(1, d), wr)


def _softmax_parts(scores):
    m = scores[0].max(axis=-1, keepdims=True)
    for s in scores[1:]:
        m = jnp.maximum(m, s.max(axis=-1, keepdims=True))
    es = [jnp.exp2(s - m) for s in scores]
    z = es[0].sum(axis=-1, keepdims=True)
    for e in es[1:]:
        z = z + e.sum(axis=-1, keepdims=True)
    return es, z


def _attn_kernel(*refs, diff, has_lat, post_scale):
    refs = list(refs)
    q_ref = refs.pop(0)
    kv = [(refs.pop(0), refs.pop(0))] if has_lat else []
    kv.append((refs.pop(0), refs.pop(0)))
    if diff:
        lam_ref, sub_ref = refs.pop(0), refs.pop(0)
    o_ref = refs.pop(0)
    tq = q_ref.shape[1]
    lo = lax.broadcasted_iota(jnp.int32, (1, LANE), 1) < HEAD_DIM
    zero = jnp.zeros((), BF16)
    for t in range(q_ref.shape[2] // LANE):
        sl = slice(t * LANE, (t + 1) * LANE)
        qt = q_ref[0, :, sl]
        q2 = jnp.concatenate([jnp.where(lo, qt, zero), jnp.where(lo, zero, qt)], axis=0)
        es, z = _softmax_parts([_dot_nt(q2, k_ref[0, :, sl]) for k_ref, _ in kv])
        acc = jnp.zeros((2 * tq, LANE), F32)
        for e, (_, v_ref) in zip(es, kv):
            acc = acc + _dot(e.astype(BF16), v_ref[0, :, sl])
        acc = acc / z
        if diff:
            o = acc[:tq] - lam_ref[0, 0] * acc[tq:]
            o = o * lax.rsqrt(jnp.mean(o * o, axis=-1, keepdims=True) + EPS) * sub_ref[...] * post_scale
        else:
            o = jnp.where(lo, acc[:tq], acc[tq:])
        o_ref[0, :, sl] = o.astype(o_ref.dtype)


def _attention(qkv_q, qkv_lat, qkv_ctx, d, diff, lam=None, subln=None, post_scale=1.0):
    b, nq, _ = qkv_q.shape
    tq = min(nq, ATTN_ROWS)
    has_lat = qkv_lat is not None
    in_specs = [pl.BlockSpec((1, tq, d), lambda i, j: (i, j, 0))]
    args = [qkv_q]
    for src in ([qkv_lat] if has_lat else []) + [qkv_ctx]:
        nk = src.shape[1]
        in_specs += [pl.BlockSpec((1, nk, d), lambda i, j: (i, 0, 1)),
                     pl.BlockSpec((1, nk, d), lambda i, j: (i, 0, 2))]
        args += [src, src]
    if diff:
        in_specs += [pl.BlockSpec(memory_space=pltpu.SMEM), pl.BlockSpec((1, LANE), lambda i, j: (0, 0))]
        args += [lam.reshape(1, 1), subln.reshape(1, LANE)]
    return pl.pallas_call(
        functools.partial(_attn_kernel, diff=diff, has_lat=has_lat, post_scale=post_scale),
        grid=(b, nq // tq), in_specs=in_specs,
        out_specs=pl.BlockSpec((1, tq, d), lambda i, j: (i, j, 0)),
        out_shape=jax.ShapeDtypeStruct((b, nq, d), BF16),
        compiler_params=_cparams(2, 56), name="attention",
    )(*args)


def _na_kernel(cls_ref, q_ref, k_ref, v_ref, kc_ref, vc_ref, bias_ref, o_ref, *, rows, wr, kr):
    del cls_ref
    kstart = jnp.clip(pl.program_id(1) * NA_ROWS - wr // 2, 0, rows - kr)
    start = pl.multiple_of(kstart * GRID_W, GRID_W)
    nk = kr * GRID_W
    tq = q_ref.shape[1]
    lo = lax.broadcasted_iota(jnp.int32, (1, LANE), 1) < HEAD_DIM
    zero = jnp.zeros((), BF16)
    for t in range(q_ref.shape[2] // LANE):
        sl = slice(t * LANE, (t + 1) * LANE)
        qt = q_ref[0, :, sl]
        kw, vw = k_ref[0, pl.ds(start, nk), sl], v_ref[0, pl.ds(start, nk), sl]
        kc, vc = kc_ref[0, :, sl], vc_ref[0, :, sl]
        q2 = jnp.concatenate([jnp.where(lo, qt, zero), jnp.where(lo, zero, qt)], axis=0)
        (el, ec), z = _softmax_parts([_dot_nt(q2, kw) + bias_ref[0, t], _dot_nt(q2, kc)])
        acc = (_dot(el.astype(BF16), vw) + _dot(ec.astype(BF16), vc)) / z
        o_ref[0, :, sl] = jnp.where(lo, acc[:tq], acc[tq:]).astype(o_ref.dtype)


def _na_layout(rows):
    wr = min(WIN_R, rows)
    kr = min(rows, wr + NA_ROWS - 1)
    classes, cls_of_block = [], []
    for blk in range(rows // NA_ROWS):
        kstart = min(max(blk * NA_ROWS - wr // 2, 0), rows - kr)
        key = []
        for qr in range(NA_ROWS):
            r = blk * NA_ROWS + qr
            r0 = min(max(r - wr // 2, 0), rows - wr)
            key.append(tuple((kstart + j - r + WIN_R - 1) if r0 <= kstart + j < r0 + wr else None for j in range(kr)))
        key = tuple(key)
        if key not in classes:
            classes.append(key)
        cls_of_block.append(classes.index(key))
    return wr, kr, classes, cls_of_block


def _na_bias_table(rpb, rows):
    heads = rpb.shape[0]
    _, _, classes, _ = _na_layout(rows)
    cq = jnp.arange(GRID_W)
    c0 = jnp.clip(cq - WIN_C // 2, 0, GRID_W - WIN_C)
    col_in = (cq[None, :] >= c0[:, None]) & (cq[None, :] < c0[:, None] + WIN_C)
    dc = jnp.clip(cq[None, :] - cq[:, None] + WIN_C - 1, 0, 2 * WIN_C - 2)
    dc_onehot = (dc[:, :, None] == jnp.arange(2 * WIN_C - 1)[None, None, :]).astype(F32)
    per_dr = jnp.einsum('hrd,qkd->hrqk', rpb.astype(F32) * math.log2(math.e), dc_onehot,
                        precision=HIGHEST)
    per_dr = jnp.where(col_in[None, None], per_dr, NEG)
    masked = jnp.full((heads, GRID_W, GRID_W), NEG, F32)
    tabs = []
    for key in classes:
        slabs = [jnp.concatenate([masked if dr is None else per_dr[:, dr] for dr in row], axis=-1) for row in key]
        tabs.append(jnp.concatenate(slabs, axis=1))
    tab = jnp.stack(tabs, axis=0)
    return tab.reshape(tab.shape[0], heads // 2, 2 * tab.shape[2], tab.shape[3])


def _na_attention(qkv_l, qkv_c, bias, d):
    b, n, _ = qkv_l.shape
    nc = qkv_c.shape[1]
    rows = n // GRID_W
    wr, kr, _, cls_of_block = _na_layout(rows)
    heads = d // HEAD_DIM
    tq = NA_ROWS * GRID_W
    grid_spec = pltpu.PrefetchScalarGridSpec(
        num_scalar_prefetch=1, grid=(b, rows // NA_ROWS),
        in_specs=[pl.BlockSpec((1, tq, d), lambda i, r, cls: (i, r, 0)),
                  pl.BlockSpec((1, n, d), lambda i, r, cls: (i, 0, 1)),
                  pl.BlockSpec((1, n, d), lambda i, r, cls: (i, 0, 2)),
                  pl.BlockSpec((1, nc, d), lambda i, r, cls: (i, 0, 1)),
                  pl.BlockSpec((1, nc, d), lambda i, r, cls: (i, 0, 2)),
                  pl.BlockSpec((1, heads // 2, 2 * tq, kr * GRID_W), lambda i, r, cls: (cls[r], 0, 0, 0),
                               pipeline_mode=pl.Buffered(1))],
        out_specs=pl.BlockSpec((1, tq, d), lambda i, r, cls: (i, r, 0)))
    return pl.pallas_call(
        functools.partial(_na_kernel, rows=rows, wr=wr, kr=kr),
        grid_spec=grid_spec,
        out_shape=jax.ShapeDtypeStruct((b, n, d), BF16),
        compiler_params=_cparams(2, 56), name="na_attention",
    )(jnp.asarray(cls_of_block, jnp.int32), qkv_l, qkv_l, qkv_l, qkv_c, qkv_c, bias)


def _normmod_kernel(x_ref, mod_ref, g_ref, o_ref):
    o_ref[0] = _normmod(x_ref[0], g_ref[...], mod_ref[0, 0:1, :], mod_ref[0, 1:2, :]).astype(o_ref.dtype)


def _normmod_call(x, mod, g):
    b, n, d = x.shape
    tm = _row_tile(n)
    return pl.pallas_call(
        _normmod_kernel, grid=(b, n // tm),
        in_specs=[pl.BlockSpec((1, tm, d), lambda i, j: (i, j, 0)),
                  pl.BlockSpec((1, 6, d), lambda i, j: (i, 0, 0)),
                  pl.BlockSpec((1, d), lambda i, j: (0, 0))],
        out_specs=pl.BlockSpec((1, tm, d), lambda i, j: (i, j, 0)),
        out_shape=jax.ShapeDtypeStruct((b, n, d), BF16),
        compiler_params=_cparams(2), name="normmod",
    )(x, mod, g.reshape(1, d))


def _s5_tables(a_re, a_im, log_dt, b_re, b_im, c_re, c_im):
    t_len = S5_CHUNK
    a_re, a_im = a_re.astype(F32), a_im.astype(F32)
    dt = jnp.exp(log_dt.astype(F32))[..., None]
    lr, li = a_re * dt, a_im * dt
    cos_li, sin_li = jnp.cos(li), jnp.sin(li)
    ab_im = jnp.exp(lr) * sin_li
    nr = jnp.expm1(lr) * cos_li - 2.0 * jnp.sin(0.5 * li) ** 2
    den = a_re * a_re + a_im * a_im
    fr = (nr * a_re + ab_im * a_im) / den
    fi = (ab_im * a_re - nr * a_im) / den
    b_re, b_im = b_re.astype(F32), b_im.astype(F32)
    bb_re = fr[..., None] * b_re - fi[..., None] * b_im
    bb_im = fr[..., None] * b_im + fi[..., None] * b_re
    tau = jnp.arange(t_len + 1, dtype=F32)[:, None, None, None]
    mag = jnp.exp(tau * lr)
    pr, pi = mag * jnp.cos(tau * li), mag * jnp.sin(tau * li)
    c_re, c_im = c_re.astype(F32), c_im.astype(F32)
    cw_re = c_re[None] * pr[:, :, :, None, :] - c_im[None] * pi[:, :, :, None, :]
    cw_im = c_re[None] * pi[:, :, :, None, :] + c_im[None] * pr[:, :, :, None, :]
    kern = (jnp.einsum('tdgkp,dgph->tdgkh', cw_re, bb_re, precision=HIGHEST)
            - jnp.einsum('tdgkp,dgph->tdgkh', cw_im, bb_im, precision=HIGHEST))
    g = a_re.shape[1]
    h = S5_GROUP
    diff = jnp.arange(t_len)[None, :] - jnp.arange(t_len)[:, None]
    lags = jnp.arange(t_len)[None, None, :]
    fwd_lag = (diff[:, :, None] == lags).astype(F32)
    bwd_lag = (-diff[:, :, None] == lags).astype(F32)
    k_f = jnp.einsum('sta,agkh->stgkh', fwd_lag, kern[:t_len, 0], precision=HIGHEST)
    k_b = jnp.einsum('sta,agkh->stgkh', bwd_lag, kern[:t_len, 1], precision=HIGHEST)
    m_both = (k_f + k_b).transpose(2, 0, 4, 1, 3).reshape(g, t_len * h, t_len * h)

    def in_map(d, p_re, p_im):
        wr_ = p_re[..., None] * bb_re[d][None] - p_im[..., None] * bb_im[d][None]
        wi_ = p_re[..., None] * bb_im[d][None] + p_im[..., None] * bb_re[d][None]
        to_rows = lambda w: w.transpose(1, 0, 3, 2).reshape(g, t_len * h, -1)
        return to_rows(wr_), to_rows(wi_)

    f_re, f_im = in_map(0, pr[:t_len, 0][::-1], pi[:t_len, 0][::-1])
    r_re, r_im = in_map(1, pr[:t_len, 1], pi[:t_len, 1])
    w_in = jnp.concatenate([f_re, f_im, f_im, f_re, r_re, r_im, r_im, r_re], axis=-1)

    def out_map(w_re, w_im):
        to_cols = lambda w: w.transpose(1, 3, 0, 2).reshape(g, -1, t_len * h)
        return jnp.concatenate([to_cols(w_re), -to_cols(w_im)], axis=1)

    w_out = jnp.concatenate([out_map(cw_re[1:, 0], cw_im[1:, 0]),
                             out_map(cw_re[1:, 1][::-1], cw_im[1:, 1][::-1])], axis=1)
    a_pow = jnp.stack([jnp.concatenate([pr[t_len, 0], pr[t_len, 0]], -1),
                       jnp.concatenate([-pi[t_len, 0], pi[t_len, 0]], -1),
                       jnp.concatenate([pr[t_len, 1], pr[t_len, 1]], -1),
                       jnp.concatenate([-pi[t_len, 1], pi[t_len, 1]], -1)], axis=1)
    return m_both.astype(BF16), w_in.astype(BF16), w_out.astype(BF16), a_pow


def _s5_core_kernel(uc_ref, ul_ref, m_ref, win_ref, wout_ref, a_ref, yc_ref, yl_ref, sall_ref, sin_ref,
                    *, bn, nc_c, nc_l):
    rc = nc_c * bn
    sin_ref[0:rc, :] = _dot(uc_ref[0], win_ref[0])
    sin_ref[rc:, :] = _dot(ul_ref[0], win_ref[0])
    a = a_ref[0]
    a1f, a2f, a1b, a2b = a[0:1], a[1:2], a[2:3], a[3:4]
    w = LANE
    zero = jnp.zeros((bn, w), F32)

    def rows_of(c):
        return pl.ds(pl.multiple_of(c * bn, bn), bn)

    def fwd(c, carry):
        s, sw = carry
        r = rows_of(c)
        sall_ref[r, 0:w] = s
        return (a1f * s + a2f * sw + sin_ref[r, 0:w], a1f * sw - a2f * s + sin_ref[r, w:2 * w])

    def bwd(c, carry):
        s, sw = carry
        r = rows_of(c)
        sall_ref[r, w:2 * w] = s
        return (a1b * s + a2b * sw + sin_ref[r, 2 * w:3 * w], a1b * sw - a2b * s + sin_ref[r, 3 * w:4 * w])

    lax.fori_loop(0, nc_c + nc_l, fwd, (zero, zero))
    carry = lax.fori_loop(0, nc_c, lambda i, cr: bwd(nc_c - 1 - i, cr), (zero, zero))
    lax.fori_loop(0, nc_l, lambda i, cr: bwd(nc_c + nc_l - 1 - i, cr), carry)
    yc_ref[0] = (_dot(uc_ref[0], m_ref[0]) + _dot(sall_ref[0:rc, :].astype(BF16), wout_ref[0])).astype(yc_ref.dtype)
    yl_ref[0] = (_dot(ul_ref[0], m_ref[0]) + _dot(sall_ref[rc:, :].astype(BF16), wout_ref[0])).astype(yl_ref.dtype)


def _s5_core(uc_rows, ul_rows, tables, bn):
    g, rc, width = uc_rows.shape
    rl = ul_rows.shape[1]
    m_both, w_in, w_out, a_pow = tables
    rows = lambda r: pl.BlockSpec((1, r, width), lambda i: (i, 0, 0))
    return pl.pallas_call(
        functools.partial(_s5_core_kernel, bn=bn, nc_c=rc // bn, nc_l=rl // bn), grid=(g,),
        in_specs=[rows(rc), rows(rl),
                  pl.BlockSpec((1, width, width), lambda i: (i, 0, 0)),
                  pl.BlockSpec((1, width, 4 * LANE), lambda i: (i, 0, 0)),
                  pl.BlockSpec((1, 2 * LANE, width), lambda i: (i, 0, 0)),
                  pl.BlockSpec((1, 4, LANE), lambda i: (i, 0, 0))],
        out_specs=[rows(rc), rows(rl)],
        out_shape=[jax.ShapeDtypeStruct((g, rc, width), BF16), jax.ShapeDtypeStruct((g, rl, width), BF16)],
        scratch_shapes=[pltpu.VMEM((rc + rl, 2 * LANE), F32), pltpu.VMEM((rc + rl, 4 * LANE), F32)],
        compiler_params=_cparams(1, 56), name="s5_core",
    )(uc_rows, ul_rows, m_both, w_in, w_out, a_pow)


def _s5_glu_kernel(x_ref, y_ref, mod_ref, g_ref, d_ref, w_ref, g2_ref, wr_ref, out_ref, h_ref, aff_ref):
    d = x_ref.shape[2]
    for rows in _row_chunks(x_ref.shape[1]):
        x = x_ref[0, rows, :]
        u = _normmod(x, g_ref[...], mod_ref[0, 0:1, :], mod_ref[0, 1:2, :])
        z = jax.nn.gelu(y_ref[0, rows, :] + d_ref[...] * u)
        zz = _dot(z.astype(BF16), w_ref[...])
        x = x + mod_ref[0, 2:3, :] * (zz[:, :d] * jax.nn.sigmoid(zz[:, d:]))
        _router_tail(x, rows, mod_ref, g2_ref, wr_ref, out_ref, h_ref, aff_ref)


def _s5_glu(x, y, mod, g, dskip, w_glu, g2, wr):
    b, n, d = x.shape
    tm = _row_tile(n)
    r_in, r_out, r_shape = _router_specs(b, n, d, tm)
    return pl.pallas_call(
        _s5_glu_kernel, grid=(b, n // tm),
        in_specs=[pl.BlockSpec((1, tm, d), lambda i, j: (i, j, 0)),
                  pl.BlockSpec((1, tm, d), lambda i, j: (i, j, 0)),
                  pl.BlockSpec((1, 6, d), lambda i, j: (i, 0, 0)),
                  pl.BlockSpec((1, d), lambda i, j: (0, 0)),
                  pl.BlockSpec((1, d), lambda i, j: (0, 0)),
                  pl.BlockSpec((d, 2 * d), lambda i, j: (0, 0))] + r_in,
        out_specs=r_out, out_shape=r_shape,
        compiler_params=_cparams(2), name="s5_glu",
    )(x, y, mod, g.reshape(1, d), dskip.reshape(1, d), w_glu, g2.reshape(1, d), wr)


def _s5_layer(x, h_ctx, mod_l, mod_c, g, tables, dskip, w_glu, g2, wr):
    b, n, d = x.shape
    n_c = h_ctx.shape[1]
    t_len, h = S5_CHUNK, S5_GROUP
    groups = d // h
    nc_l, nc_c = n // t_len, n_c // t_len

    def to_rows(u, nc):
        return u.reshape(b, nc, t_len, groups, h).transpose(3, 1, 0, 2, 4).reshape(groups, nc * b, t_len * h)

    def from_rows(y, nc):
        return y.reshape(groups, nc, b, t_len, h).transpose(2, 1, 3, 0, 4).reshape(b, nc * t_len, d)

    yc_rows, yl_rows = _s5_core(to_rows(_normmod_call(h_ctx, mod_c, g), nc_c),
                                to_rows(_normmod_call(x, mod_l, g), nc_l), tables, b)
    y_c, y_l = from_rows(yc_rows, nc_c), from_rows(yl_rows, nc_l)
    return (_s5_glu(h_ctx, y_c, mod_c, g, dskip, w_glu, g2, wr), _s5_glu(x, y_l, mod_l, g, dskip, w_glu, g2, wr))


def _hg_kernel(ql_ref, ffl_ref, fbl_ref, il_ref, ffc_ref, fbc_ref, ic_ref, lb_ref, wf_ref, wb_ref, mask_ref,
               o_ref, ob_ref, sf_ref, sb_ref, *, nc_c, nc_l):
    cn = HG_CHUNK
    levels = cn.bit_length() - 1
    small = min(levels, HG_SMALL_LEVELS)
    nh = HG_HEADS_PER_STEP
    lb = lb_ref[...]
    row = lax.broadcasted_iota(jnp.int32, (cn, 1), 0)

    def gates(ff):
        f = lb + (1.0 - lb) * jax.nn.sigmoid(ff)
        return jnp.log(f), 1.0 - f

    def dot_hi_lo(w01, x):
        hi, low = _split_hi_lo(x)
        return _dot(w01, hi) + _dot(w01, low)

    def visit(rows, q_ref, ff_ref, i_ref, w_ref, s_ref, out_ref, d_idx):
        fwd = d_idx == 0
        lf, k = gates(ff_ref[0, rows, :])
        v = _silu(i_ref[0, rows, :])
        n_blocks = 1 if out_ref is None else 1 + small
        sums = dot_hi_lo(w_ref[0:n_blocks * cn, :], lf)
        cum = sums[0:cn]
        total = cum[cn - 1:cn] if fwd else cum[0:1]
        kd = (k * jnp.exp(total - cum)).astype(BF16)
        decay = jnp.exp(total)
        vb = v.astype(BF16)
        if out_ref is not None:
            q = q_ref[0, rows, :]
            qd = (q * jnp.exp(cum)).astype(BF16)
            qk = q * k
            scaled = []
            for lv in range(levels):
                bs = cn >> (lv + 1)
                is_query = ((row // bs) % 2 == 1) if fwd else ((row // bs) % 2 == 0)
                if lv < levels - small:
                    pairs = cum.reshape(cn // (2 * bs), 2 * bs, cum.shape[1])
                    a_row = bs - 1 if fwd else bs
                    anchor = jnp.broadcast_to(pairs[:, a_row:a_row + 1, :], pairs.shape).reshape(cum.shape)
                    to_anchor = jnp.where(is_query, cum - anchor, anchor - cum)
                else:
                    blk = 1 + lv - (levels - small)
                    to_anchor = sums[blk * cn:(blk + 1) * cn]
                scaled.append((jnp.where(is_query, q, k) * jnp.exp(to_anchor)).astype(BF16))
            outs = []
            for hh in range(nh):
                sl = slice(hh * LANE, (hh + 1) * LANE)
                att = jnp.zeros((cn, cn), BF16)
                for lv in range(levels):
                    x = scaled[lv][:, sl]
                    att = att + _dot_nt(x, x).astype(BF16) * mask_ref[d_idx * levels + lv]
                same_token = jnp.sum(qk[:, sl], axis=-1, keepdims=True) * v[:, sl]
                outs.append(_dot_nt(qd[:, sl], s_ref[hh].astype(BF16)) + _dot(att, vb[:, sl]) + same_token)
            dst = out_ref.at[0] if len(out_ref.shape) == 3 else out_ref
            dst[rows, :] = jnp.concatenate(outs, axis=-1)
        for hh in range(nh):
            sl = slice(hh * LANE, (hh + 1) * LANE)
            s_ref[hh] = s_ref[hh] * decay[:, sl] + _dot(v[:, sl].T.astype(BF16), kd[:, sl])

    def rows_of(c):
        return pl.ds(pl.multiple_of(c * cn, cn), cn)

    sf_ref[...] = jnp.zeros(sf_ref.shape, F32)
    sb_ref[...] = jnp.zeros(sb_ref.shape, F32)

    def ctx_body(j, carry):
        visit(rows_of(j), None, ffc_ref, ic_ref, wf_ref, sf_ref, None, 0)
        visit(rows_of(nc_c - 1 - j), None, fbc_ref, ic_ref, wb_ref, sb_ref, None, 1)
        return carry

    def lat_body(j, carry):
        visit(rows_of(j), ql_ref, ffl_ref, il_ref, wf_ref, sf_ref, o_ref, 0)
        visit(rows_of(nc_l - 1 - j), ql_ref, fbl_ref, il_ref, wb_ref, sb_ref, ob_ref, 1)
        return carry

    lax.fori_loop(0, nc_c, ctx_body, 0)
    lax.fori_loop(0, nc_l, lat_body, 0)
    o_ref[0] = o_ref[0] + ob_ref[...]


def _hg_operators():
    cn = HG_CHUNK
    levels = cn.bit_length() - 1
    t = np.arange(cn)[:, None]
    r = np.arange(cn)[None, :]
    small = min(levels, HG_SMALL_LEVELS)
    w = np.zeros((2, 1 + small, cn, cn), np.float32)
    mask = np.zeros((2, levels, cn, cn), np.float32)
    w[0, 0] = r <= t
    w[1, 0] = r >= t
    for lv in range(levels):
        bs = cn >> (lv + 1)
        parent = t // (2 * bs) * (2 * bs)
        second = (t // bs) % 2 == 1
        blk = 1 + lv - (levels - small)
        if blk >= 1:
            anchor_f = parent + bs - 1
            w[0, blk] = np.where(second, (r > anchor_f) & (r <= t), (r > t) & (r <= anchor_f))
            anchor_b = parent + bs
            w[1, blk] = np.where(second, (r >= anchor_b) & (r < t), (r >= t) & (r < anchor_b))
        same_parent = (t // (2 * bs)) == (r // (2 * bs))
        key_second = (r // bs) % 2 == 1
        mask[0, lv] = same_parent & second & ~key_second
        mask[1, lv] = same_parent & ~second & key_second
    return (jnp.asarray(w.reshape(2, (1 + small) * cn, cn), BF16),
            jnp.asarray(mask.reshape(2 * levels, cn, cn), BF16))


def _hg_scan(z_l, z_c, lb, d):
    b, n, _ = z_l.shape
    n_c = z_c.shape[1]
    width = HG_HEADS_PER_STEP * LANE
    steps = d // width
    cn = HG_CHUNK
    w, mask = _hg_operators()
    col = lambda k: (lambda i, h: (i, 0, k * steps + h))
    lat = lambda k: pl.BlockSpec((1, n, width), col(k))
    ctx = lambda k: pl.BlockSpec((1, n_c, width), col(k))
    const = lambda a: pl.BlockSpec(a.shape, lambda i, h: (0,) * a.ndim)
    return pl.pallas_call(
        functools.partial(_hg_kernel, nc_c=n_c // cn, nc_l=n // cn), grid=(b, steps),
        in_specs=[lat(0), lat(1), lat(2), lat(3), ctx(1), ctx(2), ctx(3),
                  pl.BlockSpec((1, width), lambda i, h: (0, h)),
                  const(w[0]), const(w[1]), const(mask)],
        out_specs=pl.BlockSpec((1, n, width), lambda i, h: (i, 0, h)),
        out_shape=jax.ShapeDtypeStruct((b, n, d), F32),
        scratch_shapes=[pltpu.VMEM((n, width), F32),
                        pltpu.VMEM((HG_HEADS_PER_STEP, LANE, LANE), F32),
                        pltpu.VMEM((HG_HEADS_PER_STEP, LANE, LANE), F32)],
        compiler_params=_cparams(2), name="hgrn2_scan",
    )(z_l, z_l, z_l, z_l, z_c, z_c, z_c, lb.reshape(1, d), w[0], w[1], mask)


def _hg_out_kernel(o_ref, z_ref, gn_ref, x_ref, mod_ref, w_ref, g2_ref, wr_ref, out_ref, h_ref, aff_ref):
    for rows in _row_chunks(x_ref.shape[1]):
        o, gate = o_ref[0, rows, :], z_ref[0, rows, :]
        parts = []
        for h in range(o.shape[1] // LANE):
            sl = slice(h * LANE, (h + 1) * LANE)
            t = o[:, sl]
            t = t * lax.rsqrt(jnp.mean(t * t, axis=-1, keepdims=True) + EPS) * gn_ref[...]
            parts.append((t * _silu(gate[:, sl])).astype(BF16))
        y = _dot(jnp.concatenate(parts, axis=-1), w_ref[...])
        x = x_ref[0, rows, :] + mod_ref[0, 2:3, :] * y
        _router_tail(x, rows, mod_ref, g2_ref, wr_ref, out_ref, h_ref, aff_ref)


def _hg_out(o, z, gnorm, x, mod, w, g2, wr):
    b, n, d = x.shape
    tm = _row_tile(n)
    r_in, r_out, r_shape = _router_specs(b, n, d, tm)
    return pl.pallas_call(
        _hg_out_kernel, grid=(b, n // tm),
        in_specs=[pl.BlockSpec((1, tm, d), lambda i, j: (i, j, 0)),
                  pl.BlockSpec((1, tm, d), lambda i, j: (i, j, 4)),
                  pl.BlockSpec((1, LANE), lambda i, j: (0, 0)),
                  pl.BlockSpec((1, tm, d), lambda i, j: (i, j, 0)),
                  pl.BlockSpec((1, 6, d), lambda i, j: (i, 0, 0)),
                  pl.BlockSpec((d, d), lambda i, j: (0, 0))] + r_in,
        out_specs=r_out, out_shape=r_shape,
        compiler_params=_cparams(2), name="hgrn2_out",
    )(o, z, gnorm.reshape(1, LANE), x, mod, w, g2.reshape(1, d), wr)


def _route_kernel(aff_ref, tri_ref, pos_ref, selw_ref, start_ref, *, cap):
    groups = LANE // N_EXPERTS
    rg = aff_ref.shape[1] // groups
    samples = range(aff_ref.shape[0])
    denses = []
    for s in samples:
        dense = aff_ref[s, 0:rg, :]
        for g in range(1, groups):
            dense = dense + pltpu.roll(aff_ref[s, g * rg:(g + 1) * rg, :], g * N_EXPERTS, 1)
        denses.append(dense)
    all_bits = [lax.bitcast_convert_type(dense, jnp.int32) for dense in denses]
    lane = lax.broadcasted_iota(jnp.int32, (1, LANE), 1)
    tri = tri_ref[...]

    def indicator(mask):
        return jnp.where(mask, jnp.ones((), F32), jnp.zeros((), F32))

    def over_groups(row):
        for shift in (LANE // 2, LANE // 4, LANE // 8):
            row = row + pltpu.roll(row, shift, 1)
        return row

    def before_groups(row):
        out = jnp.zeros_like(row)
        for j in range(1, groups):
            out = out + jnp.where(lane >= j * N_EXPERTS, pltpu.roll(row, j * N_EXPERTS, 1), 0.0)
        return out

    def count(mask):
        return over_groups(jnp.sum(indicator(mask), axis=0, keepdims=True))

    def prefix(mask):
        x = indicator(mask)
        return _dot(tri, x.astype(BF16)) + before_groups(jnp.sum(x, axis=0, keepdims=True))

    def search(i, thrs):
        bit = jnp.left_shift(jnp.int32(1), 30 - i)
        return tuple(jnp.where(count(bits >= (thr | bit)) >= cap, thr | bit, thr) for bits, thr in zip(all_bits, thrs))

    thrs = lax.fori_loop(0, 31, search, tuple(jnp.zeros((1, LANE), jnp.int32) for _ in samples))
    for s, dense, bits, thr in zip(samples, denses, all_bits, thrs):
        above, tie = bits > thr, bits == thr
        need = cap - count(above)
        tie_rank = prefix(tie)
        sel = above | (tie & (tie_rank <= need))
        pos_ref[s] = jnp.where(sel, prefix(above) + jnp.minimum(tie_rank, need) - 1.0, -1.0)
        selw_ref[s] = jnp.where(sel, dense, 0.0)
        start_ref[s] = before_groups(jnp.sum(indicator(sel), axis=0, keepdims=True))


def _route(aff, cap):
    b, n, _ = aff.shape
    rg = n // (LANE // N_EXPERTS)
    idx = jnp.arange(rg)
    tri = (idx[None, :] <= idx[:, None]).astype(BF16)
    sb = math.gcd(b, ROUTE_SAMPLES)
    out_spec = pl.BlockSpec((sb, rg, LANE), lambda i: (i, 0, 0))
    return pl.pallas_call(
        functools.partial(_route_kernel, cap=cap), grid=(b // sb,),
        in_specs=[pl.BlockSpec((sb, n, LANE), lambda i: (i, 0, 0)), pl.BlockSpec((rg, rg), lambda i: (0, 0))],
        out_specs=[out_spec, out_spec, pl.BlockSpec((sb, 1, LANE), lambda i: (i, 0, 0))],
        out_shape=[jax.ShapeDtypeStruct((b, rg, LANE), F32)] * 2 + [jax.ShapeDtypeStruct((b, 1, LANE), F32)],
        compiler_params=_cparams(1), name="moe_route",
    )(aff, tri)


def _gather_kernel(pos_ref, h_ref, xs_ref, *, cap):
    h = h_ref[0]
    slot = lax.broadcasted_iota(jnp.int32, (cap, h.shape[0]), 0).astype(F32)
    for e in range(N_EXPERTS):
        onehot = jnp.where(pos_ref[0, e:e + 1, :] == slot, 1.0, 0.0).astype(BF16)
        xs_ref[e, 0] = _dot(onehot, h).astype(BF16)


def _gather(pos_rows, h, cap):
    b, n, d = h.shape
    return pl.pallas_call(
        functools.partial(_gather_kernel, cap=cap), grid=(b,),
        in_specs=[pl.BlockSpec((1, N_EXPERTS, n), lambda i: (i, 0, 0)),
                  pl.BlockSpec((1, n, d), lambda i: (i, 0, 0))],
        out_specs=pl.BlockSpec((N_EXPERTS, 1, cap, d), lambda i: (0, i, 0, 0)),
        out_shape=jax.ShapeDtypeStruct((N_EXPERTS, b, cap, d), BF16),
        compiler_params=_cparams(1), name="moe_gather",
    )(pos_rows, h)


def _ffn_kernel(xs_ref, wg_ref, wu_ref, wd_ref, o_ref):
    xs = xs_ref[0]
    hid = _silu(_dot(xs, wg_ref[0, 0])) * _dot(xs, wu_ref[0, 0])
    o_ref[0] = _dot(hid.astype(BF16), wd_ref[0, 0]).astype(BF16)


def _ffn(xs, wg, wu, wd, layer):
    e, m, d = xs.shape
    ff = wg.shape[3]
    tm = _row_tile(m)
    return pl.pallas_call(
        _ffn_kernel, grid=(e, m // tm),
        in_specs=[pl.BlockSpec((1, tm, d), lambda i, j: (i, j, 0)),
                  pl.BlockSpec((1, 1, d, ff), lambda i, j: (layer, i, 0, 0)),
                  pl.BlockSpec((1, 1, d, ff), lambda i, j: (layer, i, 0, 0)),
                  pl.BlockSpec((1, 1, ff, d), lambda i, j: (layer, i, 0, 0))],
        out_specs=pl.BlockSpec((1, tm, d), lambda i, j: (i, j, 0)),
        out_shape=jax.ShapeDtypeStruct((e, m, d), BF16),
        compiler_params=_cparams(2, 56), name="moe_ffn",
    )(xs, wg, wu, wd)


def _combine_kernel(*refs, cap, final, windowed):
    refs = list(refs)
    off_ref = refs.pop(0) if windowed else None
    out_ref, pos_ref, selw_ref, x_ref, mod_ref = refs[:5]
    o_ref = refs[-2] if windowed else refs[-1]
    x = x_ref[0]
    pos, selw = pos_ref[0], selw_ref[0]
    if windowed:
        shift = (LANE - pl.program_id(1) * N_EXPERTS) % LANE
        pos, selw = pltpu.roll(pos, shift, 1), pltpu.roll(selw, shift, 1)

    def all_slots():
        slot = lax.broadcasted_iota(jnp.int32, (x.shape[0], cap), 1).astype(F32)
        y = jnp.zeros(x.shape, F32)
        for e in range(N_EXPERTS):
            weighted = jnp.where(pos[:, e:e + 1] == slot, selw[:, e:e + 1], 0.0).astype(BF16)
            y = y + _dot(weighted, out_ref[e, 0])
        return y

    if windowed:
        y_ref = refs[-1]
        half = cap // 2
        i, g, last = pl.program_id(0), pl.program_id(1), pl.num_programs(1) - 1
        starts, fits = [], None
        for e in range(N_EXPERTS):
            lo = off_ref[i, g, e]
            hi = jnp.where(g < last, off_ref[i, jnp.minimum(g + 1, last), e], cap)
            w = jnp.minimum(lo // 16 * 16, cap - half)
            starts.append(w)
            ok = hi - w <= half
            fits = ok if fits is None else jnp.logical_and(fits, ok)

        @pl.when(fits)
        def _():
            pos_i = pos.astype(jnp.int32)
            slot = lax.broadcasted_iota(jnp.int32, (x.shape[0], half), 1)
            y = jnp.zeros(x.shape, F32)
            for e in range(0, N_EXPERTS, 2):
                lhs, rhs = [], []
                for k in (e, e + 1):
                    w = pl.multiple_of(starts[k], 16)
                    lhs.append(jnp.where(pos_i[:, k:k + 1] == slot + w, selw[:, k:k + 1], 0.0).astype(BF16))
                    rhs.append(out_ref[k, 0, pl.ds(w, half), :])
                y = y + _dot(jnp.concatenate(lhs, axis=1), jnp.concatenate(rhs, axis=0))
            y_ref[...] = y

        @pl.when(jnp.logical_not(fits))
        def _():
            y_ref[...] = all_slots()

        y = y_ref[...]
    else:
        y = all_slots()
    x = x + mod_ref[0, 5:6, :] * y
    if final:
        x = x * lax.rsqrt(jnp.mean(x * x, axis=-1, keepdims=True) + EPS) * refs[5][...]
    o_ref[0] = x


def _combine(out, pos, selw, starts, x, mod, cap, norm_f):
    b, n, d = x.shape
    groups = starts.shape[1]
    windowed = cap // 2 >= LANE and (n // groups) % 8 == 0
    tm = n // groups if windowed else _row_tile(n)
    final = norm_f is not None
    if windowed:
        routing = pl.BlockSpec((1, n // groups, LANE), lambda i, j, *_: (i, 0, 0))
    else:
        by_token = lambda a: (a.reshape(b, n // groups, groups, N_EXPERTS).transpose(0, 2, 1, 3)
                              .reshape(b, n, N_EXPERTS))
        pos, selw = by_token(pos), by_token(selw)
        routing = pl.BlockSpec((1, tm, N_EXPERTS), lambda i, j, *_: (i, j, 0))
    in_specs = [pl.BlockSpec((N_EXPERTS, 1, cap, d), lambda i, j, *_: (0, i, 0, 0)),
                routing, routing,
                pl.BlockSpec((1, tm, d), lambda i, j, *_: (i, j, 0)),
                pl.BlockSpec((1, 6, d), lambda i, j, *_: (i, 0, 0))]
    args = [out, pos, selw, x, mod]
    if final:
        in_specs.append(pl.BlockSpec((1, d), lambda i, j, *_: (0, 0)))
        args.append(norm_f.reshape(1, d))
    grid_spec = pltpu.PrefetchScalarGridSpec(
        num_scalar_prefetch=1 if windowed else 0, grid=(b, n // tm), in_specs=in_specs,
        out_specs=pl.BlockSpec((1, tm, d), lambda i, j, *_: (i, j, 0)),
        scratch_shapes=[pltpu.VMEM((tm, d), F32)] if windowed else [])
    return pl.pallas_call(
        functools.partial(_combine_kernel, cap=cap, final=final, windowed=windowed), grid_spec=grid_spec,
        out_shape=jax.ShapeDtypeStruct((b, n, d), F32),
        compiler_params=_cparams(2), name="moe_combine",
    )(*(([starts] if windowed else []) + args))


def _router_parts(w_router):
    w = jnp.pad(w_router.astype(F32), ((0, 0), (0, LANE - N_EXPERTS)))
    return jnp.concatenate(_split_hi_lo(w), axis=1)


def _moe(x, h, aff, mod, wg, wu, wd, layer, norm_f=None):
    b, n, d = x.shape
    cap = CAPACITY_FACTOR * n // N_EXPERTS
    groups = LANE // N_EXPERTS
    pos, selw, starts = _route(aff, cap)
    starts = starts.reshape(b, groups, N_EXPERTS).astype(jnp.int32)
    by_expert = pos.reshape(b, n // groups, groups, N_EXPERTS).transpose(0, 3, 2, 1).reshape(b, N_EXPERTS, n)
    xs = _gather(by_expert, h, cap)
    out = _ffn(xs.reshape(N_EXPERTS, b * cap, d), wg, wu, wd, layer).reshape(N_EXPERTS, b, cap, d)
    return _combine(out, pos, selw, starts, x, mod, cap, norm_f)


def kernel(x, c, ctx, c_ctx, w_mod, b_mod, norm1, norm2, s5_a_re, s5_a_im, s5_log_dt, s5_b_re, s5_b_im, s5_c_re, s5_c_im, s5_d, s5_w_glu, na_w_qkv, na_w_o, na_rpb, da_w_qkv, da_w_o, da_lambda, da_subln, hg_w_in, hg_w_o, hg_gnorm, hg_lower_bounds, moe_router, moe_w_gate, moe_w_up, moe_w_down, norm_f):
    b, n, d = x.shape
    depth = w_mod.shape[0]
    assert depth == 4, "layer i uses mixer i; only the last layer drops the context stream"

    rows_pad = -(b + 1) % 8
    cc = jnp.concatenate([c, c_ctx[None, :], jnp.zeros((rows_pad, d), F32)], axis=0)
    mod_all = _modulation(cc, w_mod, b_mod)
    p_lb = jax.nn.softmax(hg_lower_bounds.astype(F32), axis=0)
    lbs = jnp.cumsum(p_lb, axis=0) - p_lb[0]
    wg, wu, wd = moe_w_gate.astype(BF16), moe_w_up.astype(BF16), moe_w_down.astype(BF16)
    h_ctx = ctx

    for i in range(depth):
        last = i == depth - 1
        mod_l = mod_all[i, :b].reshape(b, 6, d)
        mod_c = jnp.broadcast_to(mod_all[i, b].reshape(1, 6, d), (b, 6, d))
        g1, g2 = norm1[i], norm2[i]
        wr = _router_parts(moe_router[i])
        moe_c = None
        if i == 0:
            tables = _s5_tables(s5_a_re[0], s5_a_im[0], s5_log_dt[0], s5_b_re[0], s5_b_im[0], s5_c_re[0], s5_c_im[0])
            moe_c, moe_l = _s5_layer(x, h_ctx, mod_l, mod_c, g1, tables, s5_d[0], s5_w_glu[0].astype(BF16), g2, wr)
        elif i == 1:
            w_qkv = na_w_qkv[0].astype(BF16)
            qkv_l = _proj_in(x, mod_l, g1, w_qkv, BF16, q_cols=d)
            qkv_c = _proj_in(h_ctx, mod_c, g1, w_qkv, BF16, q_cols=d)
            bias = _na_bias_table(na_rpb[0], n // GRID_W)
            w_o = na_w_o[0].astype(BF16)
            moe_l = _proj_out(_na_attention(qkv_l, qkv_c, bias, d), x, mod_l, w_o, g2, wr)
            moe_c = _proj_out(_attention(qkv_c, None, qkv_c, d, diff=False), h_ctx, mod_c, w_o, g2, wr)
        elif i == 2:
            w_qkv = da_w_qkv[0].astype(BF16)
            qkv_l = _proj_in(x, mod_l, g1, w_qkv, BF16, rope=_rope_tables(n), rope_cols=2 * d, q_cols=d)
            qkv_c = _proj_in(h_ctx, mod_c, g1, w_qkv, BF16, q_cols=d)
            lam = da_lambda[0].astype(F32)
            lam_init = 0.8 - 0.6 * math.exp(-0.3 * i)
            lam_full = jnp.exp(jnp.sum(lam[0] * lam[1])) - jnp.exp(jnp.sum(lam[2] * lam[3])) + lam_init
            w_o = da_w_o[0].astype(BF16)
            attend = functools.partial(_attention, d=d, diff=True, lam=lam_full, subln=da_subln[0],
                                       post_scale=1.0 - lam_init)
            moe_l = _proj_out(attend(qkv_l, qkv_l, qkv_c), x, mod_l, w_o, g2, wr)
            moe_c = _proj_out(attend(qkv_c, None, qkv_c), h_ctx, mod_c, w_o, g2, wr)
        else:
            w_in = hg_w_in[0].astype(BF16)
            z_l = _proj_in(x, mod_l, g1, w_in, F32)
            z_c = _proj_in(h_ctx, mod_c, g1, w_in, F32)
            moe_l = _hg_out(_hg_scan(z_l, z_c, lbs[i], d), z_l, hg_gnorm[0], x, mod_l, hg_w_o[0].astype(BF16), g2, wr)

        x = _moe(*moe_l, mod_l, wg, wu, wd, i, norm_f if last else None)
        if not last:
            h_ctx = _moe(*moe_c, mod_c, wg, wu, wd, i)
    return x
```

```python
import functools
import math

import jax
import jax.numpy as jnp
import numpy as np
from jax import lax
from jax.experimental import pallas as pl
from jax.experimental.pallas import tpu as pltpu

F32 = jnp.float32
BF16 = jnp.bfloat16
HIGHEST = lax.Precision.HIGHEST
EPS = 1e-6
NEG = -1e30

LANE = 128
GRID_W = 64
WIN_R = 8
WIN_C = 16
ATTN_ROWS = 256
ATTN_KEYS = 512
NA_ROWS = 4
HEAD_DIM = 64
QUERY_SCALE = HEAD_DIM ** -0.5 * math.log2(math.e)
ROPE_BASE = 10000.0
S5_GROUP = 16
S5_CHUNK = 16
HG_CHUNK = 256
HG_HEADS_PER_STEP = 2
HG_SMALL_LEVELS = 3
N_EXPERTS = 16
CAPACITY_FACTOR = 2
ROUTE_SAMPLES = 4
ROW_TILE = 512
EPILOGUE_ROWS = 256


def _cparams(n_axes, vmem_mb=48):
    return pltpu.CompilerParams(dimension_semantics=("arbitrary",) * n_axes,
                                vmem_limit_bytes=vmem_mb * 1024 * 1024)


def _dot(a, b):
    return jnp.dot(a, b, preferred_element_type=F32)


def _dot_nt(a, b):
    return lax.dot_general(a, b, (((1,), (1,)), ((), ())), preferred_element_type=F32)


def _normmod(x, g, shift, scale):
    y = x * lax.rsqrt(jnp.mean(x * x, axis=-1, keepdims=True) + EPS)
    return (y * g) * (1.0 + scale) + shift


def _split_hi_lo(x):
    bits = lax.bitcast_convert_type(x, jnp.int32)
    hi = lax.bitcast_convert_type((bits + jnp.int32(0x8000)) & jnp.int32(-65536), F32)
    return hi.astype(BF16), (x - hi).astype(BF16)


def _row_chunks(tm):
    step = min(tm, EPILOGUE_ROWS)
    return [slice(r, r + step) for r in range(0, tm, step)]


def _router_tail(x, rows, mod_ref, g2_ref, wr_ref, x_ref, h_ref, aff_ref):
    x_ref[0, rows, :] = x
    h = _normmod(x, g2_ref[...], mod_ref[0, 3:4, :], mod_ref[0, 4:5, :])
    h_hi, h_lo = _split_hi_lo(h)
    h_ref[0, rows, :] = h_hi
    both = _dot(h_hi, wr_ref[...]) + _dot(h_lo, wr_ref[...])
    logits = both[:, :LANE] + both[:, LANE:]
    lane = lax.broadcasted_iota(jnp.int32, logits.shape, 1)
    logits = jnp.where(lane < N_EXPERTS, logits, NEG)
    e = jnp.exp(logits - logits.max(axis=-1, keepdims=True))
    aff_ref[0, rows, :] = e / e.sum(axis=-1, keepdims=True)


def _router_specs(b, n, d, tm):
    tile = lambda w: pl.BlockSpec((1, tm, w), lambda i, j: (i, j, 0))
    in_specs = [pl.BlockSpec((1, d), lambda i, j: (0, 0)), pl.BlockSpec((d, 2 * LANE), lambda i, j: (0, 0))]
    out_shape = [jax.ShapeDtypeStruct((b, n, d), F32), jax.ShapeDtypeStruct((b, n, d), BF16),
                 jax.ShapeDtypeStruct((b, n, LANE), F32)]
    return in_specs, [tile(d), tile(d), tile(LANE)], out_shape


def _silu(x):
    return x * jax.nn.sigmoid(x)


def _row_tile(n):
    return min(n, ROW_TILE)


def _mod_kernel(c_ref, w_ref, b_ref, o_ref):
    s = _silu(c_ref[...])
    o_ref[0] = jnp.dot(s, w_ref[0], precision=HIGHEST, preferred_element_type=F32) + b_ref[0]


def _modulation(cc, w_mod, b_mod):
    depth, d, d6 = w_mod.shape
    r = cc.shape[0]
    return pl.pallas_call(
        _mod_kernel, grid=(depth, d6 // d),
        in_specs=[pl.BlockSpec((r, d), lambda i, j: (0, 0)),
                  pl.BlockSpec((1, d, d), lambda i, j: (i, 0, j)),
                  pl.BlockSpec((1, 1, d), lambda i, j: (i, 0, j))],
        out_specs=pl.BlockSpec((1, r, d), lambda i, j: (i, 0, j)),
        out_shape=jax.ShapeDtypeStruct((depth, r, d6), F32),
        compiler_params=_cparams(2), name="modulation",
    )(cc, w_mod, b_mod.reshape(depth, 1, d6))


def _proj_in_kernel(x_ref, mod_ref, g_ref, w_ref, *rest, rope_cols, q_cols):
    o_ref = rest[-1]
    h = _normmod(x_ref[0], g_ref[...], mod_ref[0, 0:1, :], mod_ref[0, 1:2, :])
    y = _dot(h.astype(BF16), w_ref[...])
    if rope_cols:
        cos, s_up, s_dn = rest[0][...], rest[1][...], rest[2][...]
        for j in range(rope_cols // LANE):
            t = y[:, j * LANE:(j + 1) * LANE]
            t = t * cos + pltpu.roll(t, LANE - 16, 1) * s_up + pltpu.roll(t, 16, 1) * s_dn
            if j * LANE < q_cols:
                t = t * QUERY_SCALE
            o_ref[0, :, j * LANE:(j + 1) * LANE] = t.astype(o_ref.dtype)
        o_ref[0, :, rope_cols:] = y[:, rope_cols:].astype(o_ref.dtype)
    elif q_cols:
        o_ref[0, :, :q_cols] = (y[:, :q_cols] * QUERY_SCALE).astype(o_ref.dtype)
        o_ref[0, :, q_cols:] = y[:, q_cols:].astype(o_ref.dtype)
    else:
        o_ref[0] = y.astype(o_ref.dtype)


def _proj_in(x, mod, g, w, out_dtype, rope=None, rope_cols=0, q_cols=0):
    b, n, d = x.shape
    nout = w.shape[1]
    tm = _row_tile(n)
    in_specs = [pl.BlockSpec((1, tm, d), lambda i, j: (i, j, 0)),
                pl.BlockSpec((1, 6, d), lambda i, j: (i, 0, 0)),
                pl.BlockSpec((1, d), lambda i, j: (0, 0)),
                pl.BlockSpec((d, nout), lambda i, j: (0, 0))]
    args = [x, mod, g.reshape(1, d), w]
    if rope_cols:
        in_specs += [pl.BlockSpec((tm, LANE), lambda i, j: (j, 0))] * 3
        args += list(rope)
    return pl.pallas_call(
        functools.partial(_proj_in_kernel, rope_cols=rope_cols, q_cols=q_cols), grid=(b, n // tm),
        in_specs=in_specs,
        out_specs=pl.BlockSpec((1, tm, nout), lambda i, j: (i, j, 0)),
        out_shape=jax.ShapeDtypeStruct((b, n, nout), out_dtype),
        compiler_params=_cparams(2, 56), name="proj_in",
    )(*args)


def _rope_tables(n):
    t = jnp.arange(n)
    row = (t // GRID_W).astype(F32)
    col = (t % GRID_W).astype(F32)
    quarter = HEAD_DIM // 4
    inv = ROPE_BASE ** (-jnp.arange(quarter, dtype=F32) / quarter)
    lane = jnp.arange(LANE)
    pos = jnp.where(((lane % HEAD_DIM) // (HEAD_DIM // 2))[None, :] == 0, row[:, None], col[:, None])
    ang = pos * inv[lane % quarter][None, :]
    first = ((lane % (HEAD_DIM // 2)) < quarter)[None, :]
    cos, sin = jnp.cos(ang), jnp.sin(ang)
    return cos, jnp.where(first, -sin, 0.0), jnp.where(first, 0.0, sin)


def _proj_out_kernel(o_ref, x_ref, mod_ref, w_ref, g2_ref, wr_ref, out_ref, h_ref, aff_ref):
    for rows in _row_chunks(x_ref.shape[1]):
        x = x_ref[0, rows, :] + mod_ref[0, 2:3, :] * _dot(o_ref[0, rows, :], w_ref[...])
        _router_tail(x, rows, mod_ref, g2_ref, wr_ref, out_ref, h_ref, aff_ref)


def _proj_out(o, x, mod, w, g2, wr):
    b, n, d = x.shape
    tm = _row_tile(n)
    r_in, r_out, r_shape = _router_specs(b, n, d, tm)
    return pl.pallas_call(
        _proj_out_kernel, grid=(b, n // tm),
        in_specs=[pl.BlockSpec((1, tm, d), lambda i, j: (i, j, 0)),
                  pl.BlockSpec((1, tm, d), lambda i, j: (i, j, 0)),
                  pl.BlockSpec((1, 6, d), lambda i, j: (i, 0, 0)),
                  pl.BlockSpec((d, d), lambda i, j: (0, 0))] + r_in,
        out_specs=r_out, out_shape=r_shape,
        compiler_params=_cparams(2), name="proj_out",
    )(o, x, mod, w, g2.reshape(1, d), wr)


def _softmax_parts(scores):
    m = scores[0].max(axis=-1, keepdims=True)
    for s in scores[1:]:
        m = jnp.maximum(m, s.max(axis=-1, keepdims=True))
    es = [jnp.exp2(s - m) for s in scores]
    z = es[0].sum(axis=-1, keepdims=True)
    for e in es[1:]:
        z = z + e.sum(axis=-1, keepdims=True)
    return es, z


def _attn_kernel(*refs, diff, has_lat, post_scale):
    refs = list(refs)
    q_ref = refs.pop(0)
    kv = [(refs.pop(0), refs.pop(0))] if has_lat else []
    kv.append((refs.pop(0), refs.pop(0)))
    if diff:
        lam_ref, sub_ref = refs.pop(0), refs.pop(0)
    o_ref = refs.pop(0)
    tq = q_ref.shape[1]
    lo = lax.broadcasted_iota(jnp.int32, (1, LANE), 1) < HEAD_DIM
    zero = jnp.zeros((), BF16)
    for t in range(q_ref.shape[2] // LANE):
        sl = slice(t * LANE, (t + 1) * LANE)
        qt = q_ref[0, :, sl]
        q2 = jnp.concatenate([jnp.where(lo, qt, zero), jnp.where(lo, zero, qt)], axis=0)
        m = jnp.full((2 * tq, 1), NEG, F32)
        z = jnp.zeros((2 * tq, 1), F32)
        acc = jnp.zeros((2 * tq, LANE), F32)
        for k_ref, v_ref in kv:
            for c0 in range(0, k_ref.shape[1], ATTN_KEYS):
                c1 = min(c0 + ATTN_KEYS, k_ref.shape[1])
                s = _dot_nt(q2, k_ref[0, c0:c1, sl])
                m_new = jnp.maximum(m, s.max(axis=-1, keepdims=True))
                carry = jnp.exp2(m - m_new)
                e = jnp.exp2(s - m_new)
                z = carry * z + e.sum(axis=-1, keepdims=True)
                acc = carry * acc + _dot(e.astype(BF16), v_ref[0, c0:c1, sl])
                m = m_new
        acc = acc / z
        if diff:
            o = acc[:tq] - lam_ref[0, 0] * acc[tq:]
            o = o * lax.rsqrt(jnp.mean(o * o, axis=-1, keepdims=True) + EPS) * sub_ref[...] * post_scale
        else:
            o = jnp.where(lo, acc[:tq], acc[tq:])
        o_ref[0, :, sl] = o.astype(o_ref.dtype)


def _attention(qkv_q, qkv_lat, qkv_ctx, d, diff, lam=None, subln=None, post_scale=1.0):
    b, nq, _ = qkv_q.shape
    tq = min(nq, ATTN_ROWS)
    has_lat = qkv_lat is not None
    in_specs = [pl.BlockSpec((1, tq, d), lambda i, j: (i, j, 0))]
    args = [qkv_q]
    for src in ([qkv_lat] if has_lat else []) + [qkv_ctx]:
        nk = src.shape[1]
        in_specs += [pl.BlockSpec((1, nk, d), lambda i, j: (i, 0, 1)),
                     pl.BlockSpec((1, nk, d), lambda i, j: (i, 0, 2))]
        args += [src, src]
    if diff:
        in_specs += [pl.BlockSpec(memory_space=pltpu.SMEM), pl.BlockSpec((1, LANE), lambda i, j: (0, 0))]
        args += [lam.reshape(1, 1), subln.reshape(1, LANE)]
    return pl.pallas_call(
        functools.partial(_attn_kernel, diff=diff, has_lat=has_lat, post_scale=post_scale),
        grid=(b, nq // tq), in_specs=in_specs,
        out_specs=pl.BlockSpec((1, tq, d), lambda i, j: (i, j, 0)),
        out_shape=jax.ShapeDtypeStruct((b, nq, d), BF16),
        compiler_params=_cparams(2, 56), name="attention",
    )(*args)


def _na_kernel(cls_ref, q_ref, k_ref, v_ref, kc_ref, vc_ref, bias_ref, o_ref, *, rows, wr, kr):
    del cls_ref
    kstart = jnp.clip(pl.program_id(1) * NA_ROWS - wr // 2, 0, rows - kr)
    start = pl.multiple_of(kstart * GRID_W, GRID_W)
    nk = kr * GRID_W
    tq = q_ref.shape[1]
    lo = lax.broadcasted_iota(jnp.int32, (1, LANE), 1) < HEAD_DIM
    zero = jnp.zeros((), BF16)
    for t in range(q_ref.shape[2] // LANE):
        sl = slice(t * LANE, (t + 1) * LANE)
        qt = q_ref[0, :, sl]
        kw, vw = k_ref[0, pl.ds(start, nk), sl], v_ref[0, pl.ds(start, nk), sl]
        kc, vc = kc_ref[0, :, sl], vc_ref[0, :, sl]
        q2 = jnp.concatenate([jnp.where(lo, qt, zero), jnp.where(lo, zero, qt)], axis=0)
        (el, ec), z = _softmax_parts([_dot_nt(q2, kw) + bias_ref[0, t], _dot_nt(q2, kc)])
        acc = (_dot(el.astype(BF16), vw) + _dot(ec.astype(BF16), vc)) / z
        o_ref[0, :, sl] = jnp.where(lo, acc[:tq], acc[tq:]).astype(o_ref.dtype)


def _na_layout(rows):
    wr = min(WIN_R, rows)
    kr = min(rows, wr + NA_ROWS - 1)
    classes, cls_of_block = [], []
    for blk in range(rows // NA_ROWS):
        kstart = min(max(blk * NA_ROWS - wr // 2, 0), rows - kr)
        key = []
        for qr in range(NA_ROWS):
            r = blk * NA_ROWS + qr
            r0 = min(max(r - wr // 2, 0), rows - wr)
            key.append(tuple((kstart + j - r + WIN_R - 1) if r0 <= kstart + j < r0 + wr else None for j in range(kr)))
        key = tuple(key)
        if key not in classes:
            classes.append(key)
        cls_of_block.append(classes.index(key))
    return wr, kr, classes, cls_of_block


def _na_bias_table(rpb, rows):
    heads = rpb.shape[0]
    _, _, classes, _ = _na_layout(rows)
    cq = jnp.arange(GRID_W)
    c0 = jnp.clip(cq - WIN_C // 2, 0, GRID_W - WIN_C)
    col_in = (cq[None, :] >= c0[:, None]) & (cq[None, :] < c0[:, None] + WIN_C)
    dc = jnp.clip(cq[None, :] - cq[:, None] + WIN_C - 1, 0, 2 * WIN_C - 2)
    dc_onehot = (dc[:, :, None] == jnp.arange(2 * WIN_C - 1)[None, None, :]).astype(F32)
    per_dr = jnp.einsum('hrd,qkd->hrqk', rpb.astype(F32) * math.log2(math.e), dc_onehot,
                        precision=HIGHEST)
    per_dr = jnp.where(col_in[None, None], per_dr, NEG)
    masked = jnp.full((heads, GRID_W, GRID_W), NEG, F32)
    tabs = []
    for key in classes:
        slabs = [jnp.concatenate([masked if dr is None else per_dr[:, dr] for dr in row], axis=-1) for row in key]
        tabs.append(jnp.concatenate(slabs, axis=1))
    tab = jnp.stack(tabs, axis=0)
    return tab.reshape(tab.shape[0], heads // 2, 2 * tab.shape[2], tab.shape[3])


def _na_attention(qkv_l, qkv_c, bias, d):
    b, n, _ = qkv_l.shape
    nc = qkv_c.shape[1]
    rows = n // GRID_W
    wr, kr, _, cls_of_block = _na_layout(rows)
    heads = d // HEAD_DIM
    tq = NA_ROWS * GRID_W
    grid_spec = pltpu.PrefetchScalarGridSpec(
        num_scalar_prefetch=1, grid=(b, rows // NA_ROWS),
        in_specs=[pl.BlockSpec((1, tq, d), lambda i, r, cls: (i, r, 0)),
                  pl.BlockSpec((1, n, d), lambda i, r, cls: (i, 0, 1)),
                  pl.BlockSpec((1, n, d), lambda i, r, cls: (i, 0, 2)),
                  pl.BlockSpec((1, nc, d), lambda i, r, cls: (i, 0, 1)),
                  pl.BlockSpec((1, nc, d), lambda i, r, cls: (i, 0, 2)),
                  pl.BlockSpec((1, heads // 2, 2 * tq, kr * GRID_W), lambda i, r, cls: (cls[r], 0, 0, 0),
                               pipeline_mode=pl.Buffered(1))],
        out_specs=pl.BlockSpec((1, tq, d), lambda i, r, cls: (i, r, 0)))
    return pl.pallas_call(
        functools.partial(_na_kernel, rows=rows, wr=wr, kr=kr),
        grid_spec=grid_spec,
        out_shape=jax.ShapeDtypeStruct((b, n, d), BF16),
        compiler_params=_cparams(2, 56), name="na_attention",
    )(jnp.asarray(cls_of_block, jnp.int32), qkv_l, qkv_l, qkv_l, qkv_c, qkv_c, bias)


def _normmod_kernel(x_ref, mod_ref, g_ref, o_ref):
    o_ref[0] = _normmod(x_ref[0], g_ref[...], mod_ref[0, 0:1, :], mod_ref[0, 1:2, :]).astype(o_ref.dtype)


def _normmod_call(x, mod, g):
    b, n, d = x.shape
    tm = _row_tile(n)
    return pl.pallas_call(
        _normmod_kernel, grid=(b, n // tm),
        in_specs=[pl.BlockSpec((1, tm, d), lambda i, j: (i, j, 0)),
                  pl.BlockSpec((1, 6, d), lambda i, j: (i, 0, 0)),
                  pl.BlockSpec((1, d), lambda i, j: (0, 0))],
        out_specs=pl.BlockSpec((1, tm, d), lambda i, j: (i, j, 0)),
        out_shape=jax.ShapeDtypeStruct((b, n, d), BF16),
        compiler_params=_cparams(2), name="normmod",
    )(x, mod, g.reshape(1, d))


def _s5_tables(a_re, a_im, log_dt, b_re, b_im, c_re, c_im):
    t_len = S5_CHUNK
    a_re, a_im = a_re.astype(F32), a_im.astype(F32)
    dt = jnp.exp(log_dt.astype(F32))[..., None]
    lr, li = a_re * dt, a_im * dt
    cos_li, sin_li = jnp.cos(li), jnp.sin(li)
    ab_im = jnp.exp(lr) * sin_li
    nr = jnp.expm1(lr) * cos_li - 2.0 * jnp.sin(0.5 * li) ** 2
    den = a_re * a_re + a_im * a_im
    fr = (nr * a_re + ab_im * a_im) / den
    fi = (ab_im * a_re - nr * a_im) / den
    b_re, b_im = b_re.astype(F32), b_im.astype(F32)
    bb_re = fr[..., None] * b_re - fi[..., None] * b_im
    bb_im = fr[..., None] * b_im + fi[..., None] * b_re
    tau = jnp.arange(t_len + 1, dtype=F32)[:, None, None, None]
    mag = jnp.exp(tau * lr)
    pr, pi = mag * jnp.cos(tau * li), mag * jnp.sin(tau * li)
    c_re, c_im = c_re.astype(F32), c_im.astype(F32)
    cw_re = c_re[None] * pr[:, :, :, None, :] - c_im[None] * pi[:, :, :, None, :]
    cw_im = c_re[None] * pi[:, :, :, None, :] + c_im[None] * pr[:, :, :, None, :]
    kern = (jnp.einsum('tdgkp,dgph->tdgkh', cw_re, bb_re, precision=HIGHEST)
            - jnp.einsum('tdgkp,dgph->tdgkh', cw_im, bb_im, precision=HIGHEST))
    g = a_re.shape[1]
    h = S5_GROUP
    diff = jnp.arange(t_len)[None, :] - jnp.arange(t_len)[:, None]
    lags = jnp.arange(t_len)[None, None, :]
    fwd_lag = (diff[:, :, None] == lags).astype(F32)
    bwd_lag = (-diff[:, :, None] == lags).astype(F32)
    k_f = jnp.einsum('sta,agkh->stgkh', fwd_lag, kern[:t_len, 0], precision=HIGHEST)
    k_b = jnp.einsum('sta,agkh->stgkh', bwd_lag, kern[:t_len, 1], precision=HIGHEST)
    m_both = (k_f + k_b).transpose(2, 0, 4, 1, 3).reshape(g, t_len * h, t_len * h)

    def in_map(d, p_re, p_im):
        wr_ = p_re[..., None] * bb_re[d][None] - p_im[..., None] * bb_im[d][None]
        wi_ = p_re[..., None] * bb_im[d][None] + p_im[..., None] * bb_re[d][None]
        to_rows = lambda w: w.transpose(1, 0, 3, 2).reshape(g, t_len * h, -1)
        return to_rows(wr_), to_rows(wi_)

    f_re, f_im = in_map(0, pr[:t_len, 0][::-1], pi[:t_len, 0][::-1])
    r_re, r_im = in_map(1, pr[:t_len, 1], pi[:t_len, 1])
    w_in = jnp.concatenate([f_re, f_im, f_im, f_re, r_re, r_im, r_im, r_re], axis=-1)

    def out_map(w_re, w_im):
        to_cols = lambda w: w.transpose(1, 3, 0, 2).reshape(g, -1, t_len * h)
        return jnp.concatenate([to_cols(w_re), -to_cols(w_im)], axis=1)

    w_out = jnp.concatenate([out_map(cw_re[1:, 0], cw_im[1:, 0]),
                             out_map(cw_re[1:, 1][::-1], cw_im[1:, 1][::-1])], axis=1)
    a_pow = jnp.stack([jnp.concatenate([pr[t_len, 0], pr[t_len, 0]], -1),
                       jnp.concatenate([-pi[t_len, 0], pi[t_len, 0]], -1),
                       jnp.concatenate([pr[t_len, 1], pr[t_len, 1]], -1),
                       jnp.concatenate([-pi[t_len, 1], pi[t_len, 1]], -1)], axis=1)
    return m_both.astype(BF16), w_in.astype(BF16), w_out.astype(BF16), a_pow


def _s5_core_kernel(uc_ref, ul_ref, m_ref, win_ref, wout_ref, a_ref, yc_ref, yl_ref, sall_ref, sin_ref,
                    *, bn, nc_c, nc_l):
    rc = nc_c * bn
    sin_ref[0:rc, :] = _dot(uc_ref[0], win_ref[0])
    sin_ref[rc:, :] = _dot(ul_ref[0], win_ref[0])
    a = a_ref[0]
    a1f, a2f, a1b, a2b = a[0:1], a[1:2], a[2:3], a[3:4]
    w = LANE
    zero = jnp.zeros((bn, w), F32)

    def rows_of(c):
        return pl.ds(pl.multiple_of(c * bn, bn), bn)

    def fwd(c, carry):
        s, sw = carry
        r = rows_of(c)
        sall_ref[r, 0:w] = s
        return (a1f * s + a2f * sw + sin_ref[r, 0:w], a1f * sw - a2f * s + sin_ref[r, w:2 * w])

    def bwd(c, carry):
        s, sw = carry
        r = rows_of(c)
        sall_ref[r, w:2 * w] = s
        return (a1b * s + a2b * sw + sin_ref[r, 2 * w:3 * w], a1b * sw - a2b * s + sin_ref[r, 3 * w:4 * w])

    lax.fori_loop(0, nc_c + nc_l, fwd, (zero, zero))
    carry = lax.fori_loop(0, nc_c, lambda i, cr: bwd(nc_c - 1 - i, cr), (zero, zero))
    lax.fori_loop(0, nc_l, lambda i, cr: bwd(nc_c + nc_l - 1 - i, cr), carry)
    yc_ref[0] = (_dot(uc_ref[0], m_ref[0]) + _dot(sall_ref[0:rc, :].astype(BF16), wout_ref[0])).astype(yc_ref.dtype)
    yl_ref[0] = (_dot(ul_ref[0], m_ref[0]) + _dot(sall_ref[rc:, :].astype(BF16), wout_ref[0])).astype(yl_ref.dtype)


def _s5_core(uc_rows, ul_rows, tables, bn):
    g, rc, width = uc_rows.shape
    rl = ul_rows.shape[1]
    m_both, w_in, w_out, a_pow = tables
    rows = lambda r: pl.BlockSpec((1, r, width), lambda i: (i, 0, 0))
    return pl.pallas_call(
        functools.partial(_s5_core_kernel, bn=bn, nc_c=rc // bn, nc_l=rl // bn), grid=(g,),
        in_specs=[rows(rc), rows(rl),
                  pl.BlockSpec((1, width, width), lambda i: (i, 0, 0)),
                  pl.BlockSpec((1, width, 4 * LANE), lambda i: (i, 0, 0)),
                  pl.BlockSpec((1, 2 * LANE, width), lambda i: (i, 0, 0)),
                  pl.BlockSpec((1, 4, LANE), lambda i: (i, 0, 0))],
        out_specs=[rows(rc), rows(rl)],
        out_shape=[jax.ShapeDtypeStruct((g, rc, width), BF16), jax.ShapeDtypeStruct((g, rl, width), BF16)],
        scratch_shapes=[pltpu.VMEM((rc + rl, 2 * LANE), F32), pltpu.VMEM((rc + rl, 4 * LANE), F32)],
        compiler_params=_cparams(1, 56), name="s5_core",
    )(uc_rows, ul_rows, m_both, w_in, w_out, a_pow)


def _s5_glu_kernel(x_ref, y_ref, mod_ref, g_ref, d_ref, w_ref, g2_ref, wr_ref, out_ref, h_ref, aff_ref):
    d = x_ref.shape[2]
    for rows in _row_chunks(x_ref.shape[1]):
        x = x_ref[0, rows, :]
        u = _normmod(x, g_ref[...], mod_ref[0, 0:1, :], mod_ref[0, 1:2, :])
        z = jax.nn.gelu(y_ref[0, rows, :] + d_ref[...] * u)
        zz = _dot(z.astype(BF16), w_ref[...])
        x = x + mod_ref[0, 2:3, :] * (zz[:, :d] * jax.nn.sigmoid(zz[:, d:]))
        _router_tail(x, rows, mod_ref, g2_ref, wr_ref, out_ref, h_ref, aff_ref)


def _s5_glu(x, y, mod, g, dskip, w_glu, g2, wr):
    b, n, d = x.shape
    tm = _row_tile(n)
    r_in, r_out, r_shape = _router_specs(b, n, d, tm)
    return pl.pallas_call(
        _s5_glu_kernel, grid=(b, n // tm),
        in_specs=[pl.BlockSpec((1, tm, d), lambda i, j: (i, j, 0)),
                  pl.BlockSpec((1, tm, d), lambda i, j: (i, j, 0)),
                  pl.BlockSpec((1, 6, d), lambda i, j: (i, 0, 0)),
                  pl.BlockSpec((1, d), lambda i, j: (0, 0)),
                  pl.BlockSpec((1, d), lambda i, j: (0, 0)),
                  pl.BlockSpec((d, 2 * d), lambda i, j: (0, 0))] + r_in,
        out_specs=r_out, out_shape=r_shape,
        compiler_params=_cparams(2), name="s5_glu",
    )(x, y, mod, g.reshape(1, d), dskip.reshape(1, d), w_glu, g2.reshape(1, d), wr)


def _s5_layer(x, h_ctx, mod_l, mod_c, g, tables, dskip, w_glu, g2, wr):
    b, n, d = x.shape
    n_c = h_ctx.shape[1]
    t_len, h = S5_CHUNK, S5_GROUP
    groups = d // h
    nc_l, nc_c = n // t_len, n_c // t_len

    def to_rows(u, nc):
        return u.reshape(b, nc, t_len, groups, h).transpose(3, 1, 0, 2, 4).reshape(groups, nc * b, t_len * h)

    def from_rows(y, nc):
        return y.reshape(groups, nc, b, t_len, h).transpose(2, 1, 3, 0, 4).reshape(b, nc * t_len, d)

    yc_rows, yl_rows = _s5_core(to_rows(_normmod_call(h_ctx, mod_c, g), nc_c),
                                to_rows(_normmod_call(x, mod_l, g), nc_l), tables, b)
    y_c, y_l = from_rows(yc_rows, nc_c), from_rows(yl_rows, nc_l)
    return (_s5_glu(h_ctx, y_c, mod_c, g, dskip, w_glu, g2, wr), _s5_glu(x, y_l, mod_l, g, dskip, w_glu, g2, wr))


def _hg_kernel(ql_ref, ffl_ref, fbl_ref, il_ref, ffc_ref, fbc_ref, ic_ref, lb_ref, wf_ref, wb_ref, mask_ref,
               o_ref, ob_ref, sf_ref, sb_ref, *, nc_c, nc_l):
    cn = HG_CHUNK
    levels = cn.bit_length() - 1
    small = min(levels, HG_SMALL_LEVELS)
    nh = HG_HEADS_PER_STEP
    lb = lb_ref[...]
    row = lax.broadcasted_iota(jnp.int32, (cn, 1), 0)

    def gates(ff):
        f = lb + (1.0 - lb) * jax.nn.sigmoid(ff)
        return jnp.log(f), 1.0 - f

    def dot_hi_lo(w01, x):
        hi, low = _split_hi_lo(x)
        return _dot(w01, hi) + _dot(w01, low)

    def visit(rows, q_ref, ff_ref, i_ref, w_ref, s_ref, out_ref, d_idx):
        fwd = d_idx == 0
        lf, k = gates(ff_ref[0, rows, :])
        v = _silu(i_ref[0, rows, :])
        n_blocks = 1 if out_ref is None else 1 + small
        sums = dot_hi_lo(w_ref[0:n_blocks * cn, :], lf)
        cum = sums[0:cn]
        total = cum[cn - 1:cn] if fwd else cum[0:1]
        kd = (k * jnp.exp(total - cum)).astype(BF16)
        decay = jnp.exp(total)
        vb = v.astype(BF16)
        if out_ref is not None:
            q = q_ref[0, rows, :]
            qd = (q * jnp.exp(cum)).astype(BF16)
            qk = q * k
            scaled = []
            for lv in range(levels):
                bs = cn >> (lv + 1)
                is_query = ((row // bs) % 2 == 1) if fwd else ((row // bs) % 2 == 0)
                if lv < levels - small:
                    pairs = cum.reshape(cn // (2 * bs), 2 * bs, cum.shape[1])
                    a_row = bs - 1 if fwd else bs
                    anchor = jnp.broadcast_to(pairs[:, a_row:a_row + 1, :], pairs.shape).reshape(cum.shape)
                    to_anchor = jnp.where(is_query, cum - anchor, anchor - cum)
                else:
                    blk = 1 + lv - (levels - small)
                    to_anchor = sums[blk * cn:(blk + 1) * cn]
                scaled.append((jnp.where(is_query, q, k) * jnp.exp(to_anchor)).astype(BF16))
            outs = []
            for hh in range(nh):
                sl = slice(hh * LANE, (hh + 1) * LANE)
                att = jnp.zeros((cn, cn), BF16)
                for lv in range(levels):
                    x = scaled[lv][:, sl]
                    att = att + _dot_nt(x, x).astype(BF16) * mask_ref[d_idx * levels + lv]
                same_token = jnp.sum(qk[:, sl], axis=-1, keepdims=True) * v[:, sl]
                outs.append(_dot_nt(qd[:, sl], s_ref[hh].astype(BF16)) + _dot(att, vb[:, sl]) + same_token)
            dst = out_ref.at[0] if len(out_ref.shape) == 3 else out_ref
            dst[rows, :] = jnp.concatenate(outs, axis=-1)
        for hh in range(nh):
            sl = slice(hh * LANE, (hh + 1) * LANE)
            s_ref[hh] = s_ref[hh] * decay[:, sl] + _dot(v[:, sl].T.astype(BF16), kd[:, sl])

    def rows_of(c):
        return pl.ds(pl.multiple_of(c * cn, cn), cn)

    sf_ref[...] = jnp.zeros(sf_ref.shape, F32)
    sb_ref[...] = jnp.zeros(sb_ref.shape, F32)

    def ctx_body(j, carry):
        visit(rows_of(j), None, ffc_ref, ic_ref, wf_ref, sf_ref, None, 0)
        visit(rows_of(nc_c - 1 - j), None, fbc_ref, ic_ref, wb_ref, sb_ref, None, 1)
        return carry

    def lat_body(j, carry):
        visit(rows_of(j), ql_ref, ffl_ref, il_ref, wf_ref, sf_ref, o_ref, 0)
        visit(rows_of(nc_l - 1 - j), ql_ref, fbl_ref, il_ref, wb_ref, sb_ref, ob_ref, 1)
        return carry

    lax.fori_loop(0, nc_c, ctx_body, 0)
    lax.fori_loop(0, nc_l, lat_body, 0)
    o_ref[0] = o_ref[0] + ob_ref[...]


def _hg_operators():
    cn = HG_CHUNK
    levels = cn.bit_length() - 1
    t = np.arange(cn)[:, None]
    r = np.arange(cn)[None, :]
    small = min(levels, HG_SMALL_LEVELS)
    w = np.zeros((2, 1 + small, cn, cn), np.float32)
    mask = np.zeros((2, levels, cn, cn), np.float32)
    w[0, 0] = r <= t
    w[1, 0] = r >= t
    for lv in range(levels):
        bs = cn >> (lv + 1)
        parent = t // (2 * bs) * (2 * bs)
        second = (t // bs) % 2 == 1
        blk = 1 + lv - (levels - small)
        if blk >= 1:
            anchor_f = parent + bs - 1
            w[0, blk] = np.where(second, (r > anchor_f) & (r <= t), (r > t) & (r <= anchor_f))
            anchor_b = parent + bs
            w[1, blk] = np.where(second, (r >= anchor_b) & (r < t), (r >= t) & (r < anchor_b))
        same_parent = (t // (2 * bs)) == (r // (2 * bs))
        key_second = (r // bs) % 2 == 1
        mask[0, lv] = same_parent & second & ~key_second
        mask[1, lv] = same_parent & ~second & key_second
    return (jnp.asarray(w.reshape(2, (1 + small) * cn, cn), BF16),
            jnp.asarray(mask.reshape(2 * levels, cn, cn), BF16))


def _hg_scan(z_l, z_c, lb, d):
    b, n, _ = z_l.shape
    n_c = z_c.shape[1]
    width = HG_HEADS_PER_STEP * LANE
    steps = d // width
    cn = HG_CHUNK
    w, mask = _hg_operators()
    col = lambda k: (lambda i, h: (i, 0, k * steps + h))
    lat = lambda k: pl.BlockSpec((1, n, width), col(k))
    ctx = lambda k: pl.BlockSpec((1, n_c, width), col(k))
    const = lambda a: pl.BlockSpec(a.shape, lambda i, h: (0,) * a.ndim)
    return pl.pallas_call(
        functools.partial(_hg_kernel, nc_c=n_c // cn, nc_l=n // cn), grid=(b, steps),
        in_specs=[lat(0), lat(1), lat(2), lat(3), ctx(1), ctx(2), ctx(3),
                  pl.BlockSpec((1, width), lambda i, h: (0, h)),
                  const(w[0]), const(w[1]), const(mask)],
        out_specs=pl.BlockSpec((1, n, width), lambda i, h: (i, 0, h)),
        out_shape=jax.ShapeDtypeStruct((b, n, d), F32),
        scratch_shapes=[pltpu.VMEM((n, width), F32),
                        pltpu.VMEM((HG_HEADS_PER_STEP, LANE, LANE), F32),
                        pltpu.VMEM((HG_HEADS_PER_STEP, LANE, LANE), F32)],
        compiler_params=_cparams(2), name="hgrn2_scan",
    )(z_l, z_l, z_l, z_l, z_c, z_c, z_c, lb.reshape(1, d), w[0], w[1], mask)


def _hg_out_kernel(o_ref, z_ref, gn_ref, x_ref, mod_ref, w_ref, g2_ref, wr_ref, out_ref, h_ref, aff_ref):
    for rows in _row_chunks(x_ref.shape[1]):
        o, gate = o_ref[0, rows, :], z_ref[0, rows, :]
        parts = []
        for h in range(o.shape[1] // LANE):
            sl = slice(h * LANE, (h + 1) * LANE)
            t = o[:, sl]
            t = t * lax.rsqrt(jnp.mean(t * t, axis=-1, keepdims=True) + EPS) * gn_ref[...]
            parts.append((t * _silu(gate[:, sl])).astype(BF16))
        y = _dot(jnp.concatenate(parts, axis=-1), w_ref[...])
        x = x_ref[0, rows, :] + mod_ref[0, 2:3, :] * y
        _router_tail(x, rows, mod_ref, g2_ref, wr_ref, out_ref, h_ref, aff_ref)


def _hg_out(o, z, gnorm, x, mod, w, g2, wr):
    b, n, d = x.shape
    tm = _row_tile(n)
    r_in, r_out, r_shape = _router_specs(b, n, d, tm)
    return pl.pallas_call(
        _hg_out_kernel, grid=(b, n // tm),
        in_specs=[pl.BlockSpec((1, tm, d), lambda i, j: (i, j, 0)),
                  pl.BlockSpec((1, tm, d), lambda i, j: (i, j, 4)),
                  pl.BlockSpec((1, LANE), lambda i, j: (0, 0)),
                  pl.BlockSpec((1, tm, d), lambda i, j: (i, j, 0)),
                  pl.BlockSpec((1, 6, d), lambda i, j: (i, 0, 0)),
                  pl.BlockSpec((d, d), lambda i, j: (0, 0))] + r_in,
        out_specs=r_out, out_shape=r_shape,
        compiler_params=_cparams(2), name="hgrn2_out",
    )(o, z, gnorm.reshape(1, LANE), x, mod, w, g2.reshape(1, d), wr)


def _route_kernel(aff_ref, tri_ref, pos_ref, selw_ref, start_ref, *, cap):
    groups = LANE // N_EXPERTS
    rg = aff_ref.shape[1] // groups
    samples = range(aff_ref.shape[0])
    denses = []
    for s in samples:
        dense = aff_ref[s, 0:rg, :]
        for g in range(1, groups):
            dense = dense + pltpu.roll(aff_ref[s, g * rg:(g + 1) * rg, :], g * N_EXPERTS, 1)
        denses.append(dense)
    all_bits = [lax.bitcast_convert_type(dense, jnp.int32) for dense in denses]
    lane = lax.broadcasted_iota(jnp.int32, (1, LANE), 1)
    tri = tri_ref[...]

    def indicator(mask):
        return jnp.where(mask, jnp.ones((), F32), jnp.zeros((), F32))

    def over_groups(row):
        for shift in (LANE // 2, LANE // 4, LANE // 8):
            row = row + pltpu.roll(row, shift, 1)
        return row

    def before_groups(row):
        out = jnp.zeros_like(row)
        for j in range(1, groups):
            out = out + jnp.where(lane >= j * N_EXPERTS, pltpu.roll(row, j * N_EXPERTS, 1), 0.0)
        return out

    def count(mask):
        return over_groups(jnp.sum(indicator(mask), axis=0, keepdims=True))

    def prefix(mask):
        x = indicator(mask)
        return _dot(tri, x.astype(BF16)) + before_groups(jnp.sum(x, axis=0, keepdims=True))

    def search(i, thrs):
        bit = jnp.left_shift(jnp.int32(1), 30 - i)
        return tuple(jnp.where(count(bits >= (thr | bit)) >= cap, thr | bit, thr) for bits, thr in zip(all_bits, thrs))

    thrs = lax.fori_loop(0, 31, search, tuple(jnp.zeros((1, LANE), jnp.int32) for _ in samples))
    for s, dense, bits, thr in zip(samples, denses, all_bits, thrs):
        above, tie = bits > thr, bits == thr
        need = cap - count(above)
        tie_rank = prefix(tie)
        sel = above | (tie & (tie_rank <= need))
        pos_ref[s] = jnp.where(sel, prefix(above) + jnp.minimum(tie_rank, need) - 1.0, -1.0)
        selw_ref[s] = jnp.where(sel, dense, 0.0)
        start_ref[s] = before_groups(jnp.sum(indicator(sel), axis=0, keepdims=True))


def _route(aff, cap):
    b, n, _ = aff.shape
    rg = n // (LANE // N_EXPERTS)
    idx = jnp.arange(rg)
    tri = (idx[None, :] <= idx[:, None]).astype(BF16)
    sb = math.gcd(b, ROUTE_SAMPLES)
    out_spec = pl.BlockSpec((sb, rg, LANE), lambda i: (i, 0, 0))
    return pl.pallas_call(
        functools.partial(_route_kernel, cap=cap), grid=(b // sb,),
        in_specs=[pl.BlockSpec((sb, n, LANE), lambda i: (i, 0, 0)), pl.BlockSpec((rg, rg), lambda i: (0, 0))],
        out_specs=[out_spec, out_spec, pl.BlockSpec((sb, 1, LANE), lambda i: (i, 0, 0))],
        out_shape=[jax.ShapeDtypeStruct((b, rg, LANE), F32)] * 2 + [jax.ShapeDtypeStruct((b, 1, LANE), F32)],
        compiler_params=_cparams(1), name="moe_route",
    )(aff, tri)


def _gather_kernel(pos_ref, h_ref, xs_ref, *, cap):
    h = h_ref[0]
    slot = lax.broadcasted_iota(jnp.int32, (cap, h.shape[0]), 0).astype(F32)
    for e in range(N_EXPERTS):
        onehot = jnp.where(pos_ref[0, e:e + 1, :] == slot, 1.0, 0.0).astype(BF16)
        xs_ref[e, 0] = _dot(onehot, h).astype(BF16)


def _gather(pos_rows, h, cap):
    b, n, d = h.shape
    return pl.pallas_call(
        functools.partial(_gather_kernel, cap=cap), grid=(b,),
        in_specs=[pl.BlockSpec((1, N_EXPERTS, n), lambda i: (i, 0, 0)),
                  pl.BlockSpec((1, n, d), lambda i: (i, 0, 0))],
        out_specs=pl.BlockSpec((N_EXPERTS, 1, cap, d), lambda i: (0, i, 0, 0)),
        out_shape=jax.ShapeDtypeStruct((N_EXPERTS, b, cap, d), BF16),
        compiler_params=_cparams(1), name="moe_gather",
    )(pos_rows, h)


def _ffn_kernel(xs_ref, wg_ref, wu_ref, wd_ref, o_ref):
    xs = xs_ref[0]
    hid = _silu(_dot(xs, wg_ref[0, 0])) * _dot(xs, wu_ref[0, 0])
    o_ref[0] = _dot(hid.astype(BF16), wd_ref[0, 0]).astype(BF16)


def _ffn(xs, wg, wu, wd, layer):
    e, m, d = xs.shape
    ff = wg.shape[3]
    tm = _row_tile(m)
    return pl.pallas_call(
        _ffn_kernel, grid=(e, m // tm),
        in_specs=[pl.BlockSpec((1, tm, d), lambda i, j: (i, j, 0)),
                  pl.BlockSpec((1, 1, d, ff), lambda i, j: (layer, i, 0, 0)),
                  pl.BlockSpec((1, 1, d, ff), lambda i, j: (layer, i, 0, 0)),
                  pl.BlockSpec((1, 1, ff, d), lambda i, j: (layer, i, 0, 0))],
        out_specs=pl.BlockSpec((1, tm, d), lambda i, j: (i, j, 0)),
        out_shape=jax.ShapeDtypeStruct((e, m, d), BF16),
        compiler_params=_cparams(2, 56), name="moe_ffn",
    )(xs, wg, wu, wd)


def _combine_kernel(*refs, cap, final, windowed):
    refs = list(refs)
    off_ref = refs.pop(0) if windowed else None
    out_ref, pos_ref, selw_ref, x_ref, mod_ref = refs[:5]
    o_ref = refs[-2] if windowed else refs[-1]
    x = x_ref[0]
    pos, selw = pos_ref[0], selw_ref[0]
    if windowed:
        shift = (LANE - pl.program_id(1) * N_EXPERTS) % LANE
        pos, selw = pltpu.roll(pos, shift, 1), pltpu.roll(selw, shift, 1)

    def all_slots():
        slot = lax.broadcasted_iota(jnp.int32, (x.shape[0], cap), 1).astype(F32)
        y = jnp.zeros(x.shape, F32)
        for e in range(N_EXPERTS):
            weighted = jnp.where(pos[:, e:e + 1] == slot, selw[:, e:e + 1], 0.0).astype(BF16)
            y = y + _dot(weighted, out_ref[e, 0])
        return y

    if windowed:
        y_ref = refs[-1]
        half = cap // 2
        i, g, last = pl.program_id(0), pl.program_id(1), pl.num_programs(1) - 1
        starts, fits = [], None
        for e in range(N_EXPERTS):
            lo = off_ref[i, g, e]
            hi = jnp.where(g < last, off_ref[i, jnp.minimum(g + 1, last), e], cap)
            w = jnp.minimum(lo // 16 * 16, cap - half)
            starts.append(w)
            ok = hi - w <= half
            fits = ok if fits is None else jnp.logical_and(fits, ok)

        @pl.when(fits)
        def _():
            pos_i = pos.astype(jnp.int32)
            slot = lax.broadcasted_iota(jnp.int32, (x.shape[0], half), 1)
            y = jnp.zeros(x.shape, F32)
            for e in range(0, N_EXPERTS, 2):
                lhs, rhs = [], []
                for k in (e, e + 1):
                    w = pl.multiple_of(starts[k], 16)
                    lhs.append(jnp.where(pos_i[:, k:k + 1] == slot + w, selw[:, k:k + 1], 0.0).astype(BF16))
                    rhs.append(out_ref[k, 0, pl.ds(w, half), :])
                y = y + _dot(jnp.concatenate(lhs, axis=1), jnp.concatenate(rhs, axis=0))
            y_ref[...] = y

        @pl.when(jnp.logical_not(fits))
        def _():
            y_ref[...] = all_slots()

        y = y_ref[...]
    else:
        y = all_slots()
    x = x + mod_ref[0, 5:6, :] * y
    if final:
        x = x * lax.rsqrt(jnp.mean(x * x, axis=-1, keepdims=True) + EPS) * refs[5][...]
    o_ref[0] = x


def _combine(out, pos, selw, starts, x, mod, cap, norm_f):
    b, n, d = x.shape
    groups = starts.shape[1]
    windowed = cap // 2 >= LANE and (n // groups) % 8 == 0
    tm = n // groups if windowed else _row_tile(n)
    final = norm_f is not None
    if windowed:
        routing = pl.BlockSpec((1, n // groups, LANE), lambda i, j, *_: (i, 0, 0))
    else:
        by_token = lambda a: (a.reshape(b, n // groups, groups, N_EXPERTS).transpose(0, 2, 1, 3)
                              .reshape(b, n, N_EXPERTS))
        pos, selw = by_token(pos), by_token(selw)
        routing = pl.BlockSpec((1, tm, N_EXPERTS), lambda i, j, *_: (i, j, 0))
    in_specs = [pl.BlockSpec((N_EXPERTS, 1, cap, d), lambda i, j, *_: (0, i, 0, 0)),
                routing, routing,
                pl.BlockSpec((1, tm, d), lambda i, j, *_: (i, j, 0)),
                pl.BlockSpec((1, 6, d), lambda i, j, *_: (i, 0, 0))]
    args = [out, pos, selw, x, mod]
    if final:
        in_specs.append(pl.BlockSpec((1, d), lambda i, j, *_: (0, 0)))
        args.append(norm_f.reshape(1, d))
    grid_spec = pltpu.PrefetchScalarGridSpec(
        num_scalar_prefetch=1 if windowed else 0, grid=(b, n // tm), in_specs=in_specs,
        out_specs=pl.BlockSpec((1, tm, d), lambda i, j, *_: (i, j, 0)),
        scratch_shapes=[pltpu.VMEM((tm, d), F32)] if windowed else [])
    return pl.pallas_call(
        functools.partial(_combine_kernel, cap=cap, final=final, windowed=windowed), grid_spec=grid_spec,
        out_shape=jax.ShapeDtypeStruct((b, n, d), F32),
        compiler_params=_cparams(2), name="moe_combine",
    )(*(([starts] if windowed else []) + args))


def _router_parts(w_router):
    w = jnp.pad(w_router.astype(F32), ((0, 0), (0, LANE - N_EXPERTS)))
    return jnp.concatenate(_split_hi_lo(w), axis=1)


def _moe(x, h, aff, mod, wg, wu, wd, layer, norm_f=None):
    b, n, d = x.shape
    cap = CAPACITY_FACTOR * n // N_EXPERTS
    groups = LANE // N_EXPERTS
    pos, selw, starts = _route(aff, cap)
    starts = starts.reshape(b, groups, N_EXPERTS).astype(jnp.int32)
    by_expert = pos.reshape(b, n // groups, groups, N_EXPERTS).transpose(0, 3, 2, 1).reshape(b, N_EXPERTS, n)
    xs = _gather(by_expert, h, cap)
    out = _ffn(xs.reshape(N_EXPERTS, b * cap, d), wg, wu, wd, layer).reshape(N_EXPERTS, b, cap, d)
    return _combine(out, pos, selw, starts, x, mod, cap, norm_f)


def kernel(x, c, ctx, c_ctx, w_mod, b_mod, norm1, norm2, s5_a_re, s5_a_im, s5_log_dt, s5_b_re, s5_b_im, s5_c_re, s5_c_im, s5_d, s5_w_glu, na_w_qkv, na_w_o, na_rpb, da_w_qkv, da_w_o, da_lambda, da_subln, hg_w_in, hg_w_o, hg_gnorm, hg_lower_bounds, moe_router, moe_w_gate, moe_w_up, moe_w_down, norm_f):
    b, n, d = x.shape
    depth = w_mod.shape[0]
    assert depth == 4, "layer i uses mixer i; only the last layer drops the context stream"

    rows_pad = -(b + 1) % 8
    cc = jnp.concatenate([c, c_ctx[None, :], jnp.zeros((rows_pad, d), F32)], axis=0)
    mod_all = _modulation(cc, w_mod, b_mod)
    p_lb = jax.nn.softmax(hg_lower_bounds.astype(F32), axis=0)
    lbs = jnp.cumsum(p_lb, axis=0) - p_lb[0]
    wg, wu, wd = moe_w_gate.astype(BF16), moe_w_up.astype(BF16), moe_w_down.astype(BF16)
    h_ctx = ctx

    for i in range(depth):
        last = i == depth - 1
        mod_l = mod_all[i, :b].reshape(b, 6, d)
        mod_c = jnp.broadcast_to(mod_all[i, b].reshape(1, 6, d), (b, 6, d))
        g1, g2 = norm1[i], norm2[i]
        wr = _router_parts(moe_router[i])
        moe_c = None
        if i == 0:
            tables = _s5_tables(s5_a_re[0], s5_a_im[0], s5_log_dt[0], s5_b_re[0], s5_b_im[0], s5_c_re[0], s5_c_im[0])
            moe_c, moe_l = _s5_layer(x, h_ctx, mod_l, mod_c, g1, tables, s5_d[0], s5_w_glu[0].astype(BF16), g2, wr)
        elif i == 1:
            w_qkv = na_w_qkv[0].astype(BF16)
            qkv_l = _proj_in(x, mod_l, g1, w_qkv, BF16, q_cols=d)
            qkv_c = _proj_in(h_ctx, mod_c, g1, w_qkv, BF16, q_cols=d)
            bias = _na_bias_table(na_rpb[0], n // GRID_W)
            w_o = na_w_o[0].astype(BF16)
            moe_l = _proj_out(_na_attention(qkv_l, qkv_c, bias, d), x, mod_l, w_o, g2, wr)
            moe_c = _proj_out(_attention(qkv_c, None, qkv_c, d, diff=False), h_ctx, mod_c, w_o, g2, wr)
        elif i == 2:
            w_qkv = da_w_qkv[0].astype(BF16)
            qkv_l = _proj_in(x, mod_l, g1, w_qkv, BF16, rope=_rope_tables(n), rope_cols=2 * d, q_cols=d)
            qkv_c = _proj_in(h_ctx, mod_c, g1, w_qkv, BF16, q_cols=d)
            lam = da_lambda[0].astype(F32)
            lam_init = 0.8 - 0.6 * math.exp(-0.3 * i)
            lam_full = jnp.exp(jnp.sum(lam[0] * lam[1])) - jnp.exp(jnp.sum(lam[2] * lam[3])) + lam_init
            w_o = da_w_o[0].astype(BF16)
            attend = functools.partial(_attention, d=d, diff=True, lam=lam_full, subln=da_subln[0],
                                       post_scale=1.0 - lam_init)
            moe_l = _proj_out(attend(qkv_l, qkv_l, qkv_c), x, mod_l, w_o, g2, wr)
            moe_c = _proj_out(attend(qkv_c, None, qkv_c), h_ctx, mod_c, w_o, g2, wr)
        else:
            w_in = hg_w_in[0].astype(BF16)
            z_l = _proj_in(x, mod_l, g1, w_in, F32)
            z_c = _proj_in(h_ctx, mod_c, g1, w_in, F32)
            moe_l = _hg_out(_hg_scan(z_l, z_c, lbs[i], d), z_l, hg_gnorm[0], x, mod_l, hg_w_o[0].astype(BF16), g2, wr)

        x = _moe(*moe_l, mod_l, wg, wu, wd, i, norm_f if last else None)
        if not last:
            h_ctx = _moe(*moe_c, mod_c, wg, wu, wd, i)
    return x
```
